```python
import jax, jax.numpy as jnp
from jax import lax
import numpy as np

D_MODEL = 2048
BATCH = 4
SEQ = 2048
DEPTH = 2
DEC_BATCH = 8
DEC_SEQ = 1
PAST_LEN = 16384
PAGE_SIZE = 128

MIX_DIM = D_MODEL
FOX_DIM = MIX_DIM // 2
FOX_HEADS = 8
FOX_HEAD_DIM = FOX_DIM // FOX_HEADS
Q_BLOCK = 128
GLA_DIM = MIX_DIM - FOX_DIM
GLA_HEADS = 4
GLA_KEY_DIM = GLA_DIM // 2
GLA_DK = GLA_KEY_DIM // GLA_HEADS
GLA_DV = GLA_DIM // GLA_HEADS
GLA_GATE_RANK = 16
GLA_TAU = 16.0
GLA_CHUNK = 64
MEM_LEN = 256
X_HEADS = 4
X_HEAD_DIM = 128
X_DIM = X_HEADS * X_HEAD_DIM
FFN_DIM = -(-8 * D_MODEL // (3 * 256)) * 256
RMS_EPS = 1e-6
FORGET_BIAS_INIT = 3.0
IN_SPLITS = (FOX_DIM, FOX_DIM, FOX_DIM, FOX_HEADS, GLA_KEY_DIM, GLA_KEY_DIM, GLA_DIM, GLA_DIM, GLA_GATE_RANK)
N_IN = 3 * FOX_DIM + FOX_HEADS + 2 * GLA_KEY_DIM + 2 * GLA_DIM + GLA_GATE_RANK

kernel_name = 'hymba_fox_gla_memxattn_decode_step'


def rmsnorm(x, g):
    xf = x.astype(jnp.float32)
    y = xf * lax.rsqrt(jnp.mean(xf * xf, axis=-1, keepdims=True) + RMS_EPS)
    return (y * g.astype(jnp.float32)).astype(x.dtype)


def mixer_inputs(h, w_in, b_f, w_g2, b_g):
    B, T, _ = h.shape
    z = h @ w_in
    idx = np.cumsum(IN_SPLITS)[:-1].tolist()
    fq, fk, fv, ff, gq, gk, gv, gr, gg = jnp.split(z, idx, axis=-1)
    fox_q = fq.reshape(B, T, FOX_HEADS, FOX_HEAD_DIM)
    fox_k = fk.reshape(B, T, FOX_HEADS, FOX_HEAD_DIM)
    fox_v = fv.reshape(B, T, FOX_HEADS, FOX_HEAD_DIM)
    fox_logf = jax.nn.log_sigmoid((ff + b_f).astype(jnp.float32))
    gla_q = gq.reshape(B, T, GLA_HEADS, GLA_DK) * (GLA_DK ** -0.5)
    gla_k = gk.reshape(B, T, GLA_HEADS, GLA_DK)
    gla_v = gv.reshape(B, T, GLA_HEADS, GLA_DV)
    gla_r = gr.reshape(B, T, GLA_HEADS, GLA_DV)
    gla_logalpha = (jax.nn.log_sigmoid((gg @ w_g2 + b_g).astype(jnp.float32)) / GLA_TAU).reshape(B, T, GLA_HEADS, GLA_DK)
    return fox_q, fox_k, fox_v, fox_logf, gla_q, gla_k, gla_v, gla_r, gla_logalpha


def fox_prompt(q, k, v, logf):
    B, S, H, dh = q.shape
    dT = jnp.cumsum(logf.astype(jnp.float32), axis=1).transpose(0, 2, 1)
    kpos = jnp.arange(S)
    scale = dh ** -0.5

    def block(i):
        start = i * Q_BLOCK
        qb = lax.dynamic_slice_in_dim(q, start, Q_BLOCK, axis=1)
        db = lax.dynamic_slice_in_dim(dT, start, Q_BLOCK, axis=2)
        s = jnp.einsum('bqhd,bkhd->bhqk', qb, k, preferred_element_type=jnp.float32) * scale
        s = s + (db[..., None] - dT[:, :, None, :])
        qpos = start + jnp.arange(Q_BLOCK)
        s = jnp.where(kpos[None, :] <= qpos[:, None], s, -jnp.inf)
        p = jax.nn.softmax(s, axis=-1).astype(v.dtype)
        return jnp.einsum('bhqk,bkhd->bqhd', p, v)

    out = lax.map(block, jnp.arange(S // Q_BLOCK))
    return out.swapaxes(0, 1).reshape(B, S, H, dh)


def fox_sample(q, k_new, v_new, logf_new, k_past, v_past, logf_past):
    dh = q.shape[-1]
    T = q.shape[1]
    P = k_past.shape[1]
    scale = dh ** -0.5
    c_past = jnp.cumsum(logf_past.astype(jnp.float32), axis=1)
    tail = c_past[:, -1:] - c_past
    c_new = jnp.cumsum(logf_new.astype(jnp.float32), axis=1)
    cq = c_new.transpose(0, 2, 1)
    s_past = jnp.einsum('bthd,bshd->bhts', q, k_past, preferred_element_type=jnp.float32) * scale
    s_past = s_past + cq[..., None] + tail.transpose(0, 2, 1)[:, :, None, :]
    s_new = jnp.einsum('bthd,bshd->bhts', q, k_new, preferred_element_type=jnp.float32) * scale
    s_new = s_new + cq[..., None] - cq[:, :, None, :]
    causal = jnp.tril(jnp.ones((T, T), dtype=bool))
    s_new = jnp.where(causal, s_new, -jnp.inf)
    p = jax.nn.softmax(jnp.concatenate([s_past, s_new], axis=-1), axis=-1).astype(v_new.dtype)
    return (jnp.einsum('bhts,bshd->bthd', p[..., :P], v_past)
            + jnp.einsum('bhts,bshd->bthd', p[..., P:], v_new))


def gla_chunk(S0, q, k, v, g):
    q, k, v, g = (a.astype(jnp.float32) for a in (q, k, v, g))
    C = q.shape[1]
    b = jnp.cumsum(g, axis=1)
    causal = jnp.tril(jnp.ones((C, C), dtype=bool))
    diff = b[:, :, None] - b[:, None, :]
    decay = jnp.exp(jnp.where(causal[None, :, :, None, None], diff, -jnp.inf))
    A = jnp.sum(q[:, :, None] * k[:, None] * decay, axis=-1)
    o = (jnp.einsum('btsh,bshv->bthv', A, v)
         + jnp.einsum('bthd,bhdv->bthv', q * jnp.exp(b), S0))
    b_last = b[:, -1]
    S_new = (jnp.exp(b_last)[..., None] * S0
             + jnp.einsum('bshd,bshv->bhdv', k * jnp.exp(b_last[:, None] - b), v))
    return S_new, o


def gla_prompt(q, k, v, g):
    B, S, H, DK = q.shape
    nc = S // GLA_CHUNK

    def to_chunks(a):
        return a.reshape(B, nc, GLA_CHUNK, *a.shape[2:]).swapaxes(0, 1)

    S0 = jnp.zeros((B, H, DK, v.shape[-1]), jnp.float32)

    def step(S_, xs):
        return gla_chunk(S_, *xs)

    S_fin, o = lax.scan(step, S0, (to_chunks(q), to_chunks(k), to_chunks(v), to_chunks(g)))
    return S_fin, o.swapaxes(0, 1).reshape(B, S, H, v.shape[-1])


def mixer_output(o_fox, o_gla, gla_r, g_head, w_out, dtype):
    B, T = o_fox.shape[:2]
    o_g = rmsnorm(o_gla, g_head) * jax.nn.silu(gla_r.astype(jnp.float32))
    cat = jnp.concatenate([o_fox.reshape(B, T, FOX_DIM).astype(dtype),
                           o_g.reshape(B, T, GLA_DIM).astype(dtype)], axis=-1)
    return cat @ w_out


def memory_kv(mem, g_mem, w_k, w_v):
    B, M, _ = mem.shape
    m = rmsnorm(mem, g_mem)
    return (m @ w_k).reshape(B, M, X_HEADS, X_HEAD_DIM), (m @ w_v).reshape(B, M, X_HEADS, X_HEAD_DIM)


def cross_attn(h, mk, mv, w_q, w_o):
    B, T, _ = h.shape
    q = (h @ w_q).reshape(B, T, X_HEADS, X_HEAD_DIM)
    s = jnp.einsum('bthd,bmhd->bhtm', q, mk, preferred_element_type=jnp.float32) * (X_HEAD_DIM ** -0.5)
    p = jax.nn.softmax(s, axis=-1).astype(mv.dtype)
    o = jnp.einsum('bhtm,bmhd->bthd', p, mv).reshape(B, T, X_DIM)
    return o @ w_o


def swiglu(h, w_gate, w_up, w_down):
    return (jax.nn.silu(h @ w_gate) * (h @ w_up)) @ w_down


def setup_inputs(seed: int = 0) -> dict:
    key = jax.random.key(seed)
    ks = iter(jax.random.split(key, 40))

    def nrm(shape, scale):
        return jax.random.normal(next(ks), shape, jnp.float32) * scale

    def gain(shape):
        return 1.0 + nrm(shape, 0.02)

    n_pages = PAST_LEN // PAGE_SIZE
    n_used = DEC_BATCH * n_pages
    n_pool = (5 * n_used + 3) // 4
    x_prompt = nrm((BATCH, SEQ, D_MODEL), 1.0)
    x_sample = nrm((DEC_BATCH, DEC_SEQ, D_MODEL), 1.0)
    mem_prompt = nrm((BATCH, MEM_LEN, D_MODEL), 1.0)
    cache_fox_k = nrm((DEPTH, n_pool, PAGE_SIZE, FOX_HEADS, FOX_HEAD_DIM), 1.0)
    cache_fox_v = nrm((DEPTH, n_pool, PAGE_SIZE, FOX_HEADS, FOX_HEAD_DIM), 1.0)
    cache_fox_logf = jax.nn.log_sigmoid(FORGET_BIAS_INIT + nrm((DEPTH, n_pool, PAGE_SIZE, FOX_HEADS), 1.0))
    state_gla = nrm((DEPTH, DEC_BATCH, GLA_HEADS, GLA_DK, GLA_DV), 1.0)
    cache_mem_k = nrm((DEPTH, DEC_BATCH, MEM_LEN, X_HEADS, X_HEAD_DIM), 1.0)
    cache_mem_v = nrm((DEPTH, DEC_BATCH, MEM_LEN, X_HEADS, X_HEAD_DIM), 1.0)
    page_table = jax.random.permutation(next(ks), n_pool)[:n_used].reshape(DEC_BATCH, n_pages).astype(jnp.int32)
    return {
        'x_prompt': x_prompt,
        'x_sample': x_sample,
        'mem_prompt': mem_prompt,
        'cache_fox_k': cache_fox_k,
        'cache_fox_v': cache_fox_v,
        'cache_fox_logf': cache_fox_logf,
        'state_gla': state_gla,
        'cache_mem_k': cache_mem_k,
        'cache_mem_v': cache_mem_v,
        'page_table': page_table,
        'g_mix': gain((DEPTH, D_MODEL)),
        'w_in': nrm((DEPTH, D_MODEL, N_IN), D_MODEL ** -0.5),
        'b_forget': FORGET_BIAS_INIT + nrm((DEPTH, FOX_HEADS), 0.1),
        'w_gla_gate': nrm((DEPTH, GLA_GATE_RANK, GLA_KEY_DIM), GLA_GATE_RANK ** -0.5),
        'b_gla_gate': nrm((DEPTH, GLA_KEY_DIM), 0.01),
        'g_gla_head': gain((DEPTH, GLA_DV)),
        'w_out': nrm((DEPTH, MIX_DIM, D_MODEL), MIX_DIM ** -0.5),
        'g_cross': gain((DEPTH, D_MODEL)),
        'g_mem': gain((DEPTH, D_MODEL)),
        'w_xq': nrm((DEPTH, D_MODEL, X_DIM), D_MODEL ** -0.5),
        'w_xk': nrm((DEPTH, D_MODEL, X_DIM), D_MODEL ** -0.5),
        'w_xv': nrm((DEPTH, D_MODEL, X_DIM), D_MODEL ** -0.5),
        'w_xo': nrm((DEPTH, X_DIM, D_MODEL), X_DIM ** -0.5),
        'g_ffn': gain((DEPTH, D_MODEL)),
        'w_ffn_gate': nrm((DEPTH, D_MODEL, FFN_DIM), D_MODEL ** -0.5),
        'w_ffn_up': nrm((DEPTH, D_MODEL, FFN_DIM), D_MODEL ** -0.5),
        'w_ffn_down': nrm((DEPTH, FFN_DIM, D_MODEL), FFN_DIM ** -0.5),
        'g_final': gain((D_MODEL,)),
    }


def reference(x_prompt, x_sample, mem_prompt, cache_fox_k, cache_fox_v, cache_fox_logf, state_gla,
              cache_mem_k, cache_mem_v, page_table, g_mix, w_in, b_forget, w_gla_gate, b_gla_gate,
              g_gla_head, w_out, g_cross, g_mem, w_xq, w_xk, w_xv, w_xo, g_ffn, w_ffn_gate, w_ffn_up,
              w_ffn_down, g_final):
    xp = x_prompt
    xs = x_sample
    Bs = x_sample.shape[0]
    past = page_table.shape[1] * cache_fox_k.shape[2]
    pk, pv, pf, pS, pmk, pmv = [], [], [], [], [], []
    sk, sv, sf, sS = [], [], [], []
    for l in range(DEPTH):
        h = rmsnorm(xp, g_mix[l])
        fq, fk, fv, flogf, gq, gk, gv, gr, glog = mixer_inputs(h, w_in[l], b_forget[l], w_gla_gate[l], b_gla_gate[l])
        o_f = fox_prompt(fq, fk, fv, flogf)
        S_fin, o_g = gla_prompt(gq, gk, gv, glog)
        xp = xp + mixer_output(o_f, o_g, gr, g_gla_head[l], w_out[l], xp.dtype)
        mk, mv = memory_kv(mem_prompt, g_mem[l], w_xk[l], w_xv[l])
        xp = xp + cross_attn(rmsnorm(xp, g_cross[l]), mk, mv, w_xq[l], w_xo[l])
        xp = xp + swiglu(rmsnorm(xp, g_ffn[l]), w_ffn_gate[l], w_ffn_up[l], w_ffn_down[l])
        pk.append(fk); pv.append(fv); pf.append(flogf); pS.append(S_fin); pmk.append(mk); pmv.append(mv)

        h = rmsnorm(xs, g_mix[l])
        fq, fk, fv, flogf, gq, gk, gv, gr, glog = mixer_inputs(h, w_in[l], b_forget[l], w_gla_gate[l], b_gla_gate[l])
        k_past = cache_fox_k[l, page_table].reshape(Bs, past, FOX_HEADS, FOX_HEAD_DIM)
        v_past = cache_fox_v[l, page_table].reshape(Bs, past, FOX_HEADS, FOX_HEAD_DIM)
        f_past = cache_fox_logf[l, page_table].reshape(Bs, past, FOX_HEADS)
        o_f = fox_sample(fq, fk, fv, flogf, k_past, v_past, f_past)
        S_new, o_g = gla_chunk(state_gla[l].astype(jnp.float32), gq, gk, gv, glog)
        xs = xs + mixer_output(o_f, o_g, gr, g_gla_head[l], w_out[l], xs.dtype)
        xs = xs + cross_attn(rmsnorm(xs, g_cross[l]), cache_mem_k[l], cache_mem_v[l], w_xq[l], w_xo[l])
        xs = xs + swiglu(rmsnorm(xs, g_ffn[l]), w_ffn_gate[l], w_ffn_up[l], w_ffn_down[l])
        sk.append(fk); sv.append(fv); sf.append(flogf); sS.append(S_new)

    y_prompt = rmsnorm(xp, g_final)
    y_sample = rmsnorm(xs, g_final)
    return (y_prompt, y_sample,
            jnp.stack(pk), jnp.stack(pv), jnp.stack(pf), jnp.stack(pS), jnp.stack(pmk), jnp.stack(pmv),
            jnp.stack(sk), jnp.stack(sv), jnp.stack(sf), jnp.stack(sS))
```

```python
import functools

import numpy as np
import jax
import jax.numpy as jnp
from jax import lax
from jax.experimental import pallas as pl
from jax.experimental.pallas import tpu as pltpu

F32 = jnp.float32
BF16 = jnp.bfloat16

RMS_EPS = 1e-6
GLA_TAU = 16.0
GLA_CHUNK = 64
LANES = 128
SUBLANES = 8
VMEM_LIMIT_BYTES = 56 * 1024 * 1024
ROW_TILE_PROJ = 1024
ROW_TILE_FFN = 512
COL_TILE = 512
FOX_Q_TILE = 256


def _params(*sem):
    return pltpu.CompilerParams(dimension_semantics=sem, vmem_limit_bytes=VMEM_LIMIT_BYTES)


def _rms(x, g):
    return x * lax.rsqrt(jnp.mean(x * x, axis=-1, keepdims=True) + RMS_EPS) * g


def _log_sigmoid(x):
    return jnp.minimum(x, 0.0) - jnp.log1p(jnp.exp(-jnp.abs(x)))


def _silu(x):
    return x / (1.0 + jnp.exp(-x))


def _dot(a, b):
    return jnp.dot(a, b, preferred_element_type=F32)


def _dot_nt(a, b):
    return lax.dot_general(a, b, (((1,), (1,)), ((), ())), preferred_element_type=F32)


def _dot_tn(a, b):
    return lax.dot_general(a, b, (((0,), (0,)), ((), ())), preferred_element_type=F32)


def _split3(x):
    a = x.astype(BF16)
    r = x - a.astype(F32)
    b = r.astype(BF16)
    c = (r - b.astype(F32)).astype(BF16)
    return a, b, c


def _dot3(m, x, dot=_dot):
    a, b, c = _split3(x)
    return dot(m, a) + dot(m, b) + dot(m, c)


def _in_proj_kernel(x_ref, g_ref, w_ref, ws_ref, bf_ref, wg2_ref, bg_ref, *rest, n_fox_heads, tiles_per_seq):
    if tiles_per_seq:
        z_ref, logf_ref, ga_ref, drow_ref, hb_ref, carry_ref = rest
    else:
        z_ref, logf_ref, ga_ref, hb_ref = rest
    i = pl.program_id(0)
    j = pl.program_id(1)

    @pl.when(j == 0)
    def _():
        hb = _rms(x_ref[...], g_ref[...]).astype(BF16)
        hb_ref[...] = hb
        zs = _dot(hb, ws_ref[...])
        lf = _log_sigmoid(zs + bf_ref[...])
        logf_ref[...] = lf[:, :n_fox_heads]
        gp = _dot(zs.astype(BF16), wg2_ref[...]) + bg_ref[...]
        ga_ref[...] = _log_sigmoid(gp) * (1.0 / GLA_TAU)
        if tiles_per_seq:
            tm = lf.shape[0]
            r = lax.broadcasted_iota(jnp.int32, (tm, tm), 0)
            c = lax.broadcasted_iota(jnp.int32, (tm, tm), 1)
            tri = jnp.where(r <= c, 1.0, 0.0).astype(BF16)
            prev = jnp.where(i % tiles_per_seq == 0, 0.0, carry_ref[:, :1])
            d = _dot3(tri, lf.T, dot=lambda m, x: _dot(x, m)) + prev
            carry_ref[...] = jnp.broadcast_to(d[:, tm - 1:tm], carry_ref.shape)
            drow_ref[...] = d[:SUBLANES, :]

    z_ref[...] = _dot(hb_ref[...], w_ref[...])


def _in_proj(x, g, w_main, w_small, b_f, w_g2, b_g, *, n_fox_heads, seq, tm, tn):
    n, d = x.shape
    nz = w_main.shape[1]
    kg = w_g2.shape[1]
    tiles_per_seq = seq // tm if seq else 0
    out_shape = [jax.ShapeDtypeStruct((n, nz), F32),
                 jax.ShapeDtypeStruct((n, n_fox_heads), F32),
                 jax.ShapeDtypeStruct((n, kg), F32)]
    out_specs = [pl.BlockSpec((tm, tn), lambda i, j: (i, j)),
                 pl.BlockSpec((tm, n_fox_heads), lambda i, j: (i, 0)),
                 pl.BlockSpec((tm, kg), lambda i, j: (i, 0))]
    scratch = [pltpu.VMEM((tm, d), BF16)]
    if tiles_per_seq:
        out_shape.append(jax.ShapeDtypeStruct((n // seq, SUBLANES, seq), F32))
        out_specs.append(pl.BlockSpec((None, SUBLANES, tm), lambda i, j: (i // tiles_per_seq, 0, i % tiles_per_seq)))
        scratch.append(pltpu.VMEM((LANES, LANES), F32))
    return pl.pallas_call(
        functools.partial(_in_proj_kernel, n_fox_heads=n_fox_heads, tiles_per_seq=tiles_per_seq),
        out_shape=out_shape,
        grid=(n // tm, nz // tn),
        in_specs=[pl.BlockSpec((tm, d), lambda i, j: (i, 0)),
                  pl.BlockSpec((1, d), lambda i, j: (0, 0)),
                  pl.BlockSpec((d, tn), lambda i, j: (0, j)),
                  pl.BlockSpec((d, LANES), lambda i, j: (0, 0)),
                  pl.BlockSpec((1, LANES), lambda i, j: (0, 0)),
                  pl.BlockSpec((LANES, kg), lambda i, j: (0, 0)),
                  pl.BlockSpec((1, kg), lambda i, j: (0, 0))],
        out_specs=out_specs,
        scratch_shapes=scratch,
        compiler_params=_params("arbitrary", "arbitrary"),
        name="in_proj",
    )(x, g, w_main, w_small, b_f, w_g2, b_g)


def _norm_matmul_kernel(x_ref, g_ref, w_ref, o_ref, hb_ref):
    @pl.when(pl.program_id(1) == 0)
    def _():
        hb_ref[...] = _rms(x_ref[...], g_ref[...]).astype(BF16)

    o_ref[...] = _dot(hb_ref[...], w_ref[...])


def _norm_matmul(x, g, w, *, tm, tn):
    n, d = x.shape
    nout = w.shape[1]
    return pl.pallas_call(
        _norm_matmul_kernel,
        out_shape=jax.ShapeDtypeStruct((n, nout), F32),
        grid=(n // tm, nout // tn),
        in_specs=[pl.BlockSpec((tm, d), lambda i, j: (i, 0)),
                  pl.BlockSpec((1, d), lambda i, j: (0, 0)),
                  pl.BlockSpec((d, tn), lambda i, j: (0, j))],
        out_specs=pl.BlockSpec((tm, tn), lambda i, j: (i, j)),
        scratch_shapes=[pltpu.VMEM((tm, d), BF16)],
        compiler_params=_params("arbitrary", "arbitrary"),
        name="norm_matmul",
    )(x, g, w)


def _fox_prompt_kernel(q_ref, k_ref, v_ref, d_ref, o_ref, kb_ref, vb_ref, *, tq, scale):
    h = pl.program_id(1)
    qi = pl.program_id(2)

    @pl.when(qi == 0)
    def _():
        kb_ref[...] = k_ref[...].astype(BF16)
        vb_ref[...] = v_ref[...].astype(BF16)

    q = (q_ref[...] * scale).astype(BF16)

    def block(j, carry, masked):
        m, l, acc = carry
        start = pl.multiple_of(j * tq, tq)
        kj = kb_ref[pl.ds(start, tq), :]
        vj = vb_ref[pl.ds(start, tq), :]
        s = _dot_nt(q, kj) - d_ref[pl.ds(h, 1), pl.ds(start, tq)]
        if masked:
            r = lax.broadcasted_iota(jnp.int32, (tq, tq), 0)
            c = lax.broadcasted_iota(jnp.int32, (tq, tq), 1)
            s = jnp.where(c <= r, s, -jnp.inf)
        m_new = jnp.maximum(m, jnp.max(s, axis=-1, keepdims=True))
        a = jnp.exp(m - m_new)
        p = jnp.exp(s - m_new)
        l = a * l + jnp.sum(p, axis=-1, keepdims=True)
        acc = a * acc + _dot(p.astype(BF16), vj)
        return m_new, l, acc

    dh = q.shape[1]
    init = (jnp.full((tq, 1), -jnp.inf, F32), jnp.zeros((tq, 1), F32), jnp.zeros((tq, dh), F32))
    carry = lax.fori_loop(0, qi, lambda j, c: block(j, c, False), init)
    _, l, acc = block(qi, carry, True)
    o_ref[...] = (acc / l).astype(o_ref.dtype)


def _fox_prompt(z, drow, *, batch, seq, heads, dh, tq):
    n = z.shape[0]
    nq = seq // tq
    return pl.pallas_call(
        functools.partial(_fox_prompt_kernel, tq=tq, scale=dh ** -0.5),
        out_shape=jax.ShapeDtypeStruct((n, heads * dh), BF16),
        grid=(batch, heads, nq),
        in_specs=[pl.BlockSpec((tq, dh), lambda b, h, i: (b * nq + i, h)),
                  pl.BlockSpec((seq, dh), lambda b, h, i: (b, heads + h)),
                  pl.BlockSpec((seq, dh), lambda b, h, i: (b, 2 * heads + h)),
                  pl.BlockSpec((None, SUBLANES, seq), lambda b, h, i: (b, 0, 0))],
        out_specs=pl.BlockSpec((tq, dh), lambda b, h, i: (b * nq + i, h)),
        scratch_shapes=[pltpu.VMEM((seq, dh), BF16), pltpu.VMEM((seq, dh), BF16)],
        compiler_params=_params("arbitrary", "arbitrary", "arbitrary"),
        name="fox_prompt",
    )(z, z, z, drow)


def _gla_tables(c):
    r = np.arange(c)
    tri = (r[None, :] <= r[:, None]).astype(np.float32)
    sums, masks = [tri], [np.eye(c, dtype=np.float32)]
    m = c // 2
    while m >= 1:
        mid = (r // (2 * m)) * (2 * m) + m - 1
        sums.append(tri[mid])
        same = (r[:, None] // (2 * m)) == (r[None, :] // (2 * m))
        upper = (r[:, None] % (2 * m)) >= m
        lower = (r[None, :] % (2 * m)) < m
        masks.append((same & upper & lower).astype(np.float32))
        m //= 2
    return np.concatenate(sums, 0), np.concatenate(masks, 0)


def _gla_kernel(q_ref, k_ref, v_ref, r_ref, ga_ref, s0_ref, sums_ref, masks_ref, gh_ref,
                og_ref, sfin_ref, st_ref, *, heads, dk, dv, scale):
    ci = pl.program_id(1)

    @pl.when(ci == 0)
    def _():
        st_ref[...] = s0_ref[...]

    c = q_ref.shape[0]
    n_levels = sums_ref.shape[0] // c - 1
    sums = sums_ref[...]
    for h in range(heads):
        q = q_ref[:, h * dk:(h + 1) * dk] * scale
        k = k_ref[:, h * dk:(h + 1) * dk]
        v = v_ref[:, h * dv:(h + 1) * dv].astype(BF16)
        g = ga_ref[:, h * dk:(h + 1) * dk]
        gs = _dot3(sums, g)
        b = gs[:c]
        a = jnp.where(masks_ref[0:c, :] != 0, _dot_nt(q.astype(BF16), k.astype(BF16)), 0.0)
        for lv in range(n_levels):
            d = b - gs[(lv + 1) * c:(lv + 2) * c]
            qt = (q * jnp.exp(jnp.minimum(d, 0.0))).astype(BF16)
            kt = (k * jnp.exp(jnp.minimum(-d, 0.0))).astype(BF16)
            a = a + jnp.where(masks_ref[(lv + 1) * c:(lv + 2) * c, :] != 0, _dot_nt(qt, kt), 0.0)
        s_old = st_ref[h]
        o = _dot(a.astype(BF16), v) + _dot((q * jnp.exp(b)).astype(BF16), s_old.astype(BF16))
        b_last = b[c - 1:c, :]
        upd = _dot_tn((k * jnp.exp(b_last - b)).astype(BF16), v)
        col = jnp.broadcast_to(jnp.exp(b_last), (dk, dk)).T
        st_ref[h] = jnp.concatenate([col] * (dv // dk), axis=1) * s_old + upd
        y = _rms(o, gh_ref[...])
        og_ref[:, h * dv:(h + 1) * dv] = (y * _silu(r_ref[:, h * dv:(h + 1) * dv])).astype(og_ref.dtype)

    @pl.when(ci == pl.num_programs(1) - 1)
    def _():
        sfin_ref[...] = st_ref[...]


def _gla(z, ga, s0, g_head, *, batch, seq, heads, dk, dv, col0):
    n = z.shape[0]
    c = GLA_CHUNK
    nc = seq // c
    wk, wv = heads * dk, heads * dv
    sums, masks = _gla_tables(c)
    row = lambda b, i: b * nc + i
    return pl.pallas_call(
        functools.partial(_gla_kernel, heads=heads, dk=dk, dv=dv, scale=dk ** -0.5),
        out_shape=[jax.ShapeDtypeStruct((n, wv), BF16), jax.ShapeDtypeStruct((batch, heads, dk, dv), F32)],
        grid=(batch, nc),
        in_specs=[pl.BlockSpec((c, wk), lambda b, i: (row(b, i), col0 // wk)),
                  pl.BlockSpec((c, wk), lambda b, i: (row(b, i), col0 // wk + 1)),
                  pl.BlockSpec((c, wv), lambda b, i: (row(b, i), (col0 + 2 * wk) // wv)),
                  pl.BlockSpec((c, wv), lambda b, i: (row(b, i), (col0 + 2 * wk) // wv + 1)),
                  pl.BlockSpec((c, wk), lambda b, i: (row(b, i), 0)),
                  pl.BlockSpec((None, heads, dk, dv), lambda b, i: (b, 0, 0, 0)),
                  pl.BlockSpec(sums.shape, lambda b, i: (0, 0)),
                  pl.BlockSpec(masks.shape, lambda b, i: (0, 0)),
                  pl.BlockSpec((1, dv), lambda b, i: (0, 0))],
        out_specs=[pl.BlockSpec((c, wv), lambda b, i: (row(b, i), 0)),
                   pl.BlockSpec((None, heads, dk, dv), lambda b, i: (b, 0, 0, 0))],
        scratch_shapes=[pltpu.VMEM((heads, dk, dv), F32)],
        compiler_params=_params("arbitrary", "arbitrary"),
        name="gla",
    )(z, z, z, z, ga, s0, jnp.asarray(sums, BF16), jnp.asarray(masks, F32), g_head)


def _out_proj_kernel(a_ref, b_ref, w1_ref, w2_ref, x_ref, o_ref):
    o_ref[...] = x_ref[...] + _dot(a_ref[...], w1_ref[...]) + _dot(b_ref[...], w2_ref[...])


def _out_proj(a, b, w, x, *, tm, tn):
    n, d = x.shape
    kh = a.shape[1]
    return pl.pallas_call(
        _out_proj_kernel,
        out_shape=jax.ShapeDtypeStruct((n, d), F32),
        grid=(n // tm, d // tn),
        in_specs=[pl.BlockSpec((tm, kh), lambda i, j: (i, 0)),
                  pl.BlockSpec((tm, kh), lambda i, j: (i, 0)),
                  pl.BlockSpec((kh, tn), lambda i, j: (0, j)),
                  pl.BlockSpec((kh, tn), lambda i, j: (1, j)),
                  pl.BlockSpec((tm, tn), lambda i, j: (i, j))],
        out_specs=pl.BlockSpec((tm, tn), lambda i, j: (i, j)),
        compiler_params=_params("arbitrary", "arbitrary"),
        name="out_proj",
    )(a, b, w, w, x)


def _cross_attn_kernel(x_ref, g_ref, wq_ref, mk_ref, mv_ref, wo_ref, o_ref, *, heads, dh):
    x = x_ref[...]
    tm, d = x.shape
    xr = x if tm >= SUBLANES else jnp.broadcast_to(x, (SUBLANES, d))
    q = _dot(_rms(xr, g_ref[...]).astype(BF16), wq_ref[...]) * (dh ** -0.5)
    outs = []
    for h in range(heads):
        sl = slice(h * dh, (h + 1) * dh)
        s = _dot_nt(q[:, sl].astype(BF16), mk_ref[:, sl].astype(BF16))
        p = jnp.exp(s - jnp.max(s, axis=-1, keepdims=True))
        p = p / jnp.sum(p, axis=-1, keepdims=True)
        outs.append(_dot(p.astype(BF16), mv_ref[:, sl].astype(BF16)))
    y = _dot(jnp.concatenate(outs, axis=1).astype(BF16), wo_ref[...])
    o_ref[...] = x + y[:tm]


def _cross_attn(x3, g, wq, mk, mv, wo, *, heads, dh, tm):
    bsz, rows, d = x3.shape
    mem = mk.shape[1]
    xd = heads * dh
    return pl.pallas_call(
        functools.partial(_cross_attn_kernel, heads=heads, dh=dh),
        out_shape=jax.ShapeDtypeStruct(x3.shape, F32),
        grid=(bsz, rows // tm),
        in_specs=[pl.BlockSpec((None, tm, d), lambda b, i: (b, i, 0)),
                  pl.BlockSpec((1, d), lambda b, i: (0, 0)),
                  pl.BlockSpec((d, xd), lambda b, i: (0, 0)),
                  pl.BlockSpec((None, mem, xd), lambda b, i: (b, 0, 0)),
                  pl.BlockSpec((None, mem, xd), lambda b, i: (b, 0, 0)),
                  pl.BlockSpec((xd, d), lambda b, i: (0, 0))],
        out_specs=pl.BlockSpec((None, tm, d), lambda b, i: (b, i, 0)),
        compiler_params=_params("arbitrary", "arbitrary"),
        name="cross_attn",
    )(x3, g, wq, mk, mv, wo)


def _swiglu_kernel(x_ref, g_ref, wg_ref, wu_ref, wd_ref, gf_ref, o_ref, hb_ref, acc_ref, *, final_norm):
    f = pl.program_id(1)

    @pl.when(f == 0)
    def _():
        hb_ref[...] = _rms(x_ref[...], g_ref[...]).astype(BF16)
        acc_ref[...] = jnp.zeros_like(acc_ref)

    hb = hb_ref[...]
    act = _silu(_dot(hb, wg_ref[...])) * _dot(hb, wu_ref[...])
    acc_ref[...] += _dot(act.astype(BF16), wd_ref[...])

    @pl.when(f == pl.num_programs(1) - 1)
    def _():
        y = x_ref[...] + acc_ref[...]
        o_ref[...] = _rms(y, gf_ref[...]) if final_norm else y


def _swiglu(x, g, wg, wu, wd, g_final, *, final_norm, tm, tf):
    n, d = x.shape
    ffn = wg.shape[1]
    return pl.pallas_call(
        functools.partial(_swiglu_kernel, final_norm=final_norm),
        out_shape=jax.ShapeDtypeStruct((n, d), F32),
        grid=(n // tm, ffn // tf),
        in_specs=[pl.BlockSpec((tm, d), lambda i, f: (i, 0)),
                  pl.BlockSpec((1, d), lambda i, f: (0, 0)),
                  pl.BlockSpec((d, tf), lambda i, f: (0, f)),
                  pl.BlockSpec((d, tf), lambda i, f: (0, f)),
                  pl.BlockSpec((tf, d), lambda i, f: (f, 0)),
                  pl.BlockSpec((1, d), lambda i, f: (0, 0))],
        out_specs=pl.BlockSpec((tm, d), lambda i, f: (i, 0)),
        scratch_shapes=[pltpu.VMEM((tm, d), BF16), pltpu.VMEM((tm, d), F32)],
        compiler_params=_params("arbitrary", "arbitrary"),
        name="swiglu",
    )(x, g, wg, wu, wd, g_final)


def _fox_decode_kernel(pt_ref, q_ref, kn_ref, vn_ref, cq_ref, ck_ref, cv_ref, cf_ref, sfx_ref, o_ref,
                       m_ref, l_ref, acc_ref, tail_ref, *, heads, scale):
    del pt_ref
    p = pl.program_id(1)

    @pl.when(p == 0)
    def _():
        m_ref[...] = jnp.full_like(m_ref, -jnp.inf)
        l_ref[...] = jnp.zeros_like(l_ref)
        acc_ref[...] = jnp.zeros_like(acc_ref)
        tail_ref[...] = jnp.zeros_like(tail_ref)

    page = cf_ref.shape[1]
    q = q_ref[...]
    qb = q.astype(BF16)
    sub = lax.broadcasted_iota(jnp.int32, (heads, page), 0)
    lf = cf_ref[...]
    bias = cq_ref[...] + tail_ref[...] + _dot3(sfx_ref[...], lf, dot=lambda m, x: _dot(x, m))
    tail_ref[...] += jnp.sum(lf, axis=-1, keepdims=True)
    s = jnp.zeros((heads, page), F32)
    for h in range(heads):
        kh = ck_ref[pl.ds(h, page, stride=heads), :].astype(BF16)
        s = s + jnp.where(sub == h, _dot_nt(qb, kh), 0.0)
    s = s * scale + bias
    m_old = m_ref[...]
    m_new = jnp.maximum(m_old, jnp.max(s, axis=-1, keepdims=True))
    a = jnp.exp(m_old - m_new)
    pr = jnp.exp(s - m_new)
    l_ref[...] = a * l_ref[...] + jnp.sum(pr, axis=-1, keepdims=True)
    m_ref[...] = m_new
    prb = pr.astype(BF16)
    pv = jnp.zeros(acc_ref.shape, F32)
    for h in range(heads):
        vh = cv_ref[pl.ds(h, page, stride=heads), :].astype(BF16)
        pv = pv + jnp.where(sub == h, _dot(prb, vh), 0.0)
    acc_ref[...] = a * acc_ref[...] + pv

    @pl.when(p == pl.num_programs(1) - 1)
    def _():
        s_new = jnp.sum(q * kn_ref[...], axis=-1, keepdims=True) * scale
        m_fin = jnp.maximum(m_ref[...], s_new)
        a_fin = jnp.exp(m_ref[...] - m_fin)
        p_new = jnp.exp(s_new - m_fin)
        o_ref[...] = ((a_fin * acc_ref[...] + p_new * vn_ref[...]) / (a_fin * l_ref[...] + p_new)).astype(o_ref.dtype)


def _fox_decode(q, k_new, v_new, cq, cache_k, cache_v, cache_ft, page_table, *, layer):
    bsz, heads, dh = q.shape
    page = cache_ft.shape[-1]
    n_pages = page_table.shape[1]
    r = np.arange(page)
    suffix = jnp.asarray(r[:, None] > r[None, :], BF16)
    vec = pl.BlockSpec((None, heads, dh), lambda b, p, pt: (b, 0, 0))
    paged = lambda rows, cols: pl.BlockSpec((None, None, rows, cols),
                                            lambda b, p, pt: (layer, pt[b, n_pages - 1 - p], 0, 0))
    grid_spec = pltpu.PrefetchScalarGridSpec(
        num_scalar_prefetch=1,
        grid=(bsz, n_pages),
        in_specs=[vec, vec, vec, pl.BlockSpec((None, heads, LANES), lambda b, p, pt: (b, 0, 0)),
                  paged(page * heads, dh), paged(page * heads, dh), paged(heads, page),
                  pl.BlockSpec((page, page), lambda b, p, pt: (0, 0))],
        out_specs=vec,
        scratch_shapes=[pltpu.VMEM((heads, 1), F32), pltpu.VMEM((heads, 1), F32),
                        pltpu.VMEM((heads, dh), F32), pltpu.VMEM((heads, 1), F32)],
    )
    return pl.pallas_call(
        functools.partial(_fox_decode_kernel, heads=heads, scale=dh ** -0.5),
        out_shape=jax.ShapeDtypeStruct((bsz, heads, dh), BF16),
        grid_spec=grid_spec,
        compiler_params=_params("arbitrary", "arbitrary"),
        name="fox_decode",
    )(page_table, q, k_new, v_new, cq, cache_k, cache_v, cache_ft, suffix)


def kernel(x_prompt, x_sample, mem_prompt, cache_fox_k, cache_fox_v, cache_fox_logf, state_gla, cache_mem_k, cache_mem_v, page_table, g_mix, w_in, b_forget, w_gla_gate, b_gla_gate, g_gla_head, w_out, g_cross, g_mem, w_xq, w_xk, w_xv, w_xo, g_ffn, w_ffn_gate, w_ffn_up, w_ffn_down, g_final):
    depth = w_in.shape[0]
    bsz, seq, d = x_prompt.shape
    sb = x_sample.shape[0]
    mem = mem_prompt.shape[1]
    _, pool, page, fh, fdh = cache_fox_k.shape
    _, _, gh, gdk, gdv = state_gla.shape
    xh, xdh = cache_mem_k.shape[3], cache_mem_k.shape[4]
    fox_dim, gk_dim, gv_dim = fh * fdh, gh * gdk, gh * gdv
    rank = w_gla_gate.shape[1]
    n = bsz * seq
    c_forget = 3 * fox_dim
    c_gla = c_forget + fh
    c_rank = c_gla + 2 * gk_dim + 2 * gv_dim
    nz = c_rank - fh

    xp = x_prompt.reshape(n, d)
    xs = x_sample.reshape(sb, d)
    memx = mem_prompt.reshape(bsz * mem, d)
    ck = cache_fox_k.reshape(depth, pool, page * fh, fdh)
    cv = cache_fox_v.reshape(depth, pool, page * fh, fdh)
    cft = jnp.swapaxes(cache_fox_logf, 2, 3)
    row = lambda v: v.reshape(1, -1)
    zeros_state = jnp.zeros((bsz, gh, gdk, gdv), F32)

    pk, pv, pf, ps, pmk, pmv, sk, sv, sf, ss = ([] for _ in range(10))
    for l in range(depth):
        w_main = jnp.concatenate([w_in[l, :, :c_forget], w_in[l, :, c_gla:c_rank]], axis=1).astype(BF16)
        w_small = jnp.zeros((d, LANES), F32).at[:, :fh].set(w_in[l, :, c_forget:c_gla])
        w_small = w_small.at[:, fh:fh + rank].set(w_in[l, :, c_rank:]).astype(BF16)
        b_f = jnp.zeros((1, LANES), F32).at[0, :fh].set(b_forget[l])
        w_g2 = jnp.zeros((LANES, gk_dim), F32).at[fh:fh + rank].set(w_gla_gate[l]).astype(BF16)
        b_g = row(b_gla_gate[l])
        w_o = w_out[l].astype(BF16)
        w_q, w_xo_l = w_xq[l].astype(BF16), w_xo[l].astype(BF16)
        w_kv = jnp.concatenate([w_xk[l], w_xv[l]], axis=1).astype(BF16)
        w_g, w_u, w_d = w_ffn_gate[l].astype(BF16), w_ffn_up[l].astype(BF16), w_ffn_down[l].astype(BF16)
        last = l == depth - 1

        z, logf, ga, drow = _in_proj(xp, row(g_mix[l]), w_main, w_small, b_f, w_g2, b_g,
                                     n_fox_heads=fh, seq=seq, tm=min(ROW_TILE_PROJ, seq), tn=COL_TILE)
        o_f = _fox_prompt(z, drow, batch=bsz, seq=seq, heads=fh, dh=fdh, tq=min(FOX_Q_TILE, seq))
        o_g, s_fin = _gla(z, ga, zeros_state, row(g_gla_head[l]), batch=bsz, seq=seq, heads=gh, dk=gdk, dv=gdv,
                          col0=3 * fox_dim)
        xp = _out_proj(o_f, o_g, w_o, xp, tm=min(ROW_TILE_PROJ, n), tn=COL_TILE)
        mkv = _norm_matmul(memx, row(g_mem[l]), w_kv, tm=bsz * mem, tn=COL_TILE)
        mk, mv = mkv[:, :xh * xdh], mkv[:, xh * xdh:]
        xp = _cross_attn(xp.reshape(bsz, seq, d), row(g_cross[l]), w_q, mk.reshape(bsz, mem, -1),
                         mv.reshape(bsz, mem, -1), w_xo_l, heads=xh, dh=xdh, tm=min(ROW_TILE_FFN, seq)).reshape(n, d)
        xp = _swiglu(xp, row(g_ffn[l]), w_g, w_u, w_d, row(g_final), final_norm=last,
                     tm=min(ROW_TILE_FFN, n), tf=COL_TILE)
        pk.append(z[:, fox_dim:2 * fox_dim].reshape(bsz, seq, fh, fdh))
        pv.append(z[:, 2 * fox_dim:3 * fox_dim].reshape(bsz, seq, fh, fdh))
        pf.append(logf.reshape(bsz, seq, fh))
        ps.append(s_fin)
        pmk.append(mk.reshape(bsz, mem, xh, xdh))
        pmv.append(mv.reshape(bsz, mem, xh, xdh))

        zs, logf_s, ga_s = _in_proj(xs, row(g_mix[l]), w_main, w_small, b_f, w_g2, b_g,
                                    n_fox_heads=fh, seq=0, tm=sb, tn=COL_TILE)
        fq, fk, fv = (zs[:, i * fox_dim:(i + 1) * fox_dim].reshape(sb, fh, fdh) for i in range(3))
        cq = jnp.broadcast_to(logf_s[:, :, None], (sb, fh, LANES))
        o_fs = _fox_decode(fq, fk, fv, cq, ck, cv, cft, page_table, layer=l).reshape(sb, fox_dim)
        pad = lambda a: jnp.zeros((sb, GLA_CHUNK, a.shape[1]), a.dtype).at[:, 0].set(a).reshape(sb * GLA_CHUNK, -1)
        o_gs, s_new = _gla(pad(zs), pad(ga_s), state_gla[l], row(g_gla_head[l]), batch=sb, seq=GLA_CHUNK,
                           heads=gh, dk=gdk, dv=gdv, col0=3 * fox_dim)
        o_gs = o_gs.reshape(sb, GLA_CHUNK, gv_dim)[:, 0]
        xs = _out_proj(o_fs, o_gs, w_o, xs, tm=sb, tn=COL_TILE)
        xs = _cross_attn(xs.reshape(sb, 1, d), row(g_cross[l]), w_q, cache_mem_k[l].reshape(sb, mem, -1),
                         cache_mem_v[l].reshape(sb, mem, -1), w_xo_l, heads=xh, dh=xdh, tm=1).reshape(sb, d)
        xs = _swiglu(xs, row(g_ffn[l]), w_g, w_u, w_d, row(g_final), final_norm=last, tm=sb, tf=COL_TILE)
        sk.append(fk.reshape(sb, 1, fh, fdh))
        sv.append(fv.reshape(sb, 1, fh, fdh))
        sf.append(logf_s.reshape(sb, 1, fh))
        ss.append(s_new)

    return (xp.reshape(bsz, seq, d), xs.reshape(sb, 1, d),
            jnp.stack(pk), jnp.stack(pv), jnp.stack(pf), jnp.stack(ps), jnp.stack(pmk), jnp.stack(pmv),
            jnp.stack(sk), jnp.stack(sv), jnp.stack(sf), jnp.stack(ss))
```

```python
import functools

import numpy as np
import jax
import jax.numpy as jnp
from jax import lax
from jax.experimental import pallas as pl
from jax.experimental.pallas import tpu as pltpu

F32 = jnp.float32
BF16 = jnp.bfloat16

RMS_EPS = 1e-6
GLA_TAU = 16.0
GLA_CHUNK = 64
LANES = 128
SUBLANES = 8
VMEM_LIMIT_BYTES = 56 * 1024 * 1024
ROW_TILE_PROJ = 1024
ROW_TILE_FFN = 512
COL_TILE = 512
FOX_Q_TILE = 512
DECODE_PAGES_PER_STEP = 8
MXU_COLS = 256


def _params(*sem):
    return pltpu.CompilerParams(dimension_semantics=sem, vmem_limit_bytes=VMEM_LIMIT_BYTES)


def _rms(x, g):
    return x * lax.rsqrt(jnp.mean(x * x, axis=-1, keepdims=True) + RMS_EPS) * g


def _log_sigmoid(x):
    return jnp.minimum(x, 0.0) - jnp.log1p(jnp.exp(-jnp.abs(x)))


def _silu(x):
    return x / (1.0 + jnp.exp(-x))


def _dot(a, b):
    return jnp.dot(a, b, preferred_element_type=F32)


def _dot_nt(a, b):
    return lax.dot_general(a, b, (((1,), (1,)), ((), ())), preferred_element_type=F32)


def _dot_tn(a, b):
    return lax.dot_general(a, b, (((0,), (0,)), ((), ())), preferred_element_type=F32)


def _split3(x):
    a = x.astype(BF16)
    r = x - a.astype(F32)
    b = r.astype(BF16)
    c = (r - b.astype(F32)).astype(BF16)
    return a, b, c


def _dot3(m, x, dot=_dot):
    a, b, c = _split3(x)
    return dot(m, a) + dot(m, b) + dot(m, c)


def _in_proj_kernel(x_ref, g_ref, w_ref, ws_ref, bf_ref, wg2_ref, bg_ref, *rest, n_fox_heads, tiles_per_seq):
    if tiles_per_seq:
        z_ref, logf_ref, ga_ref, k3_ref, v3_ref, drow_ref, hb_ref, carry_ref = rest
    else:
        z_ref, logf_ref, ga_ref, k3_ref, v3_ref, hb_ref = rest
    i = pl.program_id(0)
    j = pl.program_id(1)

    @pl.when(j == 0)
    def _():
        hb = _rms(x_ref[...], g_ref[...]).astype(BF16)
        hb_ref[...] = hb
        zs = _dot(hb, ws_ref[...])
        lf = _log_sigmoid(zs + bf_ref[...])
        logf_ref[...] = lf[:, :n_fox_heads]
        gp = _dot(zs.astype(BF16), wg2_ref[...]) + bg_ref[...]
        ga_ref[...] = _log_sigmoid(gp) * (1.0 / GLA_TAU)
        if tiles_per_seq:
            tm = lf.shape[0]
            r = lax.broadcasted_iota(jnp.int32, (tm, tm), 0)
            c = lax.broadcasted_iota(jnp.int32, (tm, tm), 1)
            tri = jnp.where(r <= c, 1.0, 0.0).astype(BF16)
            prev = jnp.where(i % tiles_per_seq == 0, 0.0, carry_ref[:, :1])
            d = _dot3(tri, lf.T, dot=lambda m, x: _dot(x, m)) + prev
            carry_ref[...] = jnp.broadcast_to(d[:, tm - 1:tm], carry_ref.shape)
            drow_ref[...] = d[:SUBLANES, :]

    zt = _dot(hb_ref[...], w_ref[...])
    z_ref[...] = zt
    tm, tn = zt.shape
    dh = k3_ref.shape[1]
    heads_per_tile = tn // dh
    tiles_per_group = n_fox_heads // heads_per_tile
    for group, out_ref in ((1, k3_ref), (2, v3_ref)):
        for t in range(tiles_per_group):
            @pl.when(j == group * tiles_per_group + t)
            def _(out_ref=out_ref, t=t):
                for c in range(heads_per_tile):
                    out_ref[pl.ds(t * heads_per_tile + c, tm, stride=n_fox_heads), :] = zt[:, c * dh:(c + 1) * dh]


def _in_proj(x, g, w_main, w_small, b_f, w_g2, b_g, *, layer, n_fox_heads, fox_dh, seq, tm, tn):
    n, d = x.shape
    nz = w_main.shape[2]
    kg = w_g2.shape[2]
    tiles_per_seq = seq // tm if seq else 0
    out_shape = [jax.ShapeDtypeStruct((n, nz), F32),
                 jax.ShapeDtypeStruct((n, n_fox_heads), F32),
                 jax.ShapeDtypeStruct((n, kg), F32),
                 jax.ShapeDtypeStruct((n * n_fox_heads, fox_dh), F32),
                 jax.ShapeDtypeStruct((n * n_fox_heads, fox_dh), F32)]
    out_specs = [pl.BlockSpec((tm, tn), lambda i, j: (i, j)),
                 pl.BlockSpec((tm, n_fox_heads), lambda i, j: (i, 0)),
                 pl.BlockSpec((tm, kg), lambda i, j: (i, 0)),
                 pl.BlockSpec((tm * n_fox_heads, fox_dh), lambda i, j: (i, 0)),
                 pl.BlockSpec((tm * n_fox_heads, fox_dh), lambda i, j: (i, 0))]
    scratch = [pltpu.VMEM((tm, d), BF16)]
    if tiles_per_seq:
        out_shape.append(jax.ShapeDtypeStruct((n // seq, SUBLANES, seq), F32))
        out_specs.append(pl.BlockSpec((None, SUBLANES, tm), lambda i, j: (i // tiles_per_seq, 0, i % tiles_per_seq)))
        scratch.append(pltpu.VMEM((LANES, LANES), F32))
    return pl.pallas_call(
        functools.partial(_in_proj_kernel, n_fox_heads=n_fox_heads, tiles_per_seq=tiles_per_seq),
        out_shape=out_shape,
        grid=(n // tm, nz // tn),
        in_specs=[pl.BlockSpec((tm, d), lambda i, j: (i, 0)),
                  pl.BlockSpec((1, d), lambda i, j: (0, 0)),
                  pl.BlockSpec((None, d, tn), lambda i, j: (layer, 0, j)),
                  pl.BlockSpec((None, d, LANES), lambda i, j: (layer, 0, 0)),
                  pl.BlockSpec((1, LANES), lambda i, j: (0, 0)),
                  pl.BlockSpec((None, LANES, kg), lambda i, j: (layer, 0, 0)),
                  pl.BlockSpec((1, kg), lambda i, j: (0, 0))],
        out_specs=out_specs,
        scratch_shapes=scratch,
        compiler_params=_params("arbitrary", "arbitrary"),
        name="in_proj",
    )(x, g, w_main, w_small, b_f, w_g2, b_g)


def _norm_matmul_kernel(x_ref, g_ref, w_ref, o_ref, hb_ref):
    @pl.when(pl.program_id(1) == 0)
    def _():
        hb_ref[...] = _rms(x_ref[...], g_ref[...]).astype(BF16)

    o_ref[...] = _dot(hb_ref[...], w_ref[...])


def _norm_matmul(x, g, w, *, layer, tm, tn):
    n, d = x.shape
    nout = w.shape[2]
    return pl.pallas_call(
        _norm_matmul_kernel,
        out_shape=jax.ShapeDtypeStruct((n, nout), F32),
        grid=(n // tm, nout // tn),
        in_specs=[pl.BlockSpec((tm, d), lambda i, j: (i, 0)),
                  pl.BlockSpec((1, d), lambda i, j: (0, 0)),
                  pl.BlockSpec((None, d, tn), lambda i, j: (layer, 0, j))],
        out_specs=pl.BlockSpec((tm, tn), lambda i, j: (i, j)),
        scratch_shapes=[pltpu.VMEM((tm, d), BF16)],
        compiler_params=_params("arbitrary", "arbitrary"),
        name="norm_matmul",
    )(x, g, w)


def _fox_prompt_kernel(q_ref, k_ref, v_ref, d_ref, o_ref, kb_ref, vb_ref, *, tq, scale):
    h = pl.program_id(1)
    qi = pl.program_id(2)

    @pl.when(qi == 0)
    def _():
        kb_ref[...] = k_ref[...].astype(BF16)
        vb_ref[...] = v_ref[...].astype(BF16)

    q = (q_ref[...] * scale).astype(BF16)

    def block(j, carry, masked):
        m, l, acc = carry
        start = pl.multiple_of(j * tq, tq)
        kj = kb_ref[pl.ds(start, tq), :]
        vj = vb_ref[pl.ds(start, tq), :]
        s = _dot_nt(q, kj) - d_ref[pl.ds(h, 1), pl.ds(start, tq)]
        if masked:
            r = lax.broadcasted_iota(jnp.int32, (tq, tq), 0)
            c = lax.broadcasted_iota(jnp.int32, (tq, tq), 1)
            s = jnp.where(c <= r, s, -jnp.inf)
        m_new = jnp.maximum(m, jnp.max(s, axis=-1, keepdims=True))
        a = jnp.exp(m - m_new)
        p = jnp.exp(s - m_new)
        l = a * l + jnp.sum(p, axis=-1, keepdims=True)
        acc = a * acc + _dot(p.astype(BF16), vj)
        return m_new, l, acc

    dh = q.shape[1]
    init = (jnp.full((tq, 1), -jnp.inf, F32), jnp.zeros((tq, 1), F32), jnp.zeros((tq, dh), F32))
    carry = lax.fori_loop(0, qi, lambda j, c: block(j, c, False), init)
    _, l, acc = block(qi, carry, True)
    o_ref[...] = (acc / l).astype(o_ref.dtype)


def _fox_prompt(z, drow, *, batch, seq, heads, dh, tq):
    n = z.shape[0]
    nq = seq // tq
    return pl.pallas_call(
        functools.partial(_fox_prompt_kernel, tq=tq, scale=dh ** -0.5),
        out_shape=jax.ShapeDtypeStruct((n, heads * dh), BF16),
        grid=(batch, heads, nq),
        in_specs=[pl.BlockSpec((tq, dh), lambda b, h, i: (b * nq + i, h)),
                  pl.BlockSpec((seq, dh), lambda b, h, i: (b, heads + h)),
                  pl.BlockSpec((seq, dh), lambda b, h, i: (b, 2 * heads + h)),
                  pl.BlockSpec((None, SUBLANES, seq), lambda b, h, i: (b, 0, 0))],
        out_specs=pl.BlockSpec((tq, dh), lambda b, h, i: (b * nq + i, h)),
        scratch_shapes=[pltpu.VMEM((seq, dh), BF16), pltpu.VMEM((seq, dh), BF16)],
        compiler_params=_params("arbitrary", "arbitrary", "arbitrary"),
        name="fox_prompt",
    )(z, z, z, drow)


def _gla_tables(c):
    r = np.arange(c)
    tri = (r[None, :] <= r[:, None]).astype(np.float32)
    sums, masks = [tri], [np.eye(c, dtype=np.float32)]
    m = c // 2
    while m >= 1:
        mid = (r // (2 * m)) * (2 * m) + m - 1
        sums.append(tri[mid])
        same = (r[:, None] // (2 * m)) == (r[None, :] // (2 * m))
        upper = (r[:, None] % (2 * m)) >= m
        lower = (r[None, :] % (2 * m)) < m
        masks.append((same & upper & lower).astype(np.float32))
        m //= 2
    return np.concatenate(sums, 0), np.concatenate(masks, 0)


def _gla_kernel(q_ref, k_ref, v_ref, r_ref, ga_ref, s0_ref, sums_ref, masks_ref, gh_ref,
                og_ref, sfin_ref, st_ref, *, heads, dk, dv, scale):
    ci = pl.program_id(1)

    @pl.when(ci == 0)
    def _():
        st_ref[...] = s0_ref[...]

    c = q_ref.shape[0]
    n_levels = sums_ref.shape[0] // c - 1
    sums = sums_ref[...]
    for h in range(heads):
        q = q_ref[:, h * dk:(h + 1) * dk] * scale
        k = k_ref[:, h * dk:(h + 1) * dk]
        v = v_ref[:, h * dv:(h + 1) * dv].astype(BF16)
        g = ga_ref[:, h * dk:(h + 1) * dk]
        gs = _dot3(sums, g)
        b = gs[:c]
        a = jnp.where(masks_ref[0:c, :] != 0, _dot_nt(q.astype(BF16), k.astype(BF16)), 0.0)
        for lv in range(n_levels):
            d = b - gs[(lv + 1) * c:(lv + 2) * c]
            qt = (q * jnp.exp(jnp.minimum(d, 0.0))).astype(BF16)
            kt = (k * jnp.exp(jnp.minimum(-d, 0.0))).astype(BF16)
            a = a + jnp.where(masks_ref[(lv + 1) * c:(lv + 2) * c, :] != 0, _dot_nt(qt, kt), 0.0)
        s_old = st_ref[h]
        o = _dot(a.astype(BF16), v) + _dot((q * jnp.exp(b)).astype(BF16), s_old.astype(BF16))
        b_last = b[c - 1:c, :]
        upd = _dot_tn((k * jnp.exp(b_last - b)).astype(BF16), v)
        col = jnp.broadcast_to(jnp.exp(b_last), (dk, dk)).T
        st_ref[h] = jnp.concatenate([col] * (dv // dk), axis=1) * s_old + upd
        y = _rms(o, gh_ref[...])
        og_ref[:, h * dv:(h + 1) * dv] = (y * _silu(r_ref[:, h * dv:(h + 1) * dv])).astype(og_ref.dtype)

    @pl.when(ci == pl.num_programs(1) - 1)
    def _():
        sfin_ref[...] = st_ref[...]


def _gla(z, ga, s0, g_head, *, batch, seq, heads, dk, dv, col0):
    n = z.shape[0]
    c = GLA_CHUNK
    nc = seq // c
    wk, wv = heads * dk, heads * dv
    sums, masks = _gla_tables(c)
    row = lambda b, i: b * nc + i
    return pl.pallas_call(
        functools.partial(_gla_kernel, heads=heads, dk=dk, dv=dv, scale=dk ** -0.5),
        out_shape=[jax.ShapeDtypeStruct((n, wv), BF16), jax.ShapeDtypeStruct((batch, heads, dk, dv), F32)],
        grid=(batch, nc),
        in_specs=[pl.BlockSpec((c, wk), lambda b, i: (row(b, i), col0 // wk)),
                  pl.BlockSpec((c, wk), lambda b, i: (row(b, i), col0 // wk + 1)),
                  pl.BlockSpec((c, wv), lambda b, i: (row(b, i), (col0 + 2 * wk) // wv)),
                  pl.BlockSpec((c, wv), lambda b, i: (row(b, i), (col0 + 2 * wk) // wv + 1)),
                  pl.BlockSpec((c, wk), lambda b, i: (row(b, i), 0)),
                  pl.BlockSpec((None, heads, dk, dv), lambda b, i: (b, 0, 0, 0)),
                  pl.BlockSpec(sums.shape, lambda b, i: (0, 0)),
                  pl.BlockSpec(masks.shape, lambda b, i: (0, 0)),
                  pl.BlockSpec((1, dv), lambda b, i: (0, 0))],
        out_specs=[pl.BlockSpec((c, wv), lambda b, i: (row(b, i), 0)),
                   pl.BlockSpec((None, heads, dk, dv), lambda b, i: (b, 0, 0, 0))],
        scratch_shapes=[pltpu.VMEM((heads, dk, dv), F32)],
        compiler_params=_params("arbitrary", "arbitrary"),
        name="gla",
    )(z, z, z, z, ga, s0, jnp.asarray(sums, BF16), jnp.asarray(masks, F32), g_head)


def _out_proj_kernel(a_ref, b_ref, w1_ref, w2_ref, x_ref, o_ref):
    o_ref[...] = x_ref[...] + _dot(a_ref[...], w1_ref[...]) + _dot(b_ref[...], w2_ref[...])


def _out_proj(a, b, w, x, *, layer, tm, tn):
    n, d = x.shape
    kh = a.shape[1]
    return pl.pallas_call(
        _out_proj_kernel,
        out_shape=jax.ShapeDtypeStruct((n, d), F32),
        grid=(n // tm, d // tn),
        in_specs=[pl.BlockSpec((tm, kh), lambda i, j: (i, 0)),
                  pl.BlockSpec((tm, kh), lambda i, j: (i, 0)),
                  pl.BlockSpec((None, kh, tn), lambda i, j: (layer, 0, j)),
                  pl.BlockSpec((None, kh, tn), lambda i, j: (layer, 1, j)),
                  pl.BlockSpec((tm, tn), lambda i, j: (i, j))],
        out_specs=pl.BlockSpec((tm, tn), lambda i, j: (i, j)),
        compiler_params=_params("arbitrary", "arbitrary"),
        name="out_proj",
    )(a, b, w, w, x)


def _cross_attn_kernel(x_ref, g_ref, wq_ref, mk_ref, mv_ref, wo_ref, o_ref, *, heads, dh):
    x = x_ref[...]
    tm, d = x.shape
    xr = x if tm >= SUBLANES else jnp.broadcast_to(x, (SUBLANES, d))
    q = _dot(_rms(xr, g_ref[...]).astype(BF16), wq_ref[...]) * (dh ** -0.5)
    outs = []
    for h in range(heads):
        sl = slice(h * dh, (h + 1) * dh)
        s = _dot_nt(q[:, sl].astype(BF16), mk_ref[:, sl].astype(BF16))
        p = jnp.exp(s - jnp.max(s, axis=-1, keepdims=True))
        p = p / jnp.sum(p, axis=-1, keepdims=True)
        outs.append(_dot(p.astype(BF16), mv_ref[:, sl].astype(BF16)))
    y = _dot(jnp.concatenate(outs, axis=1).astype(BF16), wo_ref[...])
    o_ref[...] = x + y[:tm]


def _cross_attn(x3, g, wq, mk, mv, wo, *, layer, heads, dh, tm):
    bsz, rows, d = x3.shape
    mem = mk.shape[1]
    xd = heads * dh
    return pl.pallas_call(
        functools.partial(_cross_attn_kernel, heads=heads, dh=dh),
        out_shape=jax.ShapeDtypeStruct(x3.shape, F32),
        grid=(bsz, rows // tm),
        in_specs=[pl.BlockSpec((None, tm, d), lambda b, i: (b, i, 0)),
                  pl.BlockSpec((1, d), lambda b, i: (0, 0)),
                  pl.BlockSpec((None, d, xd), lambda b, i: (layer, 0, 0)),
                  pl.BlockSpec((None, mem, xd), lambda b, i: (b, 0, 0)),
                  pl.BlockSpec((None, mem, xd), lambda b, i: (b, 0, 0)),
                  pl.BlockSpec((None, xd, d), lambda b, i: (layer, 0, 0))],
        out_specs=pl.BlockSpec((None, tm, d), lambda b, i: (b, i, 0)),
        compiler_params=_params("arbitrary", "arbitrary"),
        name="cross_attn",
    )(x3, g, wq, mk, mv, wo)


def _swiglu_kernel(x_ref, g_ref, wg_ref, wu_ref, wd_ref, gf_ref, o_ref, hb_ref, acc_ref, *, final_norm):
    f = pl.program_id(1)

    @pl.when(f == 0)
    def _():
        hb_ref[...] = _rms(x_ref[...], g_ref[...]).astype(BF16)
        acc_ref[...] = jnp.zeros_like(acc_ref)

    hb = hb_ref[...]
    act = _silu(_dot(hb, wg_ref[...])) * _dot(hb, wu_ref[...])
    acc_ref[...] += _dot(act.astype(BF16), wd_ref[...])

    @pl.when(f == pl.num_programs(1) - 1)
    def _():
        y = x_ref[...] + acc_ref[...]
        o_ref[...] = _rms(y, gf_ref[...]) if final_norm else y


def _swiglu(x, g, wg, wu, wd, g_final, *, layer, final_norm, tm, tf):
    n, d = x.shape
    ffn = wg.shape[2]
    return pl.pallas_call(
        functools.partial(_swiglu_kernel, final_norm=final_norm),
        out_shape=jax.ShapeDtypeStruct((n, d), F32),
        grid=(n // tm, ffn // tf),
        in_specs=[pl.BlockSpec((tm, d), lambda i, f: (i, 0)),
                  pl.BlockSpec((1, d), lambda i, f: (0, 0)),
                  pl.BlockSpec((None, d, tf), lambda i, f: (layer, 0, f)),
                  pl.BlockSpec((None, d, tf), lambda i, f: (layer, 0, f)),
                  pl.BlockSpec((None, tf, d), lambda i, f: (layer, f, 0)),
                  pl.BlockSpec((1, d), lambda i, f: (0, 0))],
        out_specs=pl.BlockSpec((tm, d), lambda i, f: (i, 0)),
        scratch_shapes=[pltpu.VMEM((tm, d), BF16), pltpu.VMEM((tm, d), F32)],
        compiler_params=_params("arbitrary", "arbitrary"),
        name="swiglu",
    )(x, g, wg, wu, wd, g_final)


def _suffix_rows_exclusive(x):
    n = x.shape[0]
    row = lax.broadcasted_iota(jnp.int32, x.shape, 0)

    def up(a, k):
        return jnp.where(row + k < n, pltpu.roll(a, n - k, 0), 0.0)

    y = up(x, 1)
    k = 1
    while k < n:
        y = y + up(y, k)
        k *= 2
    return y


def _fox_decode_kernel(pt_ref, q_ref, kn_ref, vn_ref, cq_ref, *rest, heads, group, scale):
    del pt_ref
    k_refs, v_refs, f_refs = rest[:group], rest[group:2 * group], rest[2 * group:3 * group]
    u_same_ref, u_later_ref, o_ref, m_ref, l_ref, acc_ref, tail_ref = rest[3 * group:]
    step = pl.program_id(1)

    @pl.when(step == 0)
    def _():
        m_ref[...] = jnp.full_like(m_ref, -jnp.inf)
        l_ref[...] = jnp.zeros_like(l_ref)
        acc_ref[...] = jnp.zeros_like(acc_ref)
        tail_ref[...] = jnp.zeros_like(tail_ref)

    n_blk = k_refs[0].shape[0] // LANES
    wide = MXU_COLS // LANES
    q = q_ref[...]
    qb = q.astype(BF16)
    sub = lax.broadcasted_iota(jnp.int32, (heads, MXU_COLS), 0)
    lane = lax.broadcasted_iota(jnp.int32, (heads, MXU_COLS), 1)
    own = sub == lane % heads
    parts = _split3(jnp.concatenate([f[...] for f in f_refs], axis=0))
    rows_later = sum(_dot(x, u_later_ref[...]) for x in parts)
    rows_total = sum(_dot(x, u_same_ref[...]) for x in parts)
    tail = tail_ref[...]
    tiles = []
    for g in range(group):
        row_total = rows_total[g * n_blk:(g + 1) * n_blk]
        bias = cq_ref[...] + tail + rows_later[g * n_blk:(g + 1) * n_blk] + _suffix_rows_exclusive(row_total)
        tail = tail + jnp.sum(row_total, axis=0, keepdims=True)
        for r in range(0, n_blk, wide):
            kb = k_refs[g][r * LANES:(r + wide) * LANES, :].astype(BF16)
            brow = jnp.concatenate([bias[r + w:r + w + 1, :] for w in range(wide)], axis=1)
            s = jnp.where(own, _dot_nt(qb, kb) * scale + brow, -jnp.inf)
            tiles.extend(s[:, w * LANES:(w + 1) * LANES] for w in range(wide))
    tail_ref[...] = tail

    top = tiles[0]
    for t in tiles[1:]:
        top = jnp.maximum(top, t)
    m_old = m_ref[...]
    m_new = jnp.maximum(m_old, jnp.max(top, axis=-1, keepdims=True))
    a = jnp.exp(m_old - m_new)
    accs = [a * acc_ref[...], jnp.zeros(acc_ref.shape, F32)]
    psum = jnp.zeros((heads, LANES), F32)
    dh = acc_ref.shape[1]
    for n, idx in enumerate(range(0, len(tiles), wide)):
        g, r = divmod(idx, n_blk)
        prs = [jnp.exp(tiles[idx + w] - m_new) for w in range(wide)]
        for pr in prs:
            psum = psum + pr
        vb = jnp.concatenate([v_refs[g][(r + w) * LANES:(r + w + 1) * LANES, :].astype(BF16) for w in range(wide)],
                             axis=1)
        res = _dot(jnp.concatenate(prs, axis=0).astype(BF16), vb)
        for w in range(wide):
            accs[n % 2] = accs[n % 2] + res[w * heads:(w + 1) * heads, w * dh:(w + 1) * dh]
    m_ref[...] = m_new
    l_ref[...] = a * l_ref[...] + jnp.sum(psum, axis=-1, keepdims=True)
    acc_ref[...] = accs[0] + accs[1]

    @pl.when(step == pl.num_programs(1) - 1)
    def _():
        s_new = jnp.sum(q * kn_ref[...], axis=-1, keepdims=True) * scale
        m_fin = jnp.maximum(m_ref[...], s_new)
        a_fin = jnp.exp(m_ref[...] - m_fin)
        p_new = jnp.exp(s_new - m_fin)
        o_ref[...] = ((a_fin * acc_ref[...] + p_new * vn_ref[...]) / (a_fin * l_ref[...] + p_new)).astype(o_ref.dtype)


def _fox_decode(q, k_new, v_new, cq, cache_k, cache_v, cache_f, page_table, *, layer, group):
    bsz, heads, dh = q.shape
    rows = cache_k.shape[2]
    n_pages = page_table.shape[1]
    lane = np.arange(LANES)
    same = lane[:, None] % heads == lane[None, :] % heads
    u_same = jnp.asarray(same, BF16)
    u_later = jnp.asarray(same & (lane[:, None] > lane[None, :]), BF16)
    vec = pl.BlockSpec((None, heads, dh), lambda b, s, pt: (b, 0, 0))

    def paged(shape, g):
        return pl.BlockSpec((None, None) + shape, lambda b, s, pt: (layer, pt[b, n_pages - 1 - (s * group + g)], 0, 0))

    const = pl.BlockSpec((LANES, LANES), lambda b, s, pt: (0, 0))
    grid_spec = pltpu.PrefetchScalarGridSpec(
        num_scalar_prefetch=1,
        grid=(bsz, n_pages // group),
        in_specs=([vec, vec, vec, pl.BlockSpec((None, 1, LANES), lambda b, s, pt: (b, 0, 0))]
                  + [paged((rows, dh), g) for g in range(group)]
                  + [paged((rows, dh), g) for g in range(group)]
                  + [paged((rows // LANES, LANES), g) for g in range(group)]
                  + [const, const]),
        out_specs=vec,
        scratch_shapes=[pltpu.VMEM((heads, 1), F32), pltpu.VMEM((heads, 1), F32),
                        pltpu.VMEM((heads, dh), F32), pltpu.VMEM((1, LANES), F32)],
    )
    return pl.pallas_call(
        functools.partial(_fox_decode_kernel, heads=heads, group=group, scale=dh ** -0.5),
        out_shape=jax.ShapeDtypeStruct((bsz, heads, dh), BF16),
        grid_spec=grid_spec,
        compiler_params=_params("arbitrary", "arbitrary"),
        name="fox_decode",
    )(page_table, q, k_new, v_new, cq, *([cache_k] * group), *([cache_v] * group), *([cache_f] * group),
      u_same, u_later)


def kernel(x_prompt, x_sample, mem_prompt, cache_fox_k, cache_fox_v, cache_fox_logf, state_gla, cache_mem_k, cache_mem_v, page_table, g_mix, w_in, b_forget, w_gla_gate, b_gla_gate, g_gla_head, w_out, g_cross, g_mem, w_xq, w_xk, w_xv, w_xo, g_ffn, w_ffn_gate, w_ffn_up, w_ffn_down, g_final):
    depth = w_in.shape[0]
    bsz, seq, d = x_prompt.shape
    sb = x_sample.shape[0]
    mem = mem_prompt.shape[1]
    _, pool, page, fh, fdh = cache_fox_k.shape
    _, _, gh, gdk, gdv = state_gla.shape
    xh, xdh = cache_mem_k.shape[3], cache_mem_k.shape[4]
    fox_dim, gk_dim, gv_dim = fh * fdh, gh * gdk, gh * gdv
    rank = w_gla_gate.shape[1]
    n = bsz * seq
    c_forget = 3 * fox_dim
    c_gla = c_forget + fh
    c_rank = c_gla + 2 * gk_dim + 2 * gv_dim

    xp = x_prompt.reshape(n, d)
    xs = x_sample.reshape(sb, d)
    memx = mem_prompt.reshape(bsz * mem, d)
    ck = cache_fox_k.reshape(depth, pool, page * fh, fdh)
    cv = cache_fox_v.reshape(depth, pool, page * fh, fdh)
    cf = cache_fox_logf.reshape(depth, pool, page * fh // LANES, LANES)
    row = lambda v: v.reshape(1, -1)
    zeros_state = jnp.zeros((bsz, gh, gdk, gdv), F32)
    group = min(DECODE_PAGES_PER_STEP, page_table.shape[1])

    w_main = jnp.concatenate([w_in[:, :, :c_forget], w_in[:, :, c_gla:c_rank]], axis=2).astype(BF16)
    w_small = jnp.concatenate([w_in[:, :, c_forget:c_gla], w_in[:, :, c_rank:],
                               jnp.zeros((depth, d, LANES - fh - rank), F32)], axis=2).astype(BF16)
    w_g2 = jnp.concatenate([jnp.zeros((depth, fh, gk_dim), F32), w_gla_gate,
                            jnp.zeros((depth, LANES - fh - rank, gk_dim), F32)], axis=1).astype(BF16)
    b_f = jnp.concatenate([b_forget, jnp.zeros((depth, LANES - fh), F32)], axis=1)
    w_o, w_q, w_xo_b = w_out.astype(BF16), w_xq.astype(BF16), w_xo.astype(BF16)
    w_kv = jnp.concatenate([w_xk, w_xv], axis=2).astype(BF16)
    w_g, w_u, w_d = w_ffn_gate.astype(BF16), w_ffn_up.astype(BF16), w_ffn_down.astype(BF16)

    pk, pv, pf, ps, pmk, pmv, sk, sv, sf, ss = ([] for _ in range(10))
    for l in range(depth):
        last = l == depth - 1

        z, logf, ga, k3, v3, drow = _in_proj(xp, row(g_mix[l]), w_main, w_small, b_f[l:l + 1], w_g2,
                                             row(b_gla_gate[l]), layer=l, n_fox_heads=fh, fox_dh=fdh, seq=seq,
                                             tm=min(ROW_TILE_PROJ, seq), tn=COL_TILE)
        o_f = _fox_prompt(z, drow, batch=bsz, seq=seq, heads=fh, dh=fdh, tq=min(FOX_Q_TILE, seq))
        o_g, s_fin = _gla(z, ga, zeros_state, row(g_gla_head[l]), batch=bsz, seq=seq, heads=gh, dk=gdk, dv=gdv,
                          col0=3 * fox_dim)
        xp = _out_proj(o_f, o_g, w_o, xp, layer=l, tm=min(ROW_TILE_PROJ, n), tn=COL_TILE)
        mkv = _norm_matmul(memx, row(g_mem[l]), w_kv, layer=l, tm=bsz * mem, tn=COL_TILE)
        mk, mv = mkv[:, :xh * xdh], mkv[:, xh * xdh:]
        xp = _cross_attn(xp.reshape(bsz, seq, d), row(g_cross[l]), w_q, mk.reshape(bsz, mem, -1),
                         mv.reshape(bsz, mem, -1), w_xo_b, layer=l, heads=xh, dh=xdh,
                         tm=min(ROW_TILE_FFN, seq)).reshape(n, d)
        xp = _swiglu(xp, row(g_ffn[l]), w_g, w_u, w_d, row(g_final), layer=l, final_norm=last,
                     tm=min(ROW_TILE_FFN, n), tf=COL_TILE)
        pk.append(k3.reshape(bsz, seq, fh, fdh))
        pv.append(v3.reshape(bsz, seq, fh, fdh))
        pf.append(logf.reshape(bsz, seq, fh))
        ps.append(s_fin)
        pmk.append(mk.reshape(bsz, mem, xh, xdh))
        pmv.append(mv.reshape(bsz, mem, xh, xdh))

        zs, logf_s, ga_s, fk, fv = _in_proj(xs, row(g_mix[l]), w_main, w_small, b_f[l:l + 1], w_g2,
                                            row(b_gla_gate[l]), layer=l, n_fox_heads=fh, fox_dh=fdh, seq=0,
                                            tm=sb, tn=COL_TILE)
        fq, fk, fv = zs[:, :fox_dim].reshape(sb, fh, fdh), fk.reshape(sb, fh, fdh), fv.reshape(sb, fh, fdh)
        cq = jnp.tile(logf_s, (1, LANES // fh)).reshape(sb, 1, LANES)
        o_fs = _fox_decode(fq, fk, fv, cq, ck, cv, cf, page_table, layer=l, group=group).reshape(sb, fox_dim)
        pad = lambda a: jnp.zeros((sb, GLA_CHUNK, a.shape[1]), a.dtype).at[:, 0].set(a).reshape(sb * GLA_CHUNK, -1)
        o_gs, s_new = _gla(pad(zs), pad(ga_s), state_gla[l], row(g_gla_head[l]), batch=sb, seq=GLA_CHUNK,
                           heads=gh, dk=gdk, dv=gdv, col0=3 * fox_dim)
        o_gs = o_gs.reshape(sb, GLA_CHUNK, gv_dim)[:, 0]
        xs = _out_proj(o_fs, o_gs, w_o, xs, layer=l, tm=sb, tn=COL_TILE)
        xs = _cross_attn(xs.reshape(sb, 1, d), row(g_cross[l]), w_q, cache_mem_k[l].reshape(sb, mem, -1),
                         cache_mem_v[l].reshape(sb, mem, -1), w_xo_b, layer=l, heads=xh, dh=xdh, tm=1).reshape(sb, d)
        xs = _swiglu(xs, row(g_ffn[l]), w_g, w_u, w_d, row(g_final), layer=l, final_norm=last, tm=sb, tf=COL_TILE)
        sk.append(fk.reshape(sb, 1, fh, fdh))
        sv.append(fv.reshape(sb, 1, fh, fdh))
        sf.append(logf_s.reshape(sb, 1, fh))
        ss.append(s_new)

    return (xp.reshape(bsz, seq, d), xs.reshape(sb, 1, d),
            jnp.stack(pk), jnp.stack(pv), jnp.stack(pf), jnp.stack(ps), jnp.stack(pmk), jnp.stack(pmv),
            jnp.stack(sk), jnp.stack(sv), jnp.stack(sf), jnp.stack(ss))
```

```python
import functools

import numpy as np
import jax
import jax.numpy as jnp
from jax import lax
from jax.experimental import pallas as pl
from jax.experimental.pallas import tpu as pltpu

F32 = jnp.float32
BF16 = jnp.bfloat16

RMS_EPS = 1e-6
GLA_TAU = 16.0
GLA_CHUNK = 128
GLA_CHUNK_SAMPLE = 64
LANES = 128
SUBLANES = 8
VMEM_LIMIT_BYTES = 56 * 1024 * 1024
ROW_TILE_PROJ = 1024
ROW_TILE_FFN = 512
COL_TILE = 512
FOX_Q_TILE = 512
FOX_HEADS_PER_STEP = 2
DECODE_PAGES_PER_STEP = 8
MXU_COLS = 256


def _params(*sem):
    return pltpu.CompilerParams(dimension_semantics=sem, vmem_limit_bytes=VMEM_LIMIT_BYTES)


def _rms(x, g):
    return x * lax.rsqrt(jnp.mean(x * x, axis=-1, keepdims=True) + RMS_EPS) * g


def _log_sigmoid(x):
    return jnp.minimum(x, 0.0) - jnp.log1p(jnp.exp(-jnp.abs(x)))


def _silu(x):
    return x / (1.0 + jnp.exp(-x))


def _dot(a, b):
    return jnp.dot(a, b, preferred_element_type=F32)


def _dot_nt(a, b):
    return lax.dot_general(a, b, (((1,), (1,)), ((), ())), preferred_element_type=F32)


def _dot_tn(a, b):
    return lax.dot_general(a, b, (((0,), (0,)), ((), ())), preferred_element_type=F32)


def _split3(x):
    a = x.astype(BF16)
    r = x - a.astype(F32)
    b = r.astype(BF16)
    c = (r - b.astype(F32)).astype(BF16)
    return a, b, c


def _dot3(m, x, dot=_dot):
    a, b, c = _split3(x)
    return dot(m, a) + dot(m, b) + dot(m, c)


def _in_proj_kernel(x_ref, g_ref, w_ref, ws_ref, bf_ref, wg2_ref, bg_ref, *rest, n_fox_heads, tiles_per_seq):
    if tiles_per_seq:
        z_ref, logf_ref, ga_ref, k3_ref, v3_ref, drow_ref, hb_ref, carry_ref = rest
    else:
        z_ref, logf_ref, ga_ref, k3_ref, v3_ref, hb_ref = rest
    i = pl.program_id(0)
    j = pl.program_id(1)

    @pl.when(j == 0)
    def _():
        hb = _rms(x_ref[...], g_ref[...]).astype(BF16)
        hb_ref[...] = hb
        zs = _dot(hb, ws_ref[...])
        lf = _log_sigmoid(zs + bf_ref[...])
        logf_ref[...] = lf[:, :n_fox_heads]
        gp = _dot(zs.astype(BF16), wg2_ref[...]) + bg_ref[...]
        ga_ref[...] = _log_sigmoid(gp) * (1.0 / GLA_TAU)
        if tiles_per_seq:
            tm = lf.shape[0]
            r = lax.broadcasted_iota(jnp.int32, (LANES, LANES), 0)
            c = lax.broadcasted_iota(jnp.int32, (LANES, LANES), 1)
            tri = jnp.where(r <= c, 1.0, 0.0).astype(BF16)
            lft = lf.T[:SUBLANES]
            run = jnp.where(i % tiles_per_seq == 0, 0.0, carry_ref[...])
            for c0 in range(0, tm, LANES):
                d = _dot3(tri, lft[:, c0:c0 + LANES], dot=lambda m, x: _dot(x, m)) + run
                drow_ref[:, c0:c0 + LANES] = d
                run = d[:, LANES - 1:LANES]
            carry_ref[...] = run

    zt = _dot(hb_ref[...], w_ref[...])
    z_ref[...] = zt
    tm, tn = zt.shape
    dh = k3_ref.shape[1]
    heads_per_tile = tn // dh
    tiles_per_group = n_fox_heads // heads_per_tile
    for group, out_ref in ((1, k3_ref), (2, v3_ref)):
        for t in range(tiles_per_group):
            @pl.when(j == group * tiles_per_group + t)
            def _(out_ref=out_ref, t=t):
                for c in range(heads_per_tile):
                    out_ref[pl.ds(t * heads_per_tile + c, tm, stride=n_fox_heads), :] = zt[:, c * dh:(c + 1) * dh]


def _in_proj(x, g, w_main, w_small, b_f, w_g2, b_g, *, layer, n_fox_heads, fox_dh, seq, tm, tn):
    n, d = x.shape
    nz = w_main.shape[2]
    kg = w_g2.shape[2]
    tiles_per_seq = seq // tm if seq else 0
    out_shape = [jax.ShapeDtypeStruct((n, nz), F32),
                 jax.ShapeDtypeStruct((n, n_fox_heads), F32),
                 jax.ShapeDtypeStruct((n, kg), F32),
                 jax.ShapeDtypeStruct((n * n_fox_heads, fox_dh), F32),
                 jax.ShapeDtypeStruct((n * n_fox_heads, fox_dh), F32)]
    out_specs = [pl.BlockSpec((tm, tn), lambda i, j: (i, j)),
                 pl.BlockSpec((tm, n_fox_heads), lambda i, j: (i, 0)),
                 pl.BlockSpec((tm, kg), lambda i, j: (i, 0)),
                 pl.BlockSpec((tm * n_fox_heads, fox_dh), lambda i, j: (i, 0)),
                 pl.BlockSpec((tm * n_fox_heads, fox_dh), lambda i, j: (i, 0))]
    scratch = [pltpu.VMEM((tm, d), BF16)]
    if tiles_per_seq:
        out_shape.append(jax.ShapeDtypeStruct((n // seq, SUBLANES, seq), F32))
        out_specs.append(pl.BlockSpec((None, SUBLANES, tm), lambda i, j: (i // tiles_per_seq, 0, i % tiles_per_seq)))
        scratch.append(pltpu.VMEM((SUBLANES, 1), F32))
    return pl.pallas_call(
        functools.partial(_in_proj_kernel, n_fox_heads=n_fox_heads, tiles_per_seq=tiles_per_seq),
        out_shape=out_shape,
        grid=(n // tm, nz // tn),
        in_specs=[pl.BlockSpec((tm, d), lambda i, j: (i, 0)),
                  pl.BlockSpec((1, d), lambda i, j: (0, 0)),
                  pl.BlockSpec((None, d, tn), lambda i, j: (layer, 0, j)),
                  pl.BlockSpec((None, d, LANES), lambda i, j: (layer, 0, 0)),
                  pl.BlockSpec((1, LANES), lambda i, j: (0, 0)),
                  pl.BlockSpec((None, LANES, kg), lambda i, j: (layer, 0, 0)),
                  pl.BlockSpec((1, kg), lambda i, j: (0, 0))],
        out_specs=out_specs,
        scratch_shapes=scratch,
        compiler_params=_params("arbitrary", "arbitrary"),
        name="in_proj",
    )(x, g, w_main, w_small, b_f, w_g2, b_g)


def _norm_matmul_kernel(x_ref, g_ref, w_ref, o_ref, hb_ref):
    @pl.when(pl.program_id(1) == 0)
    def _():
        hb_ref[...] = _rms(x_ref[...], g_ref[...]).astype(BF16)

    o_ref[...] = _dot(hb_ref[...], w_ref[...])


def _norm_matmul(x, g, w, *, layer, tm, tn):
    n, d = x.shape
    nout = w.shape[2]
    return pl.pallas_call(
        _norm_matmul_kernel,
        out_shape=jax.ShapeDtypeStruct((n, nout), F32),
        grid=(n // tm, nout // tn),
        in_specs=[pl.BlockSpec((tm, d), lambda i, j: (i, 0)),
                  pl.BlockSpec((1, d), lambda i, j: (0, 0)),
                  pl.BlockSpec((None, d, tn), lambda i, j: (layer, 0, j))],
        out_specs=pl.BlockSpec((tm, tn), lambda i, j: (i, j)),
        scratch_shapes=[pltpu.VMEM((tm, d), BF16)],
        compiler_params=_params("arbitrary", "arbitrary"),
        name="norm_matmul",
    )(x, g, w)


def _fox_prompt_kernel(q_ref, k_ref, v_ref, d_ref, o_ref, kb_ref, vb_ref, *, tq, dh, scale):
    hg = pl.program_id(1)
    qi = pl.program_id(2)
    hp = q_ref.shape[1] // dh

    @pl.when(qi == 0)
    def _():
        kb_ref[...] = k_ref[...].astype(BF16)
        vb_ref[...] = v_ref[...].astype(BF16)

    qs = [(q_ref[:, i * dh:(i + 1) * dh] * scale).astype(BF16) for i in range(hp)]

    def block(j, carry, masked):
        start = pl.multiple_of(j * tq, tq)
        out = []
        for i in range(hp):
            m, l, acc = carry[i]
            kj = kb_ref[pl.ds(start, tq), i * dh:(i + 1) * dh]
            vj = vb_ref[pl.ds(start, tq), i * dh:(i + 1) * dh]
            s = _dot_nt(qs[i], kj) - d_ref[pl.ds(hg * hp + i, 1), pl.ds(start, tq)]
            if masked:
                r = lax.broadcasted_iota(jnp.int32, (tq, tq), 0)
                c = lax.broadcasted_iota(jnp.int32, (tq, tq), 1)
                s = jnp.where(c <= r, s, -jnp.inf)
            m_new = jnp.maximum(m, jnp.max(s, axis=-1, keepdims=True))
            a = jnp.exp(m - m_new)
            p = jnp.exp(s - m_new)
            l = a * l + jnp.sum(p, axis=-1, keepdims=True)
            acc = a * acc + _dot(p.astype(BF16), vj)
            out.append((m_new, l, acc))
        return tuple(out)

    init = tuple((jnp.full((tq, 1), -jnp.inf, F32), jnp.zeros((tq, 1), F32), jnp.zeros((tq, dh), F32))
                 for _ in range(hp))
    carry = lax.fori_loop(0, qi, lambda j, c: block(j, c, False), init)
    final = block(qi, carry, True)
    for i, (_, l, acc) in enumerate(final):
        o_ref[:, i * dh:(i + 1) * dh] = (acc / l).astype(o_ref.dtype)


def _fox_prompt(z, drow, *, batch, seq, heads, dh, tq, hp):
    n = z.shape[0]
    nq = seq // tq
    hg = heads // hp
    w = hp * dh
    return pl.pallas_call(
        functools.partial(_fox_prompt_kernel, tq=tq, dh=dh, scale=dh ** -0.5),
        out_shape=jax.ShapeDtypeStruct((n, heads * dh), BF16),
        grid=(batch, hg, nq),
        in_specs=[pl.BlockSpec((tq, w), lambda b, h, i: (b * nq + i, h)),
                  pl.BlockSpec((seq, w), lambda b, h, i: (b, hg + h)),
                  pl.BlockSpec((seq, w), lambda b, h, i: (b, 2 * hg + h)),
                  pl.BlockSpec((None, SUBLANES, seq), lambda b, h, i: (b, 0, 0))],
        out_specs=pl.BlockSpec((tq, w), lambda b, h, i: (b * nq + i, h)),
        scratch_shapes=[pltpu.VMEM((seq, w), BF16), pltpu.VMEM((seq, w), BF16)],
        compiler_params=_params("arbitrary", "arbitrary", "arbitrary"),
        name="fox_prompt",
    )(z, z, z, drow)


def _gla_tables(c):
    r = np.arange(c)
    tri = (r[None, :] <= r[:, None]).astype(np.float32)
    masks = [np.eye(c, dtype=np.float32)]
    m = c // 2
    while m >= 1:
        same = (r[:, None] // (2 * m)) == (r[None, :] // (2 * m))
        upper = (r[:, None] % (2 * m)) >= m
        lower = (r[None, :] % (2 * m)) < m
        masks.append((same & upper & lower).astype(np.float32))
        m //= 2
    return tri, np.concatenate(masks, 0)


def _gla_level_refs(b, b_ref, col0):
    c, dk = b.shape
    bcast = lambda i, rows: jnp.broadcast_to(b_ref[pl.ds(i, 1), pl.ds(col0, dk)], (rows, dk))
    sub = lax.broadcasted_iota(jnp.int32, (c, dk), 0) % SUBLANES
    out = []
    m = c // 2
    while m >= 1:
        if 2 * m >= SUBLANES:
            out.append(jnp.concatenate([bcast(blk * 2 * m + m - 1, 2 * m) for blk in range(c // (2 * m))], axis=0))
        elif m > 1:
            ref = None
            for t in range(SUBLANES // (2 * m)):
                rows = jnp.concatenate([bcast(g * SUBLANES + t * 2 * m + m - 1, SUBLANES)
                                        for g in range(c // SUBLANES)], axis=0)
                ref = rows if ref is None else jnp.where(sub // (2 * m) == t, rows, ref)
            out.append(ref)
        else:
            out.append(jnp.where(sub % 2 == 0, b, pltpu.roll(b, 1, 0)))
        m //= 2
    return out


def _gla_kernel(q_ref, k_ref, v_ref, r_ref, ga_ref, s0_ref, tri_ref, masks_ref, gh_ref,
                og_ref, sfin_ref, st_ref, b_ref, *, heads, dk, dv, scale):
    ci = pl.program_id(1)

    @pl.when(ci == 0)
    def _():
        for h in range(heads):
            st_ref[h] = s0_ref[h].T

    c = q_ref.shape[0]
    b_all = _dot3(tri_ref[...], ga_ref[...])
    b_ref[...] = b_all
    for h in range(heads):
        q = q_ref[:, h * dk:(h + 1) * dk] * scale
        k = k_ref[:, h * dk:(h + 1) * dk]
        v = v_ref[:, h * dv:(h + 1) * dv]
        b = b_all[:, h * dk:(h + 1) * dk]
        a = jnp.where(masks_ref[0:c, :] != 0, _dot_nt(q.astype(BF16), k.astype(BF16)), 0.0)
        for lv, ref in enumerate(_gla_level_refs(b, b_ref, h * dk)):
            d = b - ref
            qt = (q * jnp.exp(jnp.minimum(d, 0.0))).astype(BF16)
            kt = (k * jnp.exp(jnp.minimum(-d, 0.0))).astype(BF16)
            a = a + jnp.where(masks_ref[(lv + 1) * c:(lv + 2) * c, :] != 0, _dot_nt(qt, kt), 0.0)
        s_old = st_ref[h]
        o = _dot(a.astype(BF16), v.astype(BF16)) + _dot_nt((q * jnp.exp(b)).astype(BF16), s_old.astype(BF16))
        b_last = b[c - 1:c, :]
        upd = _dot(v.T.astype(BF16), (k * jnp.exp(b_last - b)).astype(BF16))
        st_ref[h] = s_old * jnp.exp(b_last) + upd
        y = _rms(o, gh_ref[...])
        og_ref[:, h * dv:(h + 1) * dv] = (y * _silu(r_ref[:, h * dv:(h + 1) * dv])).astype(og_ref.dtype)

    @pl.when(ci == pl.num_programs(1) - 1)
    def _():
        for h in range(heads):
            sfin_ref[h] = st_ref[h].T


def _gla(z, ga, s0, g_head, *, batch, seq, chunk, heads, dk, dv, col0):
    n = z.shape[0]
    c = chunk
    nc = seq // c
    wk, wv = heads * dk, heads * dv
    tri, masks = _gla_tables(c)
    row = lambda b, i: b * nc + i
    return pl.pallas_call(
        functools.partial(_gla_kernel, heads=heads, dk=dk, dv=dv, scale=dk ** -0.5),
        out_shape=[jax.ShapeDtypeStruct((n, wv), BF16), jax.ShapeDtypeStruct((batch, heads, dk, dv), F32)],
        grid=(batch, nc),
        in_specs=[pl.BlockSpec((c, wk), lambda b, i: (row(b, i), col0 // wk)),
                  pl.BlockSpec((c, wk), lambda b, i: (row(b, i), col0 // wk + 1)),
                  pl.BlockSpec((c, wv), lambda b, i: (row(b, i), (col0 + 2 * wk) // wv)),
                  pl.BlockSpec((c, wv), lambda b, i: (row(b, i), (col0 + 2 * wk) // wv + 1)),
                  pl.BlockSpec((c, wk), lambda b, i: (row(b, i), 0)),
                  pl.BlockSpec((None, heads, dk, dv), lambda b, i: (b, 0, 0, 0)),
                  pl.BlockSpec(tri.shape, lambda b, i: (0, 0)),
                  pl.BlockSpec(masks.shape, lambda b, i: (0, 0)),
                  pl.BlockSpec((1, dv), lambda b, i: (0, 0))],
        out_specs=[pl.BlockSpec((c, wv), lambda b, i: (row(b, i), 0)),
                   pl.BlockSpec((None, heads, dk, dv), lambda b, i: (b, 0, 0, 0))],
        scratch_shapes=[pltpu.VMEM((heads, dv, dk), F32), pltpu.VMEM((c, wk), F32)],
        compiler_params=_params("arbitrary", "arbitrary"),
        name="gla",
    )(z, z, z, z, ga, s0, jnp.asarray(tri, BF16), jnp.asarray(masks, F32), g_head)


def _out_proj_kernel(a_ref, b_ref, w1_ref, w2_ref, x_ref, o_ref):
    o_ref[...] = x_ref[...] + _dot(a_ref[...], w1_ref[...]) + _dot(b_ref[...], w2_ref[...])


def _out_proj(a, b, w, x, *, layer, tm, tn):
    n, d = x.shape
    kh = a.shape[1]
    return pl.pallas_call(
        _out_proj_kernel,
        out_shape=jax.ShapeDtypeStruct((n, d), F32),
        grid=(n // tm, d // tn),
        in_specs=[pl.BlockSpec((tm, kh), lambda i, j: (i, 0)),
                  pl.BlockSpec((tm, kh), lambda i, j: (i, 0)),
                  pl.BlockSpec((None, kh, tn), lambda i, j: (layer, 0, j)),
                  pl.BlockSpec((None, kh, tn), lambda i, j: (layer, 1, j)),
                  pl.BlockSpec((tm, tn), lambda i, j: (i, j))],
        out_specs=pl.BlockSpec((tm, tn), lambda i, j: (i, j)),
        compiler_params=_params("arbitrary", "arbitrary"),
        name="out_proj",
    )(a, b, w, w, x)


def _cross_attn_kernel(x_ref, g_ref, wq_ref, mk_ref, mv_ref, wo_ref, o_ref, *, heads, dh):
    x = x_ref[...]
    tm, d = x.shape
    xr = x if tm >= SUBLANES else jnp.broadcast_to(x, (SUBLANES, d))
    q = _dot(_rms(xr, g_ref[...]).astype(BF16), wq_ref[...]) * (dh ** -0.5)
    outs = []
    for h in range(heads):
        sl = slice(h * dh, (h + 1) * dh)
        s = _dot_nt(q[:, sl].astype(BF16), mk_ref[:, sl].astype(BF16))
        p = jnp.exp(s - jnp.max(s, axis=-1, keepdims=True))
        p = p / jnp.sum(p, axis=-1, keepdims=True)
        outs.append(_dot(p.astype(BF16), mv_ref[:, sl].astype(BF16)))
    y = _dot(jnp.concatenate(outs, axis=1).astype(BF16), wo_ref[...])
    o_ref[...] = x + y[:tm]


def _cross_attn(x3, g, wq, mk, mv, wo, *, layer, heads, dh, tm):
    bsz, rows, d = x3.shape
    mem = mk.shape[1]
    xd = heads * dh
    return pl.pallas_call(
        functools.partial(_cross_attn_kernel, heads=heads, dh=dh),
        out_shape=jax.ShapeDtypeStruct(x3.shape, F32),
        grid=(bsz, rows // tm),
        in_specs=[pl.BlockSpec((None, tm, d), lambda b, i: (b, i, 0)),
                  pl.BlockSpec((1, d), lambda b, i: (0, 0)),
                  pl.BlockSpec((None, d, xd), lambda b, i: (layer, 0, 0)),
                  pl.BlockSpec((None, mem, xd), lambda b, i: (b, 0, 0)),
                  pl.BlockSpec((None, mem, xd), lambda b, i: (b, 0, 0)),
                  pl.BlockSpec((None, xd, d), lambda b, i: (layer, 0, 0))],
        out_specs=pl.BlockSpec((None, tm, d), lambda b, i: (b, i, 0)),
        compiler_params=_params("arbitrary", "arbitrary"),
        name="cross_attn",
    )(x3, g, wq, mk, mv, wo)


def _swiglu_kernel(x_ref, g_ref, wg_ref, wu_ref, wd_ref, gf_ref, o_ref, hb_ref, acc_ref, *, final_norm):
    f = pl.program_id(1)

    @pl.when(f == 0)
    def _():
        hb_ref[...] = _rms(x_ref[...], g_ref[...]).astype(BF16)
        acc_ref[...] = jnp.zeros_like(acc_ref)

    hb = hb_ref[...]
    act = _silu(_dot(hb, wg_ref[...])) * _dot(hb, wu_ref[...])
    acc_ref[...] += _dot(act.astype(BF16), wd_ref[...])

    @pl.when(f == pl.num_programs(1) - 1)
    def _():
        y = x_ref[...] + acc_ref[...]
        o_ref[...] = _rms(y, gf_ref[...]) if final_norm else y


def _swiglu(x, g, wg, wu, wd, g_final, *, layer, final_norm, tm, tf):
    n, d = x.shape
    ffn = wg.shape[2]
    return pl.pallas_call(
        functools.partial(_swiglu_kernel, final_norm=final_norm),
        out_shape=jax.ShapeDtypeStruct((n, d), F32),
        grid=(n // tm, ffn // tf),
        in_specs=[pl.BlockSpec((tm, d), lambda i, f: (i, 0)),
                  pl.BlockSpec((1, d), lambda i, f: (0, 0)),
                  pl.BlockSpec((None, d, tf), lambda i, f: (layer, 0, f)),
                  pl.BlockSpec((None, d, tf), lambda i, f: (layer, 0, f)),
                  pl.BlockSpec((None, tf, d), lambda i, f: (layer, f, 0)),
                  pl.BlockSpec((1, d), lambda i, f: (0, 0))],
        out_specs=pl.BlockSpec((tm, d), lambda i, f: (i, 0)),
        scratch_shapes=[pltpu.VMEM((tm, d), BF16), pltpu.VMEM((tm, d), F32)],
        compiler_params=_params("arbitrary", "arbitrary"),
        name="swiglu",
    )(x, g, wg, wu, wd, g_final)


def _suffix_rows_exclusive(x):
    n = x.shape[0]
    row = lax.broadcasted_iota(jnp.int32, x.shape, 0)

    def up(a, k):
        return jnp.where(row + k < n, pltpu.roll(a, n - k, 0), 0.0)

    y = up(x, 1)
    k = 1
    while k < n:
        y = y + up(y, k)
        k *= 2
    return y


def _fox_decode_kernel(pt_ref, q_ref, kn_ref, vn_ref, cq_ref, *rest, heads, group, scale):
    del pt_ref
    k_refs, v_refs, f_refs = rest[:group], rest[group:2 * group], rest[2 * group:3 * group]
    u_same_ref, u_later_ref, o_ref, m_ref, l_ref, acc_ref, tail_ref = rest[3 * group:]
    step = pl.program_id(1)

    @pl.when(step == 0)
    def _():
        m_ref[...] = jnp.full_like(m_ref, -jnp.inf)
        l_ref[...] = jnp.zeros_like(l_ref)
        acc_ref[...] = jnp.zeros_like(acc_ref)
        tail_ref[...] = jnp.zeros_like(tail_ref)

    n_blk = k_refs[0].shape[0] // LANES
    wide = MXU_COLS // LANES
    q = q_ref[...]
    qb = q.astype(BF16)
    sub = lax.broadcasted_iota(jnp.int32, (heads, MXU_COLS), 0)
    lane = lax.broadcasted_iota(jnp.int32, (heads, MXU_COLS), 1)
    own = sub == lane % heads
    parts = _split3(jnp.concatenate([f[...] for f in f_refs], axis=0))
    rows_later = sum(_dot(x, u_later_ref[...]) for x in parts)
    rows_total = sum(_dot(x, u_same_ref[...]) for x in parts)
    tail = tail_ref[...]
    tiles = []
    for g in range(group):
        row_total = rows_total[g * n_blk:(g + 1) * n_blk]
        bias = cq_ref[...] + tail + rows_later[g * n_blk:(g + 1) * n_blk] + _suffix_rows_exclusive(row_total)
        tail = tail + jnp.sum(row_total, axis=0, keepdims=True)
        for r in range(0, n_blk, wide):
            kb = k_refs[g][r * LANES:(r + wide) * LANES, :].astype(BF16)
            brow = jnp.concatenate([bias[r + w:r + w + 1, :] for w in range(wide)], axis=1)
            s = jnp.where(own, _dot_nt(qb, kb) * scale + brow, -jnp.inf)
            tiles.extend(s[:, w * LANES:(w + 1) * LANES] for w in range(wide))
    tail_ref[...] = tail

    top = tiles[0]
    for t in tiles[1:]:
        top = jnp.maximum(top, t)
    m_old = m_ref[...]
    m_new = jnp.maximum(m_old, jnp.max(top, axis=-1, keepdims=True))
    a = jnp.exp(m_old - m_new)
    accs = [a * acc_ref[...], jnp.zeros(acc_ref.shape, F32)]
    psum = jnp.zeros((heads, LANES), F32)
    dh = acc_ref.shape[1]
    for n, idx in enumerate(range(0, len(tiles), wide)):
        g, r = divmod(idx, n_blk)
        prs = [jnp.exp(tiles[idx + w] - m_new) for w in range(wide)]
        for pr in prs:
            psum = psum + pr
        vb = jnp.concatenate([v_refs[g][(r + w) * LANES:(r + w + 1) * LANES, :].astype(BF16) for w in range(wide)],
                             axis=1)
        res = _dot(jnp.concatenate(prs, axis=0).astype(BF16), vb)
        for w in range(wide):
            accs[n % 2] = accs[n % 2] + res[w * heads:(w + 1) * heads, w * dh:(w + 1) * dh]
    m_ref[...] = m_new
    l_ref[...] = a * l_ref[...] + jnp.sum(psum, axis=-1, keepdims=True)
    acc_ref[...] = accs[0] + accs[1]

    @pl.when(step == pl.num_programs(1) - 1)
    def _():
        s_new = jnp.sum(q * kn_ref[...], axis=-1, keepdims=True) * scale
        m_fin = jnp.maximum(m_ref[...], s_new)
        a_fin = jnp.exp(m_ref[...] - m_fin)
        p_new = jnp.exp(s_new - m_fin)
        o_ref[...] = ((a_fin * acc_ref[...] + p_new * vn_ref[...]) / (a_fin * l_ref[...] + p_new)).astype(o_ref.dtype)


def _fox_decode(q, k_new, v_new, cq, cache_k, cache_v, cache_f, page_table, *, layer, group):
    bsz, heads, dh = q.shape
    rows = cache_k.shape[2]
    n_pages = page_table.shape[1]
    lane = np.arange(LANES)
    same = lane[:, None] % heads == lane[None, :] % heads
    u_same = jnp.asarray(same, BF16)
    u_later = jnp.asarray(same & (lane[:, None] > lane[None, :]), BF16)
    vec = pl.BlockSpec((None, heads, dh), lambda b, s, pt: (b, 0, 0))

    def paged(shape, g):
        return pl.BlockSpec((None, None) + shape, lambda b, s, pt: (layer, pt[b, n_pages - 1 - (s * group + g)], 0, 0))

    const = pl.BlockSpec((LANES, LANES), lambda b, s, pt: (0, 0))
    grid_spec = pltpu.PrefetchScalarGridSpec(
        num_scalar_prefetch=1,
        grid=(bsz, n_pages // group),
        in_specs=([vec, vec, vec, pl.BlockSpec((None, 1, LANES), lambda b, s, pt: (b, 0, 0))]
                  + [paged((rows, dh), g) for g in range(group)]
                  + [paged((rows, dh), g) for g in range(group)]
                  + [paged((rows // LANES, LANES), g) for g in range(group)]
                  + [const, const]),
        out_specs=vec,
        scratch_shapes=[pltpu.VMEM((heads, 1), F32), pltpu.VMEM((heads, 1), F32),
                        pltpu.VMEM((heads, dh), F32), pltpu.VMEM((1, LANES), F32)],
    )
    return pl.pallas_call(
        functools.partial(_fox_decode_kernel, heads=heads, group=group, scale=dh ** -0.5),
        out_shape=jax.ShapeDtypeStruct((bsz, heads, dh), BF16),
        grid_spec=grid_spec,
        compiler_params=_params("arbitrary", "arbitrary"),
        name="fox_decode",
    )(page_table, q, k_new, v_new, cq, *([cache_k] * group), *([cache_v] * group), *([cache_f] * group),
      u_same, u_later)


def kernel(x_prompt, x_sample, mem_prompt, cache_fox_k, cache_fox_v, cache_fox_logf, state_gla, cache_mem_k, cache_mem_v, page_table, g_mix, w_in, b_forget, w_gla_gate, b_gla_gate, g_gla_head, w_out, g_cross, g_mem, w_xq, w_xk, w_xv, w_xo, g_ffn, w_ffn_gate, w_ffn_up, w_ffn_down, g_final):
    depth = w_in.shape[0]
    bsz, seq, d = x_prompt.shape
    sb = x_sample.shape[0]
    mem = mem_prompt.shape[1]
    _, pool, page, fh, fdh = cache_fox_k.shape
    _, _, gh, gdk, gdv = state_gla.shape
    xh, xdh = cache_mem_k.shape[3], cache_mem_k.shape[4]
    fox_dim, gk_dim, gv_dim = fh * fdh, gh * gdk, gh * gdv
    rank = w_gla_gate.shape[1]
    n = bsz * seq
    c_forget = 3 * fox_dim
    c_gla = c_forget + fh
    c_rank = c_gla + 2 * gk_dim + 2 * gv_dim

    xp = x_prompt.reshape(n, d)
    xs = x_sample.reshape(sb, d)
    memx = mem_prompt.reshape(bsz * mem, d)
    ck = cache_fox_k.reshape(depth, pool, page * fh, fdh)
    cv = cache_fox_v.reshape(depth, pool, page * fh, fdh)
    cf = cache_fox_logf.reshape(depth, pool, page * fh // LANES, LANES)
    row = lambda v: v.reshape(1, -1)
    zeros_state = jnp.zeros((bsz, gh, gdk, gdv), F32)
    group = min(DECODE_PAGES_PER_STEP, page_table.shape[1])

    w_main = jnp.concatenate([w_in[:, :, :c_forget], w_in[:, :, c_gla:c_rank]], axis=2).astype(BF16)
    w_small = jnp.concatenate([w_in[:, :, c_forget:c_gla], w_in[:, :, c_rank:],
                               jnp.zeros((depth, d, LANES - fh - rank), F32)], axis=2).astype(BF16)
    w_g2 = jnp.concatenate([jnp.zeros((depth, fh, gk_dim), F32), w_gla_gate,
                            jnp.zeros((depth, LANES - fh - rank, gk_dim), F32)], axis=1).astype(BF16)
    b_f = jnp.concatenate([b_forget, jnp.zeros((depth, LANES - fh), F32)], axis=1)
    w_o, w_q, w_xo_b = w_out.astype(BF16), w_xq.astype(BF16), w_xo.astype(BF16)
    w_kv = jnp.concatenate([w_xk, w_xv], axis=2).astype(BF16)
    w_g, w_u, w_d = w_ffn_gate.astype(BF16), w_ffn_up.astype(BF16), w_ffn_down.astype(BF16)

    pk, pv, pf, ps, pmk, pmv, sk, sv, sf, ss = ([] for _ in range(10))
    for l in range(depth):
        last = l == depth - 1

        z, logf, ga, k3, v3, drow = _in_proj(xp, row(g_mix[l]), w_main, w_small, b_f[l:l + 1], w_g2,
                                             row(b_gla_gate[l]), layer=l, n_fox_heads=fh, fox_dh=fdh, seq=seq,
                                             tm=min(ROW_TILE_PROJ, seq), tn=COL_TILE)
        o_f = _fox_prompt(z, drow, batch=bsz, seq=seq, heads=fh, dh=fdh, tq=min(FOX_Q_TILE, seq),
                          hp=FOX_HEADS_PER_STEP)
        o_g, s_fin = _gla(z, ga, zeros_state, row(g_gla_head[l]), batch=bsz, seq=seq, chunk=min(GLA_CHUNK, seq),
                          heads=gh, dk=gdk, dv=gdv, col0=3 * fox_dim)
        xp = _out_proj(o_f, o_g, w_o, xp, layer=l, tm=min(ROW_TILE_PROJ, n), tn=COL_TILE)
        mkv = _norm_matmul(memx, row(g_mem[l]), w_kv, layer=l, tm=bsz * mem, tn=COL_TILE)
        mk, mv = mkv[:, :xh * xdh], mkv[:, xh * xdh:]
        xp = _cross_attn(xp.reshape(bsz, seq, d), row(g_cross[l]), w_q, mk.reshape(bsz, mem, -1),
                         mv.reshape(bsz, mem, -1), w_xo_b, layer=l, heads=xh, dh=xdh,
                         tm=min(ROW_TILE_FFN, seq)).reshape(n, d)
        xp = _swiglu(xp, row(g_ffn[l]), w_g, w_u, w_d, row(g_final), layer=l, final_norm=last,
                     tm=min(ROW_TILE_FFN, n), tf=COL_TILE)
        pk.append(k3.reshape(bsz, seq, fh, fdh))
        pv.append(v3.reshape(bsz, seq, fh, fdh))
        pf.append(logf.reshape(bsz, seq, fh))
        ps.append(s_fin)
        pmk.append(mk.reshape(bsz, mem, xh, xdh))
        pmv.append(mv.reshape(bsz, mem, xh, xdh))

        zs, logf_s, ga_s, fk, fv = _in_proj(xs, row(g_mix[l]), w_main, w_small, b_f[l:l + 1], w_g2,
                                            row(b_gla_gate[l]), layer=l, n_fox_heads=fh, fox_dh=fdh, seq=0,
                                            tm=sb, tn=COL_TILE)
        fq, fk, fv = zs[:, :fox_dim].reshape(sb, fh, fdh), fk.reshape(sb, fh, fdh), fv.reshape(sb, fh, fdh)
        cq = jnp.tile(logf_s, (1, LANES // fh)).reshape(sb, 1, LANES)
        o_fs = _fox_decode(fq, fk, fv, cq, ck, cv, cf, page_table, layer=l, group=group).reshape(sb, fox_dim)
        cs = GLA_CHUNK_SAMPLE
        pad = lambda a: jnp.zeros((sb, cs, a.shape[1]), a.dtype).at[:, 0].set(a).reshape(sb * cs, -1)
        o_gs, s_new = _gla(pad(zs), pad(ga_s), state_gla[l], row(g_gla_head[l]), batch=sb, seq=cs, chunk=cs,
                           heads=gh, dk=gdk, dv=gdv, col0=3 * fox_dim)
        o_gs = o_gs.reshape(sb, cs, gv_dim)[:, 0]
        xs = _out_proj(o_fs, o_gs, w_o, xs, layer=l, tm=sb, tn=COL_TILE)
        xs = _cross_attn(xs.reshape(sb, 1, d), row(g_cross[l]), w_q, cache_mem_k[l].reshape(sb, mem, -1),
                         cache_mem_v[l].reshape(sb, mem, -1), w_xo_b, layer=l, heads=xh, dh=xdh, tm=1).reshape(sb, d)
        xs = _swiglu(xs, row(g_ffn[l]), w_g, w_u, w_d, row(g_final), layer=l, final_norm=last, tm=sb, tf=COL_TILE)
        sk.append(fk.reshape(sb, 1, fh, fdh))
        sv.append(fv.reshape(sb, 1, fh, fdh))
        sf.append(logf_s.reshape(sb, 1, fh))
        ss.append(s_new)

    return (xp.reshape(bsz, seq, d), xs.reshape(sb, 1, d),
            jnp.stack(pk), jnp.stack(pv), jnp.stack(pf), jnp.stack(ps), jnp.stack(pmk), jnp.stack(pmv),
            jnp.stack(sk), jnp.stack(sv), jnp.stack(sf), jnp.stack(ss))
```

```python
import functools

import numpy as np
import jax
import jax.numpy as jnp
from jax import lax
from jax.experimental import pallas as pl
from jax.experimental.pallas import tpu as pltpu

F32 = jnp.float32
BF16 = jnp.bfloat16

RMS_EPS = 1e-6
LOG2E = 1.4426950408889634
GLA_TAU = 16.0
GLA_CHUNK = 128
GLA_CHUNK_SAMPLE = 64
LANES = 128
SUBLANES = 8
VMEM_LIMIT_BYTES = 56 * 1024 * 1024
ROW_TILE_PROJ = 1024
ROW_TILE_XATTN = 512
ROW_TILE_FFN = 1024
FFN_TILE = 256
COL_TILE = 512
FOX_Q_TILE = 512
FOX_HEADS_PER_STEP = 2
DECODE_PAGES_PER_STEP = 8
MXU_COLS = 256


def _params(*sem):
    return pltpu.CompilerParams(dimension_semantics=sem, vmem_limit_bytes=VMEM_LIMIT_BYTES)


def _rms(x, g):
    return x * lax.rsqrt(jnp.mean(x * x, axis=-1, keepdims=True) + RMS_EPS) * g


def _log_sigmoid(x):
    return jnp.minimum(x, 0.0) - jnp.log1p(jnp.exp(-jnp.abs(x)))


def _silu(x):
    return x / (1.0 + jnp.exp(-x))


def _dot(a, b):
    return jnp.dot(a, b, preferred_element_type=F32)


def _dot_nt(a, b):
    return lax.dot_general(a, b, (((1,), (1,)), ((), ())), preferred_element_type=F32)


def _dot_tn(a, b):
    return lax.dot_general(a, b, (((0,), (0,)), ((), ())), preferred_element_type=F32)


def _split3(x):
    a = x.astype(BF16)
    r = x - a.astype(F32)
    b = r.astype(BF16)
    c = (r - b.astype(F32)).astype(BF16)
    return a, b, c


def _dot3(m, x, dot=_dot):
    a, b, c = _split3(x)
    return dot(m, a) + dot(m, b) + dot(m, c)


def _in_proj_kernel(x_ref, g_ref, w_ref, ws_ref, bf_ref, wg2_ref, bg_ref, *rest, n_fox_heads, tiles_per_seq):
    if tiles_per_seq:
        z_ref, logf_ref, ga_ref, k3_ref, v3_ref, drow_ref, hb_ref, carry_ref = rest
    else:
        z_ref, logf_ref, ga_ref, k3_ref, v3_ref, hb_ref = rest
    i = pl.program_id(0)
    j = pl.program_id(1)

    @pl.when(j == 0)
    def _():
        hb = _rms(x_ref[...], g_ref[...]).astype(BF16)
        hb_ref[...] = hb
        zs = _dot(hb, ws_ref[...])
        lf = _log_sigmoid(zs + bf_ref[...])
        logf_ref[...] = lf[:, :n_fox_heads]
        gp = _dot(zs.astype(BF16), wg2_ref[...]) + bg_ref[...]
        ga_ref[...] = _log_sigmoid(gp) * (1.0 / GLA_TAU)
        if tiles_per_seq:
            tm = lf.shape[0]
            r = lax.broadcasted_iota(jnp.int32, (LANES, LANES), 0)
            c = lax.broadcasted_iota(jnp.int32, (LANES, LANES), 1)
            tri = jnp.where(r <= c, 1.0, 0.0).astype(BF16)
            lft = lf.T[:SUBLANES]
            run = jnp.where(i % tiles_per_seq == 0, 0.0, carry_ref[...])
            for c0 in range(0, tm, LANES):
                d = _dot3(tri, lft[:, c0:c0 + LANES], dot=lambda m, x: _dot(x, m)) + run
                drow_ref[:, c0:c0 + LANES] = d
                run = d[:, LANES - 1:LANES]
            carry_ref[...] = run

    zt = _dot(hb_ref[...], w_ref[...])
    z_ref[...] = zt
    tm, tn = zt.shape
    dh = k3_ref.shape[1]
    heads_per_tile = tn // dh
    tiles_per_group = n_fox_heads // heads_per_tile
    for group, out_ref in ((1, k3_ref), (2, v3_ref)):
        for t in range(tiles_per_group):
            @pl.when(j == group * tiles_per_group + t)
            def _(out_ref=out_ref, t=t):
                for c in range(heads_per_tile):
                    out_ref[pl.ds(t * heads_per_tile + c, tm, stride=n_fox_heads), :] = zt[:, c * dh:(c + 1) * dh]


def _in_proj(x, g, w_main, w_small, b_f, w_g2, b_g, *, layer, n_fox_heads, fox_dh, seq, tm, tn):
    n, d = x.shape
    nz = w_main.shape[2]
    kg = w_g2.shape[2]
    tiles_per_seq = seq // tm if seq else 0
    out_shape = [jax.ShapeDtypeStruct((n, nz), F32),
                 jax.ShapeDtypeStruct((n, n_fox_heads), F32),
                 jax.ShapeDtypeStruct((n, kg), F32),
                 jax.ShapeDtypeStruct((n * n_fox_heads, fox_dh), F32),
                 jax.ShapeDtypeStruct((n * n_fox_heads, fox_dh), F32)]
    out_specs = [pl.BlockSpec((tm, tn), lambda i, j: (i, j)),
                 pl.BlockSpec((tm, n_fox_heads), lambda i, j: (i, 0)),
                 pl.BlockSpec((tm, kg), lambda i, j: (i, 0)),
                 pl.BlockSpec((tm * n_fox_heads, fox_dh), lambda i, j: (i, 0)),
                 pl.BlockSpec((tm * n_fox_heads, fox_dh), lambda i, j: (i, 0))]
    scratch = [pltpu.VMEM((tm, d), BF16)]
    if tiles_per_seq:
        out_shape.append(jax.ShapeDtypeStruct((n // seq, SUBLANES, seq), F32))
        out_specs.append(pl.BlockSpec((None, SUBLANES, tm), lambda i, j: (i // tiles_per_seq, 0, i % tiles_per_seq)))
        scratch.append(pltpu.VMEM((SUBLANES, 1), F32))
    return pl.pallas_call(
        functools.partial(_in_proj_kernel, n_fox_heads=n_fox_heads, tiles_per_seq=tiles_per_seq),
        out_shape=out_shape,
        grid=(n // tm, nz // tn),
        in_specs=[pl.BlockSpec((tm, d), lambda i, j: (i, 0)),
                  pl.BlockSpec((1, d), lambda i, j: (0, 0)),
                  pl.BlockSpec((None, d, tn), lambda i, j: (layer, 0, j)),
                  pl.BlockSpec((None, d, LANES), lambda i, j: (layer, 0, 0)),
                  pl.BlockSpec((1, LANES), lambda i, j: (0, 0)),
                  pl.BlockSpec((None, LANES, kg), lambda i, j: (layer, 0, 0)),
                  pl.BlockSpec((1, kg), lambda i, j: (0, 0))],
        out_specs=out_specs,
        scratch_shapes=scratch,
        compiler_params=_params("arbitrary", "arbitrary"),
        name="in_proj",
    )(x, g, w_main, w_small, b_f, w_g2, b_g)


def _norm_matmul_kernel(x_ref, g_ref, w_ref, o_ref, hb_ref):
    @pl.when(pl.program_id(1) == 0)
    def _():
        hb_ref[...] = _rms(x_ref[...], g_ref[...]).astype(BF16)

    o_ref[...] = _dot(hb_ref[...], w_ref[...].astype(BF16))


def _norm_matmul(x, g, w, *, layer, tm, tn):
    n, d = x.shape
    nout = w.shape[2]
    return pl.pallas_call(
        _norm_matmul_kernel,
        out_shape=jax.ShapeDtypeStruct((n, nout), F32),
        grid=(n // tm, nout // tn),
        in_specs=[pl.BlockSpec((tm, d), lambda i, j: (i, 0)),
                  pl.BlockSpec((1, d), lambda i, j: (0, 0)),
                  pl.BlockSpec((None, d, tn), lambda i, j: (layer, 0, j))],
        out_specs=pl.BlockSpec((tm, tn), lambda i, j: (i, j)),
        scratch_shapes=[pltpu.VMEM((tm, d), BF16)],
        compiler_params=_params("arbitrary", "arbitrary"),
        name="norm_matmul",
    )(x, g, w)


def _fox_prompt_kernel(q_ref, k_ref, v_ref, d_ref, o_ref, kb_ref, vb_ref, *, tq, dh, scale):
    hg = pl.program_id(1)
    qi = pl.program_id(2)
    hp = q_ref.shape[1] // dh

    @pl.when(qi == 0)
    def _():
        kb_ref[...] = k_ref[...].astype(BF16)
        vb_ref[...] = v_ref[...].astype(BF16)

    qs = [(q_ref[:, i * dh:(i + 1) * dh] * (scale * LOG2E)).astype(BF16) for i in range(hp)]

    def block(j, carry, masked):
        start = pl.multiple_of(j * tq, tq)
        out = []
        for i in range(hp):
            m, l, acc = carry[i]
            kj = kb_ref[pl.ds(start, tq), i * dh:(i + 1) * dh]
            vj = vb_ref[pl.ds(start, tq), i * dh:(i + 1) * dh]
            s = _dot_nt(qs[i], kj) - d_ref[pl.ds(hg * hp + i, 1), pl.ds(start, tq)] * LOG2E
            if masked:
                r = lax.broadcasted_iota(jnp.int32, (tq, tq), 0)
                c = lax.broadcasted_iota(jnp.int32, (tq, tq), 1)
                s = jnp.where(c <= r, s, -jnp.inf)
            m_new = jnp.maximum(m, jnp.max(s, axis=-1, keepdims=True))
            a = jnp.exp2(m - m_new)
            p = jnp.exp2(s - m_new)
            l = a * l + jnp.sum(p, axis=-1, keepdims=True)
            acc = a * acc + _dot(p.astype(BF16), vj)
            out.append((m_new, l, acc))
        return tuple(out)

    init = tuple((jnp.full((tq, 1), -jnp.inf, F32), jnp.zeros((tq, 1), F32), jnp.zeros((tq, dh), F32))
                 for _ in range(hp))
    carry = lax.fori_loop(0, qi, lambda j, c: block(j, c, False), init)
    final = block(qi, carry, True)
    for i, (_, l, acc) in enumerate(final):
        o_ref[:, i * dh:(i + 1) * dh] = (acc / l).astype(o_ref.dtype)


def _fox_prompt(z, drow, *, batch, seq, heads, dh, tq, hp):
    n = z.shape[0]
    nq = seq // tq
    hg = heads // hp
    w = hp * dh
    return pl.pallas_call(
        functools.partial(_fox_prompt_kernel, tq=tq, dh=dh, scale=dh ** -0.5),
        out_shape=jax.ShapeDtypeStruct((n, heads * dh), BF16),
        grid=(batch, hg, nq),
        in_specs=[pl.BlockSpec((tq, w), lambda b, h, i: (b * nq + i, h)),
                  pl.BlockSpec((seq, w), lambda b, h, i: (b, hg + h)),
                  pl.BlockSpec((seq, w), lambda b, h, i: (b, 2 * hg + h)),
                  pl.BlockSpec((None, SUBLANES, seq), lambda b, h, i: (b, 0, 0))],
        out_specs=pl.BlockSpec((tq, w), lambda b, h, i: (b * nq + i, h)),
        scratch_shapes=[pltpu.VMEM((seq, w), BF16), pltpu.VMEM((seq, w), BF16)],
        compiler_params=_params("arbitrary", "arbitrary", "arbitrary"),
        name="fox_prompt",
    )(z, z, z, drow)


def _gla_tables(c):
    r = np.arange(c)
    tri = (r[None, :] <= r[:, None]).astype(np.float32)
    masks = [np.eye(c, dtype=np.float32)]
    m = c // 2
    while m >= 1:
        same = (r[:, None] // (2 * m)) == (r[None, :] // (2 * m))
        upper = (r[:, None] % (2 * m)) >= m
        lower = (r[None, :] % (2 * m)) < m
        masks.append((same & upper & lower).astype(np.float32))
        m //= 2
    return tri, np.concatenate(masks, 0)


def _gla_level_refs(b, b_ref, col0):
    c, dk = b.shape
    bcast = lambda i, rows: jnp.broadcast_to(b_ref[pl.ds(i, 1), pl.ds(col0, dk)], (rows, dk))
    sub = lax.broadcasted_iota(jnp.int32, (c, dk), 0) % SUBLANES
    out = []
    m = c // 2
    while m >= 1:
        if 2 * m >= SUBLANES:
            out.append(jnp.concatenate([bcast(blk * 2 * m + m - 1, 2 * m) for blk in range(c // (2 * m))], axis=0))
        elif m > 1:
            ref = None
            for t in range(SUBLANES // (2 * m)):
                rows = jnp.concatenate([bcast(g * SUBLANES + t * 2 * m + m - 1, SUBLANES)
                                        for g in range(c // SUBLANES)], axis=0)
                ref = rows if ref is None else jnp.where(sub // (2 * m) == t, rows, ref)
            out.append(ref)
        else:
            out.append(jnp.where(sub % 2 == 0, b, pltpu.roll(b, 1, 0)))
        m //= 2
    return out


def _gla_kernel(q_ref, k_ref, v_ref, r_ref, ga_ref, s0_ref, tri_ref, masks_ref, gh_ref,
                og_ref, sfin_ref, st_ref, b_ref, *, heads, dk, dv, scale):
    ci = pl.program_id(1)

    @pl.when(ci == 0)
    def _():
        for h in range(heads):
            st_ref[h] = s0_ref[h].T

    c = q_ref.shape[0]
    b_all = _dot3(tri_ref[...], ga_ref[...])
    b_ref[...] = b_all
    for h in range(heads):
        q = q_ref[:, h * dk:(h + 1) * dk] * scale
        k = k_ref[:, h * dk:(h + 1) * dk]
        v = v_ref[:, h * dv:(h + 1) * dv]
        b = b_all[:, h * dk:(h + 1) * dk]
        a = jnp.where(masks_ref[0:c, :] != 0, jnp.sum(q * k, axis=-1, keepdims=True), 0.0)
        for lv, ref in enumerate(_gla_level_refs(b, b_ref, h * dk)):
            d = b - ref
            qt = (q * jnp.exp(jnp.minimum(d, 0.0))).astype(BF16)
            kt = (k * jnp.exp(jnp.minimum(-d, 0.0))).astype(BF16)
            a = a + jnp.where(masks_ref[(lv + 1) * c:(lv + 2) * c, :] != 0, _dot_nt(qt, kt), 0.0)
        s_old = st_ref[h]
        o = _dot(a.astype(BF16), v.astype(BF16)) + _dot_nt((q * jnp.exp(b)).astype(BF16), s_old.astype(BF16))
        b_last = b[c - 1:c, :]
        upd = _dot(v.T.astype(BF16), (k * jnp.exp(b_last - b)).astype(BF16))
        st_ref[h] = s_old * jnp.exp(b_last) + upd
        y = _rms(o, gh_ref[...])
        og_ref[:, h * dv:(h + 1) * dv] = (y * _silu(r_ref[:, h * dv:(h + 1) * dv])).astype(og_ref.dtype)

    @pl.when(ci == pl.num_programs(1) - 1)
    def _():
        for h in range(heads):
            sfin_ref[h] = st_ref[h].T


def _gla(z, ga, s0, g_head, *, batch, seq, chunk, heads, dk, dv, col0):
    n = z.shape[0]
    c = chunk
    nc = seq // c
    wk, wv = heads * dk, heads * dv
    tri, masks = _gla_tables(c)
    row = lambda b, i: b * nc + i
    return pl.pallas_call(
        functools.partial(_gla_kernel, heads=heads, dk=dk, dv=dv, scale=dk ** -0.5),
        out_shape=[jax.ShapeDtypeStruct((n, wv), BF16), jax.ShapeDtypeStruct((batch, heads, dk, dv), F32)],
        grid=(batch, nc),
        in_specs=[pl.BlockSpec((c, wk), lambda b, i: (row(b, i), col0 // wk)),
                  pl.BlockSpec((c, wk), lambda b, i: (row(b, i), col0 // wk + 1)),
                  pl.BlockSpec((c, wv), lambda b, i: (row(b, i), (col0 + 2 * wk) // wv)),
                  pl.BlockSpec((c, wv), lambda b, i: (row(b, i), (col0 + 2 * wk) // wv + 1)),
                  pl.BlockSpec((c, wk), lambda b, i: (row(b, i), 0)),
                  pl.BlockSpec((None, heads, dk, dv), lambda b, i: (b, 0, 0, 0)),
                  pl.BlockSpec(tri.shape, lambda b, i: (0, 0)),
                  pl.BlockSpec(masks.shape, lambda b, i: (0, 0)),
                  pl.BlockSpec((1, dv), lambda b, i: (0, 0))],
        out_specs=[pl.BlockSpec((c, wv), lambda b, i: (row(b, i), 0)),
                   pl.BlockSpec((None, heads, dk, dv), lambda b, i: (b, 0, 0, 0))],
        scratch_shapes=[pltpu.VMEM((heads, dv, dk), F32), pltpu.VMEM((c, wk), F32)],
        compiler_params=_params("arbitrary", "arbitrary"),
        name="gla",
    )(z, z, z, z, ga, s0, jnp.asarray(tri, BF16), jnp.asarray(masks, F32), g_head)


def _out_proj_kernel(a_ref, b_ref, w1_ref, w2_ref, x_ref, o_ref):
    o_ref[...] = x_ref[...] + _dot(a_ref[...], w1_ref[...].astype(BF16)) + _dot(b_ref[...], w2_ref[...].astype(BF16))


def _out_proj(a, b, w, x, *, layer, tm, tn):
    n, d = x.shape
    kh = a.shape[1]
    return pl.pallas_call(
        _out_proj_kernel,
        out_shape=jax.ShapeDtypeStruct((n, d), F32),
        grid=(n // tm, d // tn),
        in_specs=[pl.BlockSpec((tm, kh), lambda i, j: (i, 0)),
                  pl.BlockSpec((tm, kh), lambda i, j: (i, 0)),
                  pl.BlockSpec((None, kh, tn), lambda i, j: (layer, 0, j)),
                  pl.BlockSpec((None, kh, tn), lambda i, j: (layer, 1, j)),
                  pl.BlockSpec((tm, tn), lambda i, j: (i, j))],
        out_specs=pl.BlockSpec((tm, tn), lambda i, j: (i, j)),
        compiler_params=_params("arbitrary", "arbitrary"),
        name="out_proj",
    )(a, b, w, w, x)


def _cross_attn_kernel(x_ref, g_ref, wq_ref, mk_ref, mv_ref, wo_ref, o_ref, *, heads, dh):
    x = x_ref[...]
    tm, d = x.shape
    xr = x if tm >= SUBLANES else jnp.broadcast_to(x, (SUBLANES, d))
    q = _dot(_rms(xr, g_ref[...]).astype(BF16), wq_ref[...].astype(BF16)) * (dh ** -0.5)
    outs = []
    for h in range(heads):
        sl = slice(h * dh, (h + 1) * dh)
        s = _dot_nt(q[:, sl].astype(BF16), mk_ref[:, sl].astype(BF16))
        p = jnp.exp(s - jnp.max(s, axis=-1, keepdims=True))
        p = p / jnp.sum(p, axis=-1, keepdims=True)
        outs.append(_dot(p.astype(BF16), mv_ref[:, sl].astype(BF16)))
    y = _dot(jnp.concatenate(outs, axis=1).astype(BF16), wo_ref[...].astype(BF16))
    o_ref[...] = x + y[:tm]


def _cross_attn(x3, g, wq, mk, mv, wo, *, layer, heads, dh, tm):
    bsz, rows, d = x3.shape
    mem = mk.shape[1]
    xd = heads * dh
    return pl.pallas_call(
        functools.partial(_cross_attn_kernel, heads=heads, dh=dh),
        out_shape=jax.ShapeDtypeStruct(x3.shape, F32),
        grid=(bsz, rows // tm),
        in_specs=[pl.BlockSpec((None, tm, d), lambda b, i: (b, i, 0)),
                  pl.BlockSpec((1, d), lambda b, i: (0, 0)),
                  pl.BlockSpec((None, d, xd), lambda b, i: (layer, 0, 0)),
                  pl.BlockSpec((None, mem, xd), lambda b, i: (b, 0, 0)),
                  pl.BlockSpec((None, mem, xd), lambda b, i: (b, 0, 0)),
                  pl.BlockSpec((None, xd, d), lambda b, i: (layer, 0, 0))],
        out_specs=pl.BlockSpec((None, tm, d), lambda b, i: (b, i, 0)),
        compiler_params=_params("arbitrary", "arbitrary"),
        name="cross_attn",
    )(x3, g, wq, mk, mv, wo)


def _swiglu_kernel(x_ref, xs_ref, g_ref, wg_ref, wu_ref, wd_ref, gf_ref, o_ref, os_ref, hb_ref, hs_ref, *, final_norm):
    i = pl.program_id(0)
    f = pl.program_id(1)
    last = pl.num_programs(1) - 1

    @pl.when(f == 0)
    def _():
        x = x_ref[...]
        hb_ref[...] = _rms(x, g_ref[...]).astype(BF16)
        o_ref[...] = x

    wg, wu, wd = wg_ref[...].astype(BF16), wu_ref[...].astype(BF16), wd_ref[...].astype(BF16)

    def ffn(hb):
        return _dot((_silu(_dot(hb, wg)) * _dot(hb, wu)).astype(BF16), wd)

    o_ref[...] += ffn(hb_ref[...])

    @pl.when(f == last)
    def _():
        if final_norm:
            o_ref[...] = _rms(o_ref[...], gf_ref[...])

    @pl.when(i == 0)
    def _():
        @pl.when(f == 0)
        def _():
            xs = xs_ref[...]
            hs_ref[...] = _rms(xs, g_ref[...]).astype(BF16)
            os_ref[...] = xs

        os_ref[...] += ffn(hs_ref[...])

        @pl.when(f == last)
        def _():
            if final_norm:
                os_ref[...] = _rms(os_ref[...], gf_ref[...])


def _swiglu(x, xs, g, wg, wu, wd, g_final, *, layer, final_norm, tm, tf):
    n, d = x.shape
    ns = xs.shape[0]
    ffn = wg.shape[2]
    return pl.pallas_call(
        functools.partial(_swiglu_kernel, final_norm=final_norm),
        out_shape=[jax.ShapeDtypeStruct((n, d), F32), jax.ShapeDtypeStruct((ns, d), F32)],
        grid=(n // tm, ffn // tf),
        in_specs=[pl.BlockSpec((tm, d), lambda i, f: (i, 0)),
                  pl.BlockSpec((ns, d), lambda i, f: (0, 0)),
                  pl.BlockSpec((1, d), lambda i, f: (0, 0)),
                  pl.BlockSpec((None, d, tf), lambda i, f: (layer, 0, f)),
                  pl.BlockSpec((None, d, tf), lambda i, f: (layer, 0, f)),
                  pl.BlockSpec((None, tf, d), lambda i, f: (layer, f, 0)),
                  pl.BlockSpec((1, d), lambda i, f: (0, 0))],
        out_specs=[pl.BlockSpec((tm, d), lambda i, f: (i, 0)),
                   pl.BlockSpec((ns, d), lambda i, f: (0, 0))],
        scratch_shapes=[pltpu.VMEM((tm, d), BF16), pltpu.VMEM((ns, d), BF16)],
        compiler_params=_params("arbitrary", "arbitrary"),
        name="swiglu",
    )(x, xs, g, wg, wu, wd, g_final)


def _suffix_rows_exclusive(x):
    n = x.shape[0]
    row = lax.broadcasted_iota(jnp.int32, x.shape, 0)

    def up(a, k):
        return jnp.where(row + k < n, pltpu.roll(a, n - k, 0), 0.0)

    y = up(x, 1)
    k = 1
    while k < n:
        y = y + up(y, k)
        k *= 2
    return y


def _fox_decode_kernel(pt_ref, q_ref, kn_ref, vn_ref, cq_ref, *rest, heads, group, scale):
    del pt_ref
    k_refs, v_refs, f_refs = rest[:group], rest[group:2 * group], rest[2 * group:3 * group]
    u_same_ref, u_later_ref, o_ref, m_ref, l_ref, acc_ref, tail_ref = rest[3 * group:]
    step = pl.program_id(1)

    @pl.when(step == 0)
    def _():
        m_ref[...] = jnp.full_like(m_ref, -jnp.inf)
        l_ref[...] = jnp.zeros_like(l_ref)
        acc_ref[...] = jnp.zeros_like(acc_ref)
        tail_ref[...] = jnp.zeros_like(tail_ref)

    n_blk = k_refs[0].shape[0] // LANES
    wide = MXU_COLS // LANES
    q = q_ref[...]
    qb = q.astype(BF16)
    sub = lax.broadcasted_iota(jnp.int32, (heads, MXU_COLS), 0)
    lane = lax.broadcasted_iota(jnp.int32, (heads, MXU_COLS), 1)
    own = sub == lane % heads
    parts = _split3(jnp.concatenate([f[...] for f in f_refs], axis=0))
    rows_later = sum(_dot(x, u_later_ref[...]) for x in parts)
    rows_total = sum(_dot(x, u_same_ref[...]) for x in parts)
    tail = tail_ref[...]
    tiles = []
    for g in range(group):
        row_total = rows_total[g * n_blk:(g + 1) * n_blk]
        bias = cq_ref[...] + tail + rows_later[g * n_blk:(g + 1) * n_blk] + _suffix_rows_exclusive(row_total)
        tail = tail + jnp.sum(row_total, axis=0, keepdims=True)
        for r in range(0, n_blk, wide):
            kb = k_refs[g][r * LANES:(r + wide) * LANES, :].astype(BF16)
            brow = jnp.concatenate([bias[r + w:r + w + 1, :] for w in range(wide)], axis=1)
            s = jnp.where(own, _dot_nt(qb, kb) * scale + brow, -jnp.inf)
            tiles.extend(s[:, w * LANES:(w + 1) * LANES] for w in range(wide))
    tail_ref[...] = tail

    top = tiles[0]
    for t in tiles[1:]:
        top = jnp.maximum(top, t)
    m_old = m_ref[...]
    m_new = jnp.maximum(m_old, jnp.max(top, axis=-1, keepdims=True))
    a = jnp.exp(m_old - m_new)
    accs = [a * acc_ref[...], jnp.zeros(acc_ref.shape, F32)]
    psum = jnp.zeros((heads, LANES), F32)
    dh = acc_ref.shape[1]
    for n, idx in enumerate(range(0, len(tiles), wide)):
        g, r = divmod(idx, n_blk)
        prs = [jnp.exp(tiles[idx + w] - m_new) for w in range(wide)]
        for pr in prs:
            psum = psum + pr
        vb = jnp.concatenate([v_refs[g][(r + w) * LANES:(r + w + 1) * LANES, :].astype(BF16) for w in range(wide)],
                             axis=1)
        res = _dot(jnp.concatenate(prs, axis=0).astype(BF16), vb)
        for w in range(wide):
            accs[n % 2] = accs[n % 2] + res[w * heads:(w + 1) * heads, w * dh:(w + 1) * dh]
    m_ref[...] = m_new
    l_ref[...] = a * l_ref[...] + jnp.sum(psum, axis=-1, keepdims=True)
    acc_ref[...] = accs[0] + accs[1]

    @pl.when(step == pl.num_programs(1) - 1)
    def _():
        s_new = jnp.sum(q * kn_ref[...], axis=-1, keepdims=True) * scale
        m_fin = jnp.maximum(m_ref[...], s_new)
        a_fin = jnp.exp(m_ref[...] - m_fin)
        p_new = jnp.exp(s_new - m_fin)
        o_ref[...] = ((a_fin * acc_ref[...] + p_new * vn_ref[...]) / (a_fin * l_ref[...] + p_new)).astype(o_ref.dtype)


def _fox_decode(q, k_new, v_new, cq, cache_k, cache_v, cache_f, page_table, *, layer, group):
    bsz, heads, dh = q.shape
    rows = cache_k.shape[2]
    n_pages = page_table.shape[1]
    lane = np.arange(LANES)
    same = lane[:, None] % heads == lane[None, :] % heads
    u_same = jnp.asarray(same, BF16)
    u_later = jnp.asarray(same & (lane[:, None] > lane[None, :]), BF16)
    vec = pl.BlockSpec((None, heads, dh), lambda b, s, pt: (b, 0, 0))

    def paged(shape, g):
        return pl.BlockSpec((None, None) + shape, lambda b, s, pt: (layer, pt[b, n_pages - 1 - (s * group + g)], 0, 0))

    const = pl.BlockSpec((LANES, LANES), lambda b, s, pt: (0, 0))
    grid_spec = pltpu.PrefetchScalarGridSpec(
        num_scalar_prefetch=1,
        grid=(bsz, n_pages // group),
        in_specs=([vec, vec, vec, pl.BlockSpec((None, 1, LANES), lambda b, s, pt: (b, 0, 0))]
                  + [paged((rows, dh), g) for g in range(group)]
                  + [paged((rows, dh), g) for g in range(group)]
                  + [paged((rows // LANES, LANES), g) for g in range(group)]
                  + [const, const]),
        out_specs=vec,
        scratch_shapes=[pltpu.VMEM((heads, 1), F32), pltpu.VMEM((heads, 1), F32),
                        pltpu.VMEM((heads, dh), F32), pltpu.VMEM((1, LANES), F32)],
    )
    return pl.pallas_call(
        functools.partial(_fox_decode_kernel, heads=heads, group=group, scale=dh ** -0.5),
        out_shape=jax.ShapeDtypeStruct((bsz, heads, dh), BF16),
        grid_spec=grid_spec,
        compiler_params=_params("arbitrary", "arbitrary"),
        name="fox_decode",
    )(page_table, q, k_new, v_new, cq, *([cache_k] * group), *([cache_v] * group), *([cache_f] * group),
      u_same, u_later)


def kernel(x_prompt, x_sample, mem_prompt, cache_fox_k, cache_fox_v, cache_fox_logf, state_gla, cache_mem_k, cache_mem_v, page_table, g_mix, w_in, b_forget, w_gla_gate, b_gla_gate, g_gla_head, w_out, g_cross, g_mem, w_xq, w_xk, w_xv, w_xo, g_ffn, w_ffn_gate, w_ffn_up, w_ffn_down, g_final):
    depth = w_in.shape[0]
    bsz, seq, d = x_prompt.shape
    sb = x_sample.shape[0]
    mem = mem_prompt.shape[1]
    _, pool, page, fh, fdh = cache_fox_k.shape
    _, _, gh, gdk, gdv = state_gla.shape
    xh, xdh = cache_mem_k.shape[3], cache_mem_k.shape[4]
    fox_dim, gk_dim, gv_dim = fh * fdh, gh * gdk, gh * gdv
    rank = w_gla_gate.shape[1]
    n = bsz * seq
    c_forget = 3 * fox_dim
    c_gla = c_forget + fh
    c_rank = c_gla + 2 * gk_dim + 2 * gv_dim

    xp = x_prompt.reshape(n, d)
    xs = x_sample.reshape(sb, d)
    memx = mem_prompt.reshape(bsz * mem, d)
    ck = cache_fox_k.reshape(depth, pool, page * fh, fdh)
    cv = cache_fox_v.reshape(depth, pool, page * fh, fdh)
    cf = cache_fox_logf.reshape(depth, pool, page * fh // LANES, LANES)
    row = lambda v: v.reshape(1, -1)
    zeros_state = jnp.zeros((bsz, gh, gdk, gdv), F32)
    group = min(DECODE_PAGES_PER_STEP, page_table.shape[1])

    w_main = jnp.concatenate([w_in[:, :, :c_forget], w_in[:, :, c_gla:c_rank]], axis=2).astype(BF16)
    w_small = jnp.concatenate([w_in[:, :, c_forget:c_gla], w_in[:, :, c_rank:],
                               jnp.zeros((depth, d, LANES - fh - rank), F32)], axis=2).astype(BF16)
    w_g2 = jnp.concatenate([jnp.zeros((depth, fh, gk_dim), F32), w_gla_gate,
                            jnp.zeros((depth, LANES - fh - rank, gk_dim), F32)], axis=1).astype(BF16)
    b_f = jnp.concatenate([b_forget, jnp.zeros((depth, LANES - fh), F32)], axis=1)
    w_kv = jnp.concatenate([w_xk, w_xv], axis=2)

    pk, pv, pf, ps, pmk, pmv, sk, sv, sf, ss = ([] for _ in range(10))
    for l in range(depth):
        last = l == depth - 1

        z, logf, ga, k3, v3, drow = _in_proj(xp, row(g_mix[l]), w_main, w_small, b_f[l:l + 1], w_g2,
                                             row(b_gla_gate[l]), layer=l, n_fox_heads=fh, fox_dh=fdh, seq=seq,
                                             tm=min(ROW_TILE_PROJ, seq), tn=COL_TILE)
        o_f = _fox_prompt(z, drow, batch=bsz, seq=seq, heads=fh, dh=fdh, tq=min(FOX_Q_TILE, seq),
                          hp=FOX_HEADS_PER_STEP)
        o_g, s_fin = _gla(z, ga, zeros_state, row(g_gla_head[l]), batch=bsz, seq=seq, chunk=min(GLA_CHUNK, seq),
                          heads=gh, dk=gdk, dv=gdv, col0=3 * fox_dim)
        xp = _out_proj(o_f, o_g, w_out, xp, layer=l, tm=min(ROW_TILE_PROJ, n), tn=COL_TILE)
        mkv = _norm_matmul(memx, row(g_mem[l]), w_kv, layer=l, tm=bsz * mem, tn=COL_TILE)
        mk, mv = mkv[:, :xh * xdh], mkv[:, xh * xdh:]
        xp = _cross_attn(xp.reshape(bsz, seq, d), row(g_cross[l]), w_xq, mk.reshape(bsz, mem, -1),
                         mv.reshape(bsz, mem, -1), w_xo, layer=l, heads=xh, dh=xdh,
                         tm=min(ROW_TILE_XATTN, seq)).reshape(n, d)
        pk.append(k3.reshape(bsz, seq, fh, fdh))
        pv.append(v3.reshape(bsz, seq, fh, fdh))
        pf.append(logf.reshape(bsz, seq, fh))
        ps.append(s_fin)
        pmk.append(mk.reshape(bsz, mem, xh, xdh))
        pmv.append(mv.reshape(bsz, mem, xh, xdh))

        zs, logf_s, ga_s, fk, fv = _in_proj(xs, row(g_mix[l]), w_main, w_small, b_f[l:l + 1], w_g2,
                                            row(b_gla_gate[l]), layer=l, n_fox_heads=fh, fox_dh=fdh, seq=0,
                                            tm=sb, tn=COL_TILE)
        fq, fk, fv = zs[:, :fox_dim].reshape(sb, fh, fdh), fk.reshape(sb, fh, fdh), fv.reshape(sb, fh, fdh)
        cq = jnp.tile(logf_s, (1, LANES // fh)).reshape(sb, 1, LANES)
        o_fs = _fox_decode(fq, fk, fv, cq, ck, cv, cf, page_table, layer=l, group=group).reshape(sb, fox_dim)
        cs = GLA_CHUNK_SAMPLE
        pad = lambda a: jnp.zeros((sb, cs, a.shape[1]), a.dtype).at[:, 0].set(a).reshape(sb * cs, -1)
        o_gs, s_new = _gla(pad(zs), pad(ga_s), state_gla[l], row(g_gla_head[l]), batch=sb, seq=cs, chunk=cs,
                           heads=gh, dk=gdk, dv=gdv, col0=3 * fox_dim)
        o_gs = o_gs.reshape(sb, cs, gv_dim)[:, 0]
        xs = _out_proj(o_fs, o_gs, w_out, xs, layer=l, tm=sb, tn=COL_TILE)
        xs = _cross_attn(xs.reshape(sb, 1, d), row(g_cross[l]), w_xq, cache_mem_k[l].reshape(sb, mem, -1),
                         cache_mem_v[l].reshape(sb, mem, -1), w_xo, layer=l, heads=xh, dh=xdh, tm=1).reshape(sb, d)
        xp, xs = _swiglu(xp, xs, row(g_ffn[l]), w_ffn_gate, w_ffn_up, w_ffn_down, row(g_final), layer=l,
                         final_norm=last, tm=min(ROW_TILE_FFN, n), tf=FFN_TILE)
        sk.append(fk.reshape(sb, 1, fh, fdh))
        sv.append(fv.reshape(sb, 1, fh, fdh))
        sf.append(logf_s.reshape(sb, 1, fh))
        ss.append(s_new)

    return (xp.reshape(bsz, seq, d), xs.reshape(sb, 1, d),
            jnp.stack(pk), jnp.stack(pv), jnp.stack(pf), jnp.stack(ps), jnp.stack(pmk), jnp.stack(pmv),
            jnp.stack(sk), jnp.stack(sv), jnp.stack(sf), jnp.stack(ss))
```

```python
import functools

import numpy as np
import jax
import jax.numpy as jnp
from jax import lax
from jax.experimental import pallas as pl
from jax.experimental.pallas import tpu as pltpu

F32 = jnp.float32
BF16 = jnp.bfloat16

RMS_EPS = 1e-6
LOG2E = 1.4426950408889634
GLA_TAU = 16.0
GLA_CHUNK = 128
GLA_CHUNK_SAMPLE = 64
LANES = 128
SUBLANES = 8
VMEM_LIMIT_BYTES = 56 * 1024 * 1024
ROW_TILE_PROJ = 1024
ROW_TILE_XATTN = 512
ROW_TILE_FFN = 1024
FFN_TILE = 256
COL_TILE = 512
FOX_Q_TILE = 512
FOX_HEADS_PER_STEP = 2
DECODE_PAGES_PER_STEP = 8
MXU_COLS = 256
W_PREP_ROWS = 256
ROW_TILE_OUT = 512


def _params(*sem):
    return pltpu.CompilerParams(dimension_semantics=sem, vmem_limit_bytes=VMEM_LIMIT_BYTES)


def _rms(x, g):
    return x * lax.rsqrt(jnp.mean(x * x, axis=-1, keepdims=True) + RMS_EPS) * g


def _log_sigmoid(x):
    return jnp.minimum(x, 0.0) - jnp.log1p(jnp.exp(-jnp.abs(x)))


def _silu(x):
    return x / (1.0 + jnp.exp(-x))


def _dot(a, b):
    return jnp.dot(a, b, preferred_element_type=F32)


def _dot_nt(a, b):
    return lax.dot_general(a, b, (((1,), (1,)), ((), ())), preferred_element_type=F32)


def _dot_tn(a, b):
    return lax.dot_general(a, b, (((0,), (0,)), ((), ())), preferred_element_type=F32)


def _split3(x):
    a = x.astype(BF16)
    r = x - a.astype(F32)
    b = r.astype(BF16)
    c = (r - b.astype(F32)).astype(BF16)
    return a, b, c


def _dot3(m, x, dot=_dot):
    a, b, c = _split3(x)
    return dot(m, a) + dot(m, b) + dot(m, c)


def _w_main_kernel(w_ref, o_ref, *, c_a, c_b0, c_b1):
    o_ref[:, :c_a] = w_ref[:, :c_a].astype(BF16)
    o_ref[:, c_a:] = w_ref[:, c_b0:c_b1].astype(BF16)


def _w_main(w_in, *, c_a, c_b0, c_b1, tr):
    depth, d, n_in = w_in.shape
    width = c_a + c_b1 - c_b0
    return pl.pallas_call(
        functools.partial(_w_main_kernel, c_a=c_a, c_b0=c_b0, c_b1=c_b1),
        out_shape=jax.ShapeDtypeStruct((depth, d, width), BF16),
        grid=(depth, d // tr),
        in_specs=[pl.BlockSpec((None, tr, n_in), lambda l, r: (l, r, 0))],
        out_specs=pl.BlockSpec((None, tr, width), lambda l, r: (l, r, 0)),
        compiler_params=_params("arbitrary", "arbitrary"),
        name="w_main",
    )(w_in)


def _in_proj_kernel(x_ref, g_ref, w_ref, ws_ref, bf_ref, wg2_ref, bg_ref, *rest, n_fox_heads, tiles_per_seq):
    if tiles_per_seq:
        z_ref, logf_ref, ga_ref, k3_ref, v3_ref, drow_ref, hb_ref, carry_ref = rest
    else:
        z_ref, logf_ref, ga_ref, k3_ref, v3_ref, hb_ref = rest
    i = pl.program_id(0)
    j = pl.program_id(1)

    @pl.when(j == 0)
    def _():
        hb = _rms(x_ref[...], g_ref[...]).astype(BF16)
        hb_ref[...] = hb
        zs = _dot(hb, ws_ref[...])
        lf = _log_sigmoid(zs + bf_ref[...])
        logf_ref[...] = lf[:, :n_fox_heads]
        gp = _dot(zs.astype(BF16), wg2_ref[...]) + bg_ref[...]
        ga_ref[...] = _log_sigmoid(gp) * (1.0 / GLA_TAU)
        if tiles_per_seq:
            tm = lf.shape[0]
            r = lax.broadcasted_iota(jnp.int32, (LANES, LANES), 0)
            c = lax.broadcasted_iota(jnp.int32, (LANES, LANES), 1)
            tri = jnp.where(r <= c, 1.0, 0.0).astype(BF16)
            lft = lf.T[:SUBLANES]
            run = jnp.where(i % tiles_per_seq == 0, 0.0, carry_ref[...])
            for c0 in range(0, tm, LANES):
                d = _dot3(tri, lft[:, c0:c0 + LANES], dot=lambda m, x: _dot(x, m)) + run
                drow_ref[:, c0:c0 + LANES] = d
                run = d[:, LANES - 1:LANES]
            carry_ref[...] = run

    zt = _dot(hb_ref[...], w_ref[...])
    z_ref[...] = zt
    tm, tn = zt.shape
    dh = k3_ref.shape[1]
    heads_per_tile = tn // dh
    tiles_per_group = n_fox_heads // heads_per_tile
    for group, out_ref in ((1, k3_ref), (2, v3_ref)):
        for t in range(tiles_per_group):
            @pl.when(j == group * tiles_per_group + t)
            def _(out_ref=out_ref, t=t):
                for c in range(heads_per_tile):
                    out_ref[pl.ds(t * heads_per_tile + c, tm, stride=n_fox_heads), :] = zt[:, c * dh:(c + 1) * dh]


def _in_proj(x, g, w_main, w_small, b_f, w_g2, b_g, *, layer, n_fox_heads, fox_dh, seq, tm, tn):
    n, d = x.shape
    nz = w_main.shape[2]
    kg = w_g2.shape[2]
    tiles_per_seq = seq // tm if seq else 0
    out_shape = [jax.ShapeDtypeStruct((n, nz), F32),
                 jax.ShapeDtypeStruct((n, n_fox_heads), F32),
                 jax.ShapeDtypeStruct((n, kg), F32),
                 jax.ShapeDtypeStruct((n * n_fox_heads, fox_dh), F32),
                 jax.ShapeDtypeStruct((n * n_fox_heads, fox_dh), F32)]
    out_specs = [pl.BlockSpec((tm, tn), lambda i, j: (i, j)),
                 pl.BlockSpec((tm, n_fox_heads), lambda i, j: (i, 0)),
                 pl.BlockSpec((tm, kg), lambda i, j: (i, 0)),
                 pl.BlockSpec((tm * n_fox_heads, fox_dh), lambda i, j: (i, 0)),
                 pl.BlockSpec((tm * n_fox_heads, fox_dh), lambda i, j: (i, 0))]
    scratch = [pltpu.VMEM((tm, d), BF16)]
    if tiles_per_seq:
        out_shape.append(jax.ShapeDtypeStruct((n // seq, SUBLANES, seq), F32))
        out_specs.append(pl.BlockSpec((None, SUBLANES, tm), lambda i, j: (i // tiles_per_seq, 0, i % tiles_per_seq)))
        scratch.append(pltpu.VMEM((SUBLANES, 1), F32))
    return pl.pallas_call(
        functools.partial(_in_proj_kernel, n_fox_heads=n_fox_heads, tiles_per_seq=tiles_per_seq),
        out_shape=out_shape,
        grid=(n // tm, nz // tn),
        in_specs=[pl.BlockSpec((tm, d), lambda i, j: (i, 0)),
                  pl.BlockSpec((1, d), lambda i, j: (0, 0)),
                  pl.BlockSpec((None, d, tn), lambda i, j: (layer, 0, j)),
                  pl.BlockSpec((None, d, LANES), lambda i, j: (layer, 0, 0)),
                  pl.BlockSpec((1, LANES), lambda i, j: (0, 0)),
                  pl.BlockSpec((None, LANES, kg), lambda i, j: (layer, 0, 0)),
                  pl.BlockSpec((1, kg), lambda i, j: (0, 0))],
        out_specs=out_specs,
        scratch_shapes=scratch,
        compiler_params=_params("arbitrary", "arbitrary"),
        name="in_proj",
    )(x, g, w_main, w_small, b_f, w_g2, b_g)


def _norm_matmul_kernel(x_ref, g_ref, w_ref, o_ref, hb_ref):
    @pl.when(pl.program_id(1) == 0)
    def _():
        hb_ref[...] = _rms(x_ref[...], g_ref[...]).astype(BF16)

    o_ref[...] = _dot(hb_ref[...], w_ref[...].astype(BF16))


def _norm_matmul(x, g, w, *, layer, tm, tn):
    n, d = x.shape
    nout = w.shape[2]
    return pl.pallas_call(
        _norm_matmul_kernel,
        out_shape=jax.ShapeDtypeStruct((n, nout), F32),
        grid=(n // tm, nout // tn),
        in_specs=[pl.BlockSpec((tm, d), lambda i, j: (i, 0)),
                  pl.BlockSpec((1, d), lambda i, j: (0, 0)),
                  pl.BlockSpec((None, d, tn), lambda i, j: (layer, 0, j))],
        out_specs=pl.BlockSpec((tm, tn), lambda i, j: (i, j)),
        scratch_shapes=[pltpu.VMEM((tm, d), BF16)],
        compiler_params=_params("arbitrary", "arbitrary"),
        name="norm_matmul",
    )(x, g, w)


def _fox_prompt_kernel(q_ref, k_ref, v_ref, d_ref, o_ref, kb_ref, vb_ref, *, tq, dh, scale):
    hg = pl.program_id(1)
    qi = pl.program_id(2)
    hp = q_ref.shape[1] // dh

    @pl.when(qi == 0)
    def _():
        kb_ref[...] = k_ref[...].astype(BF16)
        vb_ref[...] = v_ref[...].astype(BF16)

    qs = [(q_ref[:, i * dh:(i + 1) * dh] * (scale * LOG2E)).astype(BF16) for i in range(hp)]

    def block(j, carry, masked):
        start = pl.multiple_of(j * tq, tq)
        out = []
        for i in range(hp):
            m, l, acc = carry[i]
            kj = kb_ref[pl.ds(start, tq), i * dh:(i + 1) * dh]
            vj = vb_ref[pl.ds(start, tq), i * dh:(i + 1) * dh]
            s = _dot_nt(qs[i], kj) - d_ref[pl.ds(hg * hp + i, 1), pl.ds(start, tq)] * LOG2E
            if masked:
                r = lax.broadcasted_iota(jnp.int32, (tq, tq), 0)
                c = lax.broadcasted_iota(jnp.int32, (tq, tq), 1)
                s = jnp.where(c <= r, s, -jnp.inf)
            m_new = jnp.maximum(m, jnp.max(s, axis=-1, keepdims=True))
            a = jnp.exp2(m - m_new)
            p = jnp.exp2(s - m_new)
            l = a * l + jnp.sum(p, axis=-1, keepdims=True)
            acc = a * acc + _dot(p.astype(BF16), vj)
            out.append((m_new, l, acc))
        return tuple(out)

    init = tuple((jnp.full((tq, 1), -jnp.inf, F32), jnp.zeros((tq, 1), F32), jnp.zeros((tq, dh), F32))
                 for _ in range(hp))
    carry = lax.fori_loop(0, qi, lambda j, c: block(j, c, False), init)
    final = block(qi, carry, True)
    for i, (_, l, acc) in enumerate(final):
        o_ref[:, i * dh:(i + 1) * dh] = (acc / l).astype(o_ref.dtype)


def _fox_prompt(z, drow, *, batch, seq, heads, dh, tq, hp):
    n = z.shape[0]
    nq = seq // tq
    hg = heads // hp
    w = hp * dh
    return pl.pallas_call(
        functools.partial(_fox_prompt_kernel, tq=tq, dh=dh, scale=dh ** -0.5),
        out_shape=jax.ShapeDtypeStruct((n, heads * dh), BF16),
        grid=(batch, hg, nq),
        in_specs=[pl.BlockSpec((tq, w), lambda b, h, i: (b * nq + i, h)),
                  pl.BlockSpec((seq, w), lambda b, h, i: (b, hg + h)),
                  pl.BlockSpec((seq, w), lambda b, h, i: (b, 2 * hg + h)),
                  pl.BlockSpec((None, SUBLANES, seq), lambda b, h, i: (b, 0, 0))],
        out_specs=pl.BlockSpec((tq, w), lambda b, h, i: (b * nq + i, h)),
        scratch_shapes=[pltpu.VMEM((seq, w), BF16), pltpu.VMEM((seq, w), BF16)],
        compiler_params=_params("arbitrary", "arbitrary", "arbitrary"),
        name="fox_prompt",
    )(z, z, z, drow)


def _gla_tables(c):
    r = np.arange(c)
    tri = (r[None, :] <= r[:, None]).astype(np.float32)
    masks = [np.eye(c, dtype=np.float32)]
    m = c // 2
    while m >= 1:
        same = (r[:, None] // (2 * m)) == (r[None, :] // (2 * m))
        upper = (r[:, None] % (2 * m)) >= m
        lower = (r[None, :] % (2 * m)) < m
        masks.append((same & upper & lower).astype(np.float32))
        m //= 2
    return tri, np.concatenate(masks, 0)


def _gla_level_refs(b, b_ref, col0):
    c, dk = b.shape
    bcast = lambda i, rows: jnp.broadcast_to(b_ref[pl.ds(i, 1), pl.ds(col0, dk)], (rows, dk))
    sub = lax.broadcasted_iota(jnp.int32, (c, dk), 0) % SUBLANES
    out = []
    m = c // 2
    while m >= 1:
        if 2 * m >= SUBLANES:
            out.append(jnp.concatenate([bcast(blk * 2 * m + m - 1, 2 * m) for blk in range(c // (2 * m))], axis=0))
        elif m > 1:
            ref = None
            for t in range(SUBLANES // (2 * m)):
                rows = jnp.concatenate([bcast(g * SUBLANES + t * 2 * m + m - 1, SUBLANES)
                                        for g in range(c // SUBLANES)], axis=0)
                ref = rows if ref is None else jnp.where(sub // (2 * m) == t, rows, ref)
            out.append(ref)
        else:
            out.append(jnp.where(sub % 2 == 0, b, pltpu.roll(b, 1, 0)))
        m //= 2
    return out


def _gla_kernel(q_ref, k_ref, v_ref, r_ref, ga_ref, s0_ref, tri_ref, masks_ref, gh_ref,
                og_ref, sfin_ref, st_ref, b_ref, *, heads, dk, dv, scale):
    ci = pl.program_id(1)

    @pl.when(ci == 0)
    def _():
        for h in range(heads):
            st_ref[h] = s0_ref[h].T

    c = q_ref.shape[0]
    b_all = _dot3(tri_ref[...], ga_ref[...])
    b_ref[...] = b_all
    for h in range(heads):
        q = q_ref[:, h * dk:(h + 1) * dk] * scale
        k = k_ref[:, h * dk:(h + 1) * dk]
        v = v_ref[:, h * dv:(h + 1) * dv]
        b = b_all[:, h * dk:(h + 1) * dk]
        a = jnp.where(masks_ref[0:c, :] != 0, jnp.sum(q * k, axis=-1, keepdims=True), 0.0)
        for lv, ref in enumerate(_gla_level_refs(b, b_ref, h * dk)):
            d = b - ref
            qt = (q * jnp.exp(jnp.minimum(d, 0.0))).astype(BF16)
            kt = (k * jnp.exp(jnp.minimum(-d, 0.0))).astype(BF16)
            a = a + jnp.where(masks_ref[(lv + 1) * c:(lv + 2) * c, :] != 0, _dot_nt(qt, kt), 0.0)
        s_old = st_ref[h]
        o = _dot(a.astype(BF16), v.astype(BF16)) + _dot_nt((q * jnp.exp(b)).astype(BF16), s_old.astype(BF16))
        b_last = b[c - 1:c, :]
        upd = _dot(v.T.astype(BF16), (k * jnp.exp(b_last - b)).astype(BF16))
        st_ref[h] = s_old * jnp.exp(b_last) + upd
        y = _rms(o, gh_ref[...])
        og_ref[:, h * dv:(h + 1) * dv] = (y * _silu(r_ref[:, h * dv:(h + 1) * dv])).astype(og_ref.dtype)

    @pl.when(ci == pl.num_programs(1) - 1)
    def _():
        for h in range(heads):
            sfin_ref[h] = st_ref[h].T


def _gla(z, ga, s0, g_head, *, batch, seq, chunk, heads, dk, dv, col0):
    n = z.shape[0]
    c = chunk
    nc = seq // c
    wk, wv = heads * dk, heads * dv
    tri, masks = _gla_tables(c)
    row = lambda b, i: b * nc + i
    return pl.pallas_call(
        functools.partial(_gla_kernel, heads=heads, dk=dk, dv=dv, scale=dk ** -0.5),
        out_shape=[jax.ShapeDtypeStruct((n, wv), BF16), jax.ShapeDtypeStruct((batch, heads, dk, dv), F32)],
        grid=(batch, nc),
        in_specs=[pl.BlockSpec((c, wk), lambda b, i: (row(b, i), col0 // wk)),
                  pl.BlockSpec((c, wk), lambda b, i: (row(b, i), col0 // wk + 1)),
                  pl.BlockSpec((c, wv), lambda b, i: (row(b, i), (col0 + 2 * wk) // wv)),
                  pl.BlockSpec((c, wv), lambda b, i: (row(b, i), (col0 + 2 * wk) // wv + 1)),
                  pl.BlockSpec((c, wk), lambda b, i: (row(b, i), 0)),
                  pl.BlockSpec((None, heads, dk, dv), lambda b, i: (b, 0, 0, 0)),
                  pl.BlockSpec(tri.shape, lambda b, i: (0, 0)),
                  pl.BlockSpec(masks.shape, lambda b, i: (0, 0)),
                  pl.BlockSpec((1, dv), lambda b, i: (0, 0))],
        out_specs=[pl.BlockSpec((c, wv), lambda b, i: (row(b, i), 0)),
                   pl.BlockSpec((None, heads, dk, dv), lambda b, i: (b, 0, 0, 0))],
        scratch_shapes=[pltpu.VMEM((heads, dv, dk), F32), pltpu.VMEM((c, wk), F32)],
        compiler_params=_params("arbitrary", "arbitrary"),
        name="gla",
    )(z, z, z, z, ga, s0, jnp.asarray(tri, BF16), jnp.asarray(masks, F32), g_head)


def _out_proj_kernel(a_ref, b_ref, w1_ref, w2_ref, x_ref, o_ref):
    o_ref[...] = x_ref[...] + _dot(a_ref[...], w1_ref[...]) + _dot(b_ref[...], w2_ref[...])


def _out_proj(a, b, w, x, *, layer, tm):
    n, d = x.shape
    kh = a.shape[1]
    return pl.pallas_call(
        _out_proj_kernel,
        out_shape=jax.ShapeDtypeStruct((n, d), F32),
        grid=(n // tm,),
        in_specs=[pl.BlockSpec((tm, kh), lambda i: (i, 0)),
                  pl.BlockSpec((tm, kh), lambda i: (i, 0)),
                  pl.BlockSpec((None, kh, d), lambda i: (layer, 0, 0)),
                  pl.BlockSpec((None, kh, d), lambda i: (layer, 1, 0)),
                  pl.BlockSpec((tm, d), lambda i: (i, 0))],
        out_specs=pl.BlockSpec((tm, d), lambda i: (i, 0)),
        compiler_params=_params("arbitrary"),
        name="out_proj",
    )(a, b, w, w, x)


def _cross_attn_kernel(x_ref, g_ref, wq_ref, mk_ref, mv_ref, wo_ref, o_ref, *, heads, dh):
    x = x_ref[...]
    tm, d = x.shape
    xr = x if tm >= SUBLANES else jnp.broadcast_to(x, (SUBLANES, d))
    q = _dot(_rms(xr, g_ref[...]).astype(BF16), wq_ref[...].astype(BF16)) * (dh ** -0.5)
    outs = []
    for h in range(heads):
        sl = slice(h * dh, (h + 1) * dh)
        s = _dot_nt(q[:, sl].astype(BF16), mk_ref[:, sl].astype(BF16))
        p = jnp.exp(s - jnp.max(s, axis=-1, keepdims=True))
        p = p / jnp.sum(p, axis=-1, keepdims=True)
        outs.append(_dot(p.astype(BF16), mv_ref[:, sl].astype(BF16)))
    y = _dot(jnp.concatenate(outs, axis=1).astype(BF16), wo_ref[...].astype(BF16))
    o_ref[...] = x + y[:tm]


def _cross_attn(x3, g, wq, mk, mv, wo, *, layer, heads, dh, tm):
    bsz, rows, d = x3.shape
    mem = mk.shape[1]
    xd = heads * dh
    return pl.pallas_call(
        functools.partial(_cross_attn_kernel, heads=heads, dh=dh),
        out_shape=jax.ShapeDtypeStruct(x3.shape, F32),
        grid=(bsz, rows // tm),
        in_specs=[pl.BlockSpec((None, tm, d), lambda b, i: (b, i, 0)),
                  pl.BlockSpec((1, d), lambda b, i: (0, 0)),
                  pl.BlockSpec((None, d, xd), lambda b, i: (layer, 0, 0)),
                  pl.BlockSpec((None, mem, xd), lambda b, i: (b, 0, 0)),
                  pl.BlockSpec((None, mem, xd), lambda b, i: (b, 0, 0)),
                  pl.BlockSpec((None, xd, d), lambda b, i: (layer, 0, 0))],
        out_specs=pl.BlockSpec((None, tm, d), lambda b, i: (b, i, 0)),
        compiler_params=_params("arbitrary", "arbitrary"),
        name="cross_attn",
    )(x3, g, wq, mk, mv, wo)


def _swiglu_kernel(x_ref, xs_ref, g_ref, wg_ref, wu_ref, wd_ref, gf_ref, o_ref, os_ref, hb_ref, hs_ref, *, final_norm):
    i = pl.program_id(0)
    f = pl.program_id(1)
    last = pl.num_programs(1) - 1

    @pl.when(f == 0)
    def _():
        x = x_ref[...]
        hb_ref[...] = _rms(x, g_ref[...]).astype(BF16)
        o_ref[...] = x

    wg, wu, wd = wg_ref[...].astype(BF16), wu_ref[...].astype(BF16), wd_ref[...].astype(BF16)

    def ffn(hb):
        return _dot((_silu(_dot(hb, wg)) * _dot(hb, wu)).astype(BF16), wd)

    o_ref[...] += ffn(hb_ref[...])

    @pl.when(f == last)
    def _():
        if final_norm:
            o_ref[...] = _rms(o_ref[...], gf_ref[...])

    @pl.when(i == 0)
    def _():
        @pl.when(f == 0)
        def _():
            xs = xs_ref[...]
            hs_ref[...] = _rms(xs, g_ref[...]).astype(BF16)
            os_ref[...] = xs

        os_ref[...] += ffn(hs_ref[...])

        @pl.when(f == last)
        def _():
            if final_norm:
                os_ref[...] = _rms(os_ref[...], gf_ref[...])


def _swiglu(x, xs, g, wg, wu, wd, g_final, *, layer, final_norm, tm, tf):
    n, d = x.shape
    ns = xs.shape[0]
    ffn = wg.shape[2]
    return pl.pallas_call(
        functools.partial(_swiglu_kernel, final_norm=final_norm),
        out_shape=[jax.ShapeDtypeStruct((n, d), F32), jax.ShapeDtypeStruct((ns, d), F32)],
        grid=(n // tm, ffn // tf),
        in_specs=[pl.BlockSpec((tm, d), lambda i, f: (i, 0)),
                  pl.BlockSpec((ns, d), lambda i, f: (0, 0)),
                  pl.BlockSpec((1, d), lambda i, f: (0, 0)),
                  pl.BlockSpec((None, d, tf), lambda i, f: (layer, 0, f)),
                  pl.BlockSpec((None, d, tf), lambda i, f: (layer, 0, f)),
                  pl.BlockSpec((None, tf, d), lambda i, f: (layer, f, 0)),
                  pl.BlockSpec((1, d), lambda i, f: (0, 0))],
        out_specs=[pl.BlockSpec((tm, d), lambda i, f: (i, 0)),
                   pl.BlockSpec((ns, d), lambda i, f: (0, 0))],
        scratch_shapes=[pltpu.VMEM((tm, d), BF16), pltpu.VMEM((ns, d), BF16)],
        compiler_params=_params("arbitrary", "arbitrary"),
        name="swiglu",
    )(x, xs, g, wg, wu, wd, g_final)


def _suffix_rows_exclusive(x):
    n = x.shape[0]
    row = lax.broadcasted_iota(jnp.int32, x.shape, 0)

    def up(a, k):
        return jnp.where(row + k < n, pltpu.roll(a, n - k, 0), 0.0)

    y = up(x, 1)
    k = 1
    while k < n:
        y = y + up(y, k)
        k *= 2
    return y


def _fox_decode_kernel(pt_ref, q_ref, kn_ref, vn_ref, cq_ref, *rest, heads, group, scale):
    del pt_ref
    k_refs, v_refs, f_refs = rest[:group], rest[group:2 * group], rest[2 * group:3 * group]
    u_same_ref, u_later_ref, o_ref, m_ref, l_ref, acc_ref, tail_ref = rest[3 * group:]
    step = pl.program_id(1)

    @pl.when(step == 0)
    def _():
        m_ref[...] = jnp.full_like(m_ref, -jnp.inf)
        l_ref[...] = jnp.zeros_like(l_ref)
        acc_ref[...] = jnp.zeros_like(acc_ref)
        tail_ref[...] = jnp.zeros_like(tail_ref)

    n_blk = k_refs[0].shape[0] // LANES
    wide = MXU_COLS // LANES
    q = q_ref[...]
    qb = q.astype(BF16)
    sub = lax.broadcasted_iota(jnp.int32, (heads, MXU_COLS), 0)
    lane = lax.broadcasted_iota(jnp.int32, (heads, MXU_COLS), 1)
    own = sub == lane % heads
    parts = _split3(jnp.concatenate([f[...] for f in f_refs], axis=0))
    rows_later = sum(_dot(x, u_later_ref[...]) for x in parts)
    rows_total = sum(_dot(x, u_same_ref[...]) for x in parts)
    tail = tail_ref[...]
    tiles = []
    for g in range(group):
        row_total = rows_total[g * n_blk:(g + 1) * n_blk]
        bias = cq_ref[...] + tail + rows_later[g * n_blk:(g + 1) * n_blk] + _suffix_rows_exclusive(row_total)
        tail = tail + jnp.sum(row_total, axis=0, keepdims=True)
        for r in range(0, n_blk, wide):
            kb = k_refs[g][r * LANES:(r + wide) * LANES, :].astype(BF16)
            brow = jnp.concatenate([bias[r + w:r + w + 1, :] for w in range(wide)], axis=1)
            s = jnp.where(own, _dot_nt(qb, kb) * scale + brow, -jnp.inf)
            tiles.extend(s[:, w * LANES:(w + 1) * LANES] for w in range(wide))
    tail_ref[...] = tail

    top = tiles[0]
    for t in tiles[1:]:
        top = jnp.maximum(top, t)
    m_old = m_ref[...]
    m_new = jnp.maximum(m_old, jnp.max(top, axis=-1, keepdims=True))
    a = jnp.exp(m_old - m_new)
    accs = [a * acc_ref[...], jnp.zeros(acc_ref.shape, F32)]
    psum = jnp.zeros((heads, LANES), F32)
    dh = acc_ref.shape[1]
    for n, idx in enumerate(range(0, len(tiles), wide)):
        g, r = divmod(idx, n_blk)
        prs = [jnp.exp(tiles[idx + w] - m_new) for w in range(wide)]
        for pr in prs:
            psum = psum + pr
        vb = jnp.concatenate([v_refs[g][(r + w) * LANES:(r + w + 1) * LANES, :].astype(BF16) for w in range(wide)],
                             axis=1)
        res = _dot(jnp.concatenate(prs, axis=0).astype(BF16), vb)
        for w in range(wide):
            accs[n % 2] = accs[n % 2] + res[w * heads:(w + 1) * heads, w * dh:(w + 1) * dh]
    m_ref[...] = m_new
    l_ref[...] = a * l_ref[...] + jnp.sum(psum, axis=-1, keepdims=True)
    acc_ref[...] = accs[0] + accs[1]

    @pl.when(step == pl.num_programs(1) - 1)
    def _():
        s_new = jnp.sum(q * kn_ref[...], axis=-1, keepdims=True) * scale
        m_fin = jnp.maximum(m_ref[...], s_new)
        a_fin = jnp.exp(m_ref[...] - m_fin)
        p_new = jnp.exp(s_new - m_fin)
        o_ref[...] = ((a_fin * acc_ref[...] + p_new * vn_ref[...]) / (a_fin * l_ref[...] + p_new)).astype(o_ref.dtype)


def _fox_decode(q, k_new, v_new, cq, cache_k, cache_v, cache_f, page_table, *, layer, group):
    bsz, heads, dh = q.shape
    rows = cache_k.shape[2]
    n_pages = page_table.shape[1]
    lane = np.arange(LANES)
    same = lane[:, None] % heads == lane[None, :] % heads
    u_same = jnp.asarray(same, BF16)
    u_later = jnp.asarray(same & (lane[:, None] > lane[None, :]), BF16)
    vec = pl.BlockSpec((None, heads, dh), lambda b, s, pt: (b, 0, 0))

    def paged(shape, g):
        return pl.BlockSpec((None, None) + shape, lambda b, s, pt: (layer, pt[b, n_pages - 1 - (s * group + g)], 0, 0))

    const = pl.BlockSpec((LANES, LANES), lambda b, s, pt: (0, 0))
    grid_spec = pltpu.PrefetchScalarGridSpec(
        num_scalar_prefetch=1,
        grid=(bsz, n_pages // group),
        in_specs=([vec, vec, vec, pl.BlockSpec((None, 1, LANES), lambda b, s, pt: (b, 0, 0))]
                  + [paged((rows, dh), g) for g in range(group)]
                  + [paged((rows, dh), g) for g in range(group)]
                  + [paged((rows // LANES, LANES), g) for g in range(group)]
                  + [const, const]),
        out_specs=vec,
        scratch_shapes=[pltpu.VMEM((heads, 1), F32), pltpu.VMEM((heads, 1), F32),
                        pltpu.VMEM((heads, dh), F32), pltpu.VMEM((1, LANES), F32)],
    )
    return pl.pallas_call(
        functools.partial(_fox_decode_kernel, heads=heads, group=group, scale=dh ** -0.5),
        out_shape=jax.ShapeDtypeStruct((bsz, heads, dh), BF16),
        grid_spec=grid_spec,
        compiler_params=_params("arbitrary", "arbitrary"),
        name="fox_decode",
    )(page_table, q, k_new, v_new, cq, *([cache_k] * group), *([cache_v] * group), *([cache_f] * group),
      u_same, u_later)


def kernel(x_prompt, x_sample, mem_prompt, cache_fox_k, cache_fox_v, cache_fox_logf, state_gla, cache_mem_k, cache_mem_v, page_table, g_mix, w_in, b_forget, w_gla_gate, b_gla_gate, g_gla_head, w_out, g_cross, g_mem, w_xq, w_xk, w_xv, w_xo, g_ffn, w_ffn_gate, w_ffn_up, w_ffn_down, g_final):
    depth = w_in.shape[0]
    bsz, seq, d = x_prompt.shape
    sb = x_sample.shape[0]
    mem = mem_prompt.shape[1]
    _, pool, page, fh, fdh = cache_fox_k.shape
    _, _, gh, gdk, gdv = state_gla.shape
    xh, xdh = cache_mem_k.shape[3], cache_mem_k.shape[4]
    fox_dim, gk_dim, gv_dim = fh * fdh, gh * gdk, gh * gdv
    rank = w_gla_gate.shape[1]
    n = bsz * seq
    c_forget = 3 * fox_dim
    c_gla = c_forget + fh
    c_rank = c_gla + 2 * gk_dim + 2 * gv_dim

    xp = x_prompt.reshape(n, d)
    xs = x_sample.reshape(sb, d)
    memx = mem_prompt.reshape(bsz * mem, d)
    ck = cache_fox_k.reshape(depth, pool, page * fh, fdh)
    cv = cache_fox_v.reshape(depth, pool, page * fh, fdh)
    cf = cache_fox_logf.reshape(depth, pool, page * fh // LANES, LANES)
    row = lambda v: v.reshape(1, -1)
    zeros_state = jnp.zeros((bsz, gh, gdk, gdv), F32)
    group = min(DECODE_PAGES_PER_STEP, page_table.shape[1])

    w_main = _w_main(w_in, c_a=c_forget, c_b0=c_gla, c_b1=c_rank, tr=W_PREP_ROWS)
    w_small = jnp.concatenate([w_in[:, :, c_forget:c_gla], w_in[:, :, c_rank:],
                               jnp.zeros((depth, d, LANES - fh - rank), F32)], axis=2).astype(BF16)
    w_g2 = jnp.concatenate([jnp.zeros((depth, fh, gk_dim), F32), w_gla_gate,
                            jnp.zeros((depth, LANES - fh - rank, gk_dim), F32)], axis=1).astype(BF16)
    b_f = jnp.concatenate([b_forget, jnp.zeros((depth, LANES - fh), F32)], axis=1)
    w_kv = jnp.concatenate([w_xk, w_xv], axis=2)
    w_o = w_out.astype(BF16)

    pk, pv, pf, ps, pmk, pmv, sk, sv, sf, ss = ([] for _ in range(10))
    for l in range(depth):
        last = l == depth - 1

        z, logf, ga, k3, v3, drow = _in_proj(xp, row(g_mix[l]), w_main, w_small, b_f[l:l + 1], w_g2,
                                             row(b_gla_gate[l]), layer=l, n_fox_heads=fh, fox_dh=fdh, seq=seq,
                                             tm=min(ROW_TILE_PROJ, seq), tn=COL_TILE)
        o_f = _fox_prompt(z, drow, batch=bsz, seq=seq, heads=fh, dh=fdh, tq=min(FOX_Q_TILE, seq),
                          hp=FOX_HEADS_PER_STEP)
        o_g, s_fin = _gla(z, ga, zeros_state, row(g_gla_head[l]), batch=bsz, seq=seq, chunk=min(GLA_CHUNK, seq),
                          heads=gh, dk=gdk, dv=gdv, col0=3 * fox_dim)
        xp = _out_proj(o_f, o_g, w_o, xp, layer=l, tm=min(ROW_TILE_OUT, n))
        mkv = _norm_matmul(memx, row(g_mem[l]), w_kv, layer=l, tm=bsz * mem, tn=COL_TILE)
        mk, mv = mkv[:, :xh * xdh], mkv[:, xh * xdh:]
        xp = _cross_attn(xp.reshape(bsz, seq, d), row(g_cross[l]), w_xq, mk.reshape(bsz, mem, -1),
                         mv.reshape(bsz, mem, -1), w_xo, layer=l, heads=xh, dh=xdh,
                         tm=min(ROW_TILE_XATTN, seq)).reshape(n, d)
        pk.append(k3.reshape(bsz, seq, fh, fdh))
        pv.append(v3.reshape(bsz, seq, fh, fdh))
        pf.append(logf.reshape(bsz, seq, fh))
        ps.append(s_fin)
        pmk.append(mk.reshape(bsz, mem, xh, xdh))
        pmv.append(mv.reshape(bsz, mem, xh, xdh))

        zs, logf_s, ga_s, fk, fv = _in_proj(xs, row(g_mix[l]), w_main, w_small, b_f[l:l + 1], w_g2,
                                            row(b_gla_gate[l]), layer=l, n_fox_heads=fh, fox_dh=fdh, seq=0,
                                            tm=sb, tn=COL_TILE)
        fq, fk, fv = zs[:, :fox_dim].reshape(sb, fh, fdh), fk.reshape(sb, fh, fdh), fv.reshape(sb, fh, fdh)
        cq = jnp.tile(logf_s, (1, LANES // fh)).reshape(sb, 1, LANES)
        o_fs = _fox_decode(fq, fk, fv, cq, ck, cv, cf, page_table, layer=l, group=group).reshape(sb, fox_dim)
        cs = GLA_CHUNK_SAMPLE
        pad = lambda a: jnp.zeros((sb, cs, a.shape[1]), a.dtype).at[:, 0].set(a).reshape(sb * cs, -1)
        o_gs, s_new = _gla(pad(zs), pad(ga_s), state_gla[l], row(g_gla_head[l]), batch=sb, seq=cs, chunk=cs,
                           heads=gh, dk=gdk, dv=gdv, col0=3 * fox_dim)
        o_gs = o_gs.reshape(sb, cs, gv_dim)[:, 0]
        xs = _out_proj(o_fs, o_gs, w_o, xs, layer=l, tm=sb)
        xs = _cross_attn(xs.reshape(sb, 1, d), row(g_cross[l]), w_xq, cache_mem_k[l].reshape(sb, mem, -1),
                         cache_mem_v[l].reshape(sb, mem, -1), w_xo, layer=l, heads=xh, dh=xdh, tm=1).reshape(sb, d)
        xp, xs = _swiglu(xp, xs, row(g_ffn[l]), w_ffn_gate, w_ffn_up, w_ffn_down, row(g_final), layer=l,
                         final_norm=last, tm=min(ROW_TILE_FFN, n), tf=FFN_TILE)
        sk.append(fk.reshape(sb, 1, fh, fdh))
        sv.append(fv.reshape(sb, 1, fh, fdh))
        sf.append(logf_s.reshape(sb, 1, fh))
        ss.append(s_new)

    return (xp.reshape(bsz, seq, d), xs.reshape(sb, 1, d),
            jnp.stack(pk), jnp.stack(pv), jnp.stack(pf), jnp.stack(ps), jnp.stack(pmk), jnp.stack(pmv),
            jnp.stack(sk), jnp.stack(sv), jnp.stack(sf), jnp.stack(ss))
```

```python
import functools

import numpy as np
import jax
import jax.numpy as jnp
from jax import lax
from jax.experimental import pallas as pl
from jax.experimental.pallas import tpu as pltpu

F32 = jnp.float32
BF16 = jnp.bfloat16

RMS_EPS = 1e-6
LOG2E = 1.4426950408889634
GLA_TAU = 16.0
GLA_CHUNK = 128
GLA_CHUNK_SAMPLE = 64
LANES = 128
SUBLANES = 8
VMEM_LIMIT_BYTES = 56 * 1024 * 1024
ROW_TILE_PROJ = 1024
ROW_TILE_XATTN = 512
ROW_TILE_FFN = 1024
FFN_TILE = 256
COL_TILE = 512
FOX_Q_TILE = 512
FOX_HEADS_PER_STEP = 2
DECODE_PAGES_PER_STEP = 8
MXU_COLS = 256
W_PREP_ROWS = 256
ROW_TILE_OUT = 512


def _params(*sem):
    return pltpu.CompilerParams(dimension_semantics=sem, vmem_limit_bytes=VMEM_LIMIT_BYTES)


def _rms(x, g):
    return x * lax.rsqrt(jnp.mean(x * x, axis=-1, keepdims=True) + RMS_EPS) * g


def _log_sigmoid(x):
    return jnp.minimum(x, 0.0) - jnp.log1p(jnp.exp(-jnp.abs(x)))


def _silu(x):
    return x / (1.0 + jnp.exp(-x))


def _dot(a, b):
    return jnp.dot(a, b, preferred_element_type=F32)


def _dot_nt(a, b):
    return lax.dot_general(a, b, (((1,), (1,)), ((), ())), preferred_element_type=F32)


def _dot_tn(a, b):
    return lax.dot_general(a, b, (((0,), (0,)), ((), ())), preferred_element_type=F32)


def _split3(x):
    a = x.astype(BF16)
    r = x - a.astype(F32)
    b = r.astype(BF16)
    c = (r - b.astype(F32)).astype(BF16)
    return a, b, c


def _dot3(m, x, dot=_dot):
    a, b, c = _split3(x)
    return dot(m, a) + dot(m, b) + dot(m, c)


def _w_main_kernel(w_ref, wf_ref, wr_ref, o_ref, os_ref):
    o_ref[...] = w_ref[0].astype(BF16)

    @pl.when(pl.program_id(1) == 0)
    def _():
        pad = jnp.zeros((os_ref.shape[0] - wf_ref.shape[1] - wr_ref.shape[1], os_ref.shape[1]), F32)
        os_ref[...] = jnp.concatenate([wf_ref[0], wr_ref[0], pad], axis=0).astype(BF16)


def _w_main(w_t, *, c_a, c_b0, c_b1, tr):
    depth, n_in, d = w_t.shape
    rows = c_a + c_b1 - c_b0

    def src(l, r):
        start = r * tr
        return l, pl.multiple_of(start + jnp.where(start >= c_a, c_b0 - c_a, 0), SUBLANES), 0

    rows_at = lambda n, at: pl.BlockSpec((pl.Element(1), pl.Element(n), pl.Element(d)), lambda l, r: (l, at, 0))
    return pl.pallas_call(
        _w_main_kernel,
        out_shape=[jax.ShapeDtypeStruct((depth, rows, d), BF16), jax.ShapeDtypeStruct((depth, LANES, d), BF16)],
        grid=(depth, rows // tr),
        in_specs=[pl.BlockSpec((pl.Element(1), pl.Element(tr), pl.Element(d)), src),
                  rows_at(c_b0 - c_a, c_a), rows_at(n_in - c_b1, c_b1)],
        out_specs=[pl.BlockSpec((None, tr, d), lambda l, r: (l, r, 0)),
                   pl.BlockSpec((None, LANES, d), lambda l, r: (l, 0, 0))],
        compiler_params=_params("arbitrary", "arbitrary"),
        name="w_main",
    )(w_t, w_t, w_t)


def _in_proj_kernel(x_ref, g_ref, w_ref, ws_ref, bf_ref, wg2_ref, bg_ref, *rest, n_fox_heads, tiles_per_seq):
    if tiles_per_seq:
        z_ref, logf_ref, ga_ref, k3_ref, v3_ref, drow_ref, hb_ref, carry_ref = rest
    else:
        z_ref, logf_ref, ga_ref, k3_ref, v3_ref, hb_ref = rest
    i = pl.program_id(0)
    j = pl.program_id(1)

    @pl.when(j == 0)
    def _():
        hb = _rms(x_ref[...], g_ref[...]).astype(BF16)
        hb_ref[...] = hb
        zs = _dot_nt(hb, ws_ref[...])
        lf = _log_sigmoid(zs + bf_ref[...])
        logf_ref[...] = lf[:, :n_fox_heads]
        gp = _dot(zs.astype(BF16), wg2_ref[...]) + bg_ref[...]
        ga_ref[...] = _log_sigmoid(gp) * (1.0 / GLA_TAU)
        if tiles_per_seq:
            tm = lf.shape[0]
            r = lax.broadcasted_iota(jnp.int32, (LANES, LANES), 0)
            c = lax.broadcasted_iota(jnp.int32, (LANES, LANES), 1)
            tri = jnp.where(r <= c, 1.0, 0.0).astype(BF16)
            lft = lf.T[:SUBLANES]
            run = jnp.where(i % tiles_per_seq == 0, 0.0, carry_ref[...])
            for c0 in range(0, tm, LANES):
                d = _dot3(tri, lft[:, c0:c0 + LANES], dot=lambda m, x: _dot(x, m)) + run
                drow_ref[:, c0:c0 + LANES] = d
                run = d[:, LANES - 1:LANES]
            carry_ref[...] = run

    zt = _dot_nt(hb_ref[...], w_ref[...])
    z_ref[...] = zt
    tm, tn = zt.shape
    dh = k3_ref.shape[1]
    heads_per_tile = tn // dh
    tiles_per_group = n_fox_heads // heads_per_tile
    for group, out_ref in ((1, k3_ref), (2, v3_ref)):
        for t in range(tiles_per_group):
            @pl.when(j == group * tiles_per_group + t)
            def _(out_ref=out_ref, t=t):
                for c in range(heads_per_tile):
                    out_ref[pl.ds(t * heads_per_tile + c, tm, stride=n_fox_heads), :] = zt[:, c * dh:(c + 1) * dh]


def _in_proj(x, g, w_main, w_small, b_f, w_g2, b_g, *, layer, n_fox_heads, fox_dh, seq, tm, tn):
    n, d = x.shape
    nz = w_main.shape[1]
    kg = w_g2.shape[2]
    tiles_per_seq = seq // tm if seq else 0
    out_shape = [jax.ShapeDtypeStruct((n, nz), F32),
                 jax.ShapeDtypeStruct((n, n_fox_heads), F32),
                 jax.ShapeDtypeStruct((n, kg), F32),
                 jax.ShapeDtypeStruct((n * n_fox_heads, fox_dh), F32),
                 jax.ShapeDtypeStruct((n * n_fox_heads, fox_dh), F32)]
    out_specs = [pl.BlockSpec((tm, tn), lambda i, j: (i, j)),
                 pl.BlockSpec((tm, n_fox_heads), lambda i, j: (i, 0)),
                 pl.BlockSpec((tm, kg), lambda i, j: (i, 0)),
                 pl.BlockSpec((tm * n_fox_heads, fox_dh), lambda i, j: (i, 0)),
                 pl.BlockSpec((tm * n_fox_heads, fox_dh), lambda i, j: (i, 0))]
    scratch = [pltpu.VMEM((tm, d), BF16)]
    if tiles_per_seq:
        out_shape.append(jax.ShapeDtypeStruct((n // seq, SUBLANES, seq), F32))
        out_specs.append(pl.BlockSpec((None, SUBLANES, tm), lambda i, j: (i // tiles_per_seq, 0, i % tiles_per_seq)))
        scratch.append(pltpu.VMEM((SUBLANES, 1), F32))
    return pl.pallas_call(
        functools.partial(_in_proj_kernel, n_fox_heads=n_fox_heads, tiles_per_seq=tiles_per_seq),
        out_shape=out_shape,
        grid=(n // tm, nz // tn),
        in_specs=[pl.BlockSpec((tm, d), lambda i, j: (i, 0)),
                  pl.BlockSpec((1, d), lambda i, j: (0, 0)),
                  pl.BlockSpec((None, tn, d), lambda i, j: (layer, j, 0)),
                  pl.BlockSpec((None, LANES, d), lambda i, j: (layer, 0, 0)),
                  pl.BlockSpec((1, LANES), lambda i, j: (0, 0)),
                  pl.BlockSpec((None, LANES, kg), lambda i, j: (layer, 0, 0)),
                  pl.BlockSpec((1, kg), lambda i, j: (0, 0))],
        out_specs=out_specs,
        scratch_shapes=scratch,
        compiler_params=_params("arbitrary", "arbitrary"),
        name="in_proj",
    )(x, g, w_main, w_small, b_f, w_g2, b_g)


def _norm_matmul_kernel(x_ref, g_ref, w_ref, o_ref, hb_ref):
    @pl.when(pl.program_id(1) == 0)
    def _():
        hb_ref[...] = _rms(x_ref[...], g_ref[...]).astype(BF16)

    o_ref[...] = _dot(hb_ref[...], w_ref[...].astype(BF16))


def _norm_matmul(x, g, w, *, layer, tm, tn):
    n, d = x.shape
    nout = w.shape[2]
    return pl.pallas_call(
        _norm_matmul_kernel,
        out_shape=jax.ShapeDtypeStruct((n, nout), F32),
        grid=(n // tm, nout // tn),
        in_specs=[pl.BlockSpec((tm, d), lambda i, j: (i, 0)),
                  pl.BlockSpec((1, d), lambda i, j: (0, 0)),
                  pl.BlockSpec((None, d, tn), lambda i, j: (layer, 0, j))],
        out_specs=pl.BlockSpec((tm, tn), lambda i, j: (i, j)),
        scratch_shapes=[pltpu.VMEM((tm, d), BF16)],
        compiler_params=_params("arbitrary", "arbitrary"),
        name="norm_matmul",
    )(x, g, w)


def _fox_prompt_kernel(q_ref, k_ref, v_ref, d_ref, o_ref, kb_ref, vb_ref, *, tq, dh, scale):
    hg = pl.program_id(1)
    qi = pl.program_id(2)
    hp = q_ref.shape[1] // dh

    @pl.when(qi == 0)
    def _():
        kb_ref[...] = k_ref[...].astype(BF16)
        vb_ref[...] = v_ref[...].astype(BF16)

    qs = [(q_ref[:, i * dh:(i + 1) * dh] * (scale * LOG2E)).astype(BF16) for i in range(hp)]

    def block(j, carry, masked):
        start = pl.multiple_of(j * tq, tq)
        out = []
        for i in range(hp):
            m, l, acc = carry[i]
            kj = kb_ref[pl.ds(start, tq), i * dh:(i + 1) * dh]
            vj = vb_ref[pl.ds(start, tq), i * dh:(i + 1) * dh]
            s = _dot_nt(qs[i], kj) - d_ref[pl.ds(hg * hp + i, 1), pl.ds(start, tq)] * LOG2E
            if masked:
                r = lax.broadcasted_iota(jnp.int32, (tq, tq), 0)
                c = lax.broadcasted_iota(jnp.int32, (tq, tq), 1)
                s = jnp.where(c <= r, s, -jnp.inf)
            m_new = jnp.maximum(m, jnp.max(s, axis=-1, keepdims=True))
            a = jnp.exp2(m - m_new)
            p = jnp.exp2(s - m_new)
            l = a * l + jnp.sum(p, axis=-1, keepdims=True)
            acc = a * acc + _dot(p.astype(BF16), vj)
            out.append((m_new, l, acc))
        return tuple(out)

    init = tuple((jnp.full((tq, 1), -jnp.inf, F32), jnp.zeros((tq, 1), F32), jnp.zeros((tq, dh), F32))
                 for _ in range(hp))
    carry = lax.fori_loop(0, qi, lambda j, c: block(j, c, False), init)
    final = block(qi, carry, True)
    for i, (_, l, acc) in enumerate(final):
        o_ref[:, i * dh:(i + 1) * dh] = (acc / l).astype(o_ref.dtype)


def _fox_prompt(z, drow, *, batch, seq, heads, dh, tq, hp):
    n = z.shape[0]
    nq = seq // tq
    hg = heads // hp
    w = hp * dh
    return pl.pallas_call(
        functools.partial(_fox_prompt_kernel, tq=tq, dh=dh, scale=dh ** -0.5),
        out_shape=jax.ShapeDtypeStruct((n, heads * dh), BF16),
        grid=(batch, hg, nq),
        in_specs=[pl.BlockSpec((tq, w), lambda b, h, i: (b * nq + i, h)),
                  pl.BlockSpec((seq, w), lambda b, h, i: (b, hg + h)),
                  pl.BlockSpec((seq, w), lambda b, h, i: (b, 2 * hg + h)),
                  pl.BlockSpec((None, SUBLANES, seq), lambda b, h, i: (b, 0, 0))],
        out_specs=pl.BlockSpec((tq, w), lambda b, h, i: (b * nq + i, h)),
        scratch_shapes=[pltpu.VMEM((seq, w), BF16), pltpu.VMEM((seq, w), BF16)],
        compiler_params=_params("arbitrary", "arbitrary", "arbitrary"),
        name="fox_prompt",
    )(z, z, z, drow)


def _gla_tables(c):
    r = np.arange(c)
    tri = (r[None, :] <= r[:, None]).astype(np.float32)
    masks = [np.eye(c, dtype=np.float32)]
    m = c // 2
    while m >= 1:
        same = (r[:, None] // (2 * m)) == (r[None, :] // (2 * m))
        upper = (r[:, None] % (2 * m)) >= m
        lower = (r[None, :] % (2 * m)) < m
        masks.append((same & upper & lower).astype(np.float32))
        m //= 2
    return tri, np.concatenate(masks, 0)


def _gla_level_refs(b, b_ref, col0):
    c, dk = b.shape
    bcast = lambda i, rows: jnp.broadcast_to(b_ref[pl.ds(i, 1), pl.ds(col0, dk)], (rows, dk))
    sub = lax.broadcasted_iota(jnp.int32, (c, dk), 0) % SUBLANES
    out = []
    m = c // 2
    while m >= 1:
        if 2 * m >= SUBLANES:
            out.append(jnp.concatenate([bcast(blk * 2 * m + m - 1, 2 * m) for blk in range(c // (2 * m))], axis=0))
        elif m > 1:
            ref = None
            for t in range(SUBLANES // (2 * m)):
                rows = jnp.concatenate([bcast(g * SUBLANES + t * 2 * m + m - 1, SUBLANES)
                                        for g in range(c // SUBLANES)], axis=0)
                ref = rows if ref is None else jnp.where(sub // (2 * m) == t, rows, ref)
            out.append(ref)
        else:
            out.append(jnp.where(sub % 2 == 0, b, pltpu.roll(b, 1, 0)))
        m //= 2
    return out


def _gla_kernel(q_ref, k_ref, v_ref, r_ref, ga_ref, s0_ref, tri_ref, masks_ref, gh_ref,
                og_ref, sfin_ref, st_ref, b_ref, *, heads, dk, dv, scale):
    ci = pl.program_id(1)

    @pl.when(ci == 0)
    def _():
        for h in range(heads):
            st_ref[h] = s0_ref[h].T

    c = q_ref.shape[0]
    b_all = _dot3(tri_ref[...], ga_ref[...])
    b_ref[...] = b_all
    for h in range(heads):
        q = q_ref[:, h * dk:(h + 1) * dk] * scale
        k = k_ref[:, h * dk:(h + 1) * dk]
        v = v_ref[:, h * dv:(h + 1) * dv]
        b = b_all[:, h * dk:(h + 1) * dk]
        a = jnp.where(masks_ref[0:c, :] != 0, jnp.sum(q * k, axis=-1, keepdims=True), 0.0)
        for lv, ref in enumerate(_gla_level_refs(b, b_ref, h * dk)):
            d = b - ref
            qt = (q * jnp.exp(jnp.minimum(d, 0.0))).astype(BF16)
            kt = (k * jnp.exp(jnp.minimum(-d, 0.0))).astype(BF16)
            a = a + jnp.where(masks_ref[(lv + 1) * c:(lv + 2) * c, :] != 0, _dot_nt(qt, kt), 0.0)
        s_old = st_ref[h]
        o = _dot(a.astype(BF16), v.astype(BF16)) + _dot_nt((q * jnp.exp(b)).astype(BF16), s_old.astype(BF16))
        b_last = b[c - 1:c, :]
        upd = _dot(v.T.astype(BF16), (k * jnp.exp(b_last - b)).astype(BF16))
        st_ref[h] = s_old * jnp.exp(b_last) + upd
        y = _rms(o, gh_ref[...])
        og_ref[:, h * dv:(h + 1) * dv] = (y * _silu(r_ref[:, h * dv:(h + 1) * dv])).astype(og_ref.dtype)

    @pl.when(ci == pl.num_programs(1) - 1)
    def _():
        for h in range(heads):
            sfin_ref[h] = st_ref[h].T


def _gla(z, ga, s0, g_head, *, batch, seq, chunk, heads, dk, dv, col0):
    n = z.shape[0]
    c = chunk
    nc = seq // c
    wk, wv = heads * dk, heads * dv
    tri, masks = _gla_tables(c)
    row = lambda b, i: b * nc + i
    return pl.pallas_call(
        functools.partial(_gla_kernel, heads=heads, dk=dk, dv=dv, scale=dk ** -0.5),
        out_shape=[jax.ShapeDtypeStruct((n, wv), BF16), jax.ShapeDtypeStruct((batch, heads, dk, dv), F32)],
        grid=(batch, nc),
        in_specs=[pl.BlockSpec((c, wk), lambda b, i: (row(b, i), col0 // wk)),
                  pl.BlockSpec((c, wk), lambda b, i: (row(b, i), col0 // wk + 1)),
                  pl.BlockSpec((c, wv), lambda b, i: (row(b, i), (col0 + 2 * wk) // wv)),
                  pl.BlockSpec((c, wv), lambda b, i: (row(b, i), (col0 + 2 * wk) // wv + 1)),
                  pl.BlockSpec((c, wk), lambda b, i: (row(b, i), 0)),
                  pl.BlockSpec((None, heads, dk, dv), lambda b, i: (b, 0, 0, 0)),
                  pl.BlockSpec(tri.shape, lambda b, i: (0, 0)),
                  pl.BlockSpec(masks.shape, lambda b, i: (0, 0)),
                  pl.BlockSpec((1, dv), lambda b, i: (0, 0))],
        out_specs=[pl.BlockSpec((c, wv), lambda b, i: (row(b, i), 0)),
                   pl.BlockSpec((None, heads, dk, dv), lambda b, i: (b, 0, 0, 0))],
        scratch_shapes=[pltpu.VMEM((heads, dv, dk), F32), pltpu.VMEM((c, wk), F32)],
        compiler_params=_params("arbitrary", "arbitrary"),
        name="gla",
    )(z, z, z, z, ga, s0, jnp.asarray(tri, BF16), jnp.asarray(masks, F32), g_head)


def _out_proj_kernel(a_ref, b_ref, w1_ref, w2_ref, x_ref, o_ref):
    o_ref[...] = x_ref[...] + _dot(a_ref[...], w1_ref[...]) + _dot(b_ref[...], w2_ref[...])


def _out_proj(a, b, w, x, *, layer, tm):
    n, d = x.shape
    kh = a.shape[1]
    return pl.pallas_call(
        _out_proj_kernel,
        out_shape=jax.ShapeDtypeStruct((n, d), F32),
        grid=(n // tm,),
        in_specs=[pl.BlockSpec((tm, kh), lambda i: (i, 0)),
                  pl.BlockSpec((tm, kh), lambda i: (i, 0)),
                  pl.BlockSpec((None, kh, d), lambda i: (layer, 0, 0)),
                  pl.BlockSpec((None, kh, d), lambda i: (layer, 1, 0)),
                  pl.BlockSpec((tm, d), lambda i: (i, 0))],
        out_specs=pl.BlockSpec((tm, d), lambda i: (i, 0)),
        compiler_params=_params("arbitrary"),
        name="out_proj",
    )(a, b, w, w, x)


def _cross_attn_kernel(x_ref, g_ref, wq_ref, mk_ref, mv_ref, wo_ref, o_ref, *, heads, dh):
    x = x_ref[...]
    tm, d = x.shape
    xr = x if tm >= SUBLANES else jnp.broadcast_to(x, (SUBLANES, d))
    q = _dot(_rms(xr, g_ref[...]).astype(BF16), wq_ref[...].astype(BF16)) * (dh ** -0.5)
    outs = []
    for h in range(heads):
        sl = slice(h * dh, (h + 1) * dh)
        s = _dot_nt(q[:, sl].astype(BF16), mk_ref[:, sl].astype(BF16))
        p = jnp.exp(s - jnp.max(s, axis=-1, keepdims=True))
        p = p / jnp.sum(p, axis=-1, keepdims=True)
        outs.append(_dot(p.astype(BF16), mv_ref[:, sl].astype(BF16)))
    y = _dot(jnp.concatenate(outs, axis=1).astype(BF16), wo_ref[...].astype(BF16))
    o_ref[...] = x + y[:tm]


def _cross_attn(x3, g, wq, mk, mv, wo, *, layer, heads, dh, tm):
    bsz, rows, d = x3.shape
    mem = mk.shape[1]
    xd = heads * dh
    return pl.pallas_call(
        functools.partial(_cross_attn_kernel, heads=heads, dh=dh),
        out_shape=jax.ShapeDtypeStruct(x3.shape, F32),
        grid=(bsz, rows // tm),
        in_specs=[pl.BlockSpec((None, tm, d), lambda b, i: (b, i, 0)),
                  pl.BlockSpec((1, d), lambda b, i: (0, 0)),
                  pl.BlockSpec((None, d, xd), lambda b, i: (layer, 0, 0)),
                  pl.BlockSpec((None, mem, xd), lambda b, i: (b, 0, 0)),
                  pl.BlockSpec((None, mem, xd), lambda b, i: (b, 0, 0)),
                  pl.BlockSpec((None, xd, d), lambda b, i: (layer, 0, 0))],
        out_specs=pl.BlockSpec((None, tm, d), lambda b, i: (b, i, 0)),
        compiler_params=_params("arbitrary", "arbitrary"),
        name="cross_attn",
    )(x3, g, wq, mk, mv, wo)


def _swiglu_kernel(x_ref, xs_ref, g_ref, wg_ref, wu_ref, wd_ref, gf_ref, o_ref, os_ref, hb_ref, hs_ref, *, final_norm):
    i = pl.program_id(0)
    f = pl.program_id(1)
    last = pl.num_programs(1) - 1

    @pl.when(f == 0)
    def _():
        x = x_ref[...]
        hb_ref[...] = _rms(x, g_ref[...]).astype(BF16)
        o_ref[...] = x

    wg, wu, wd = wg_ref[...].astype(BF16), wu_ref[...].astype(BF16), wd_ref[...].astype(BF16)

    def ffn(hb):
        return _dot((_silu(_dot(hb, wg)) * _dot(hb, wu)).astype(BF16), wd)

    o_ref[...] += ffn(hb_ref[...])

    @pl.when(f == last)
    def _():
        if final_norm:
            o_ref[...] = _rms(o_ref[...], gf_ref[...])

    @pl.when(i == 0)
    def _():
        @pl.when(f == 0)
        def _():
            xs = xs_ref[...]
            hs_ref[...] = _rms(xs, g_ref[...]).astype(BF16)
            os_ref[...] = xs

        os_ref[...] += ffn(hs_ref[...])

        @pl.when(f == last)
        def _():
            if final_norm:
                os_ref[...] = _rms(os_ref[...], gf_ref[...])


def _swiglu(x, xs, g, wg, wu, wd, g_final, *, layer, final_norm, tm, tf):
    n, d = x.shape
    ns = xs.shape[0]
    ffn = wg.shape[2]
    return pl.pallas_call(
        functools.partial(_swiglu_kernel, final_norm=final_norm),
        out_shape=[jax.ShapeDtypeStruct((n, d), F32), jax.ShapeDtypeStruct((ns, d), F32)],
        grid=(n // tm, ffn // tf),
        in_specs=[pl.BlockSpec((tm, d), lambda i, f: (i, 0)),
                  pl.BlockSpec((ns, d), lambda i, f: (0, 0)),
                  pl.BlockSpec((1, d), lambda i, f: (0, 0)),
                  pl.BlockSpec((None, d, tf), lambda i, f: (layer, 0, f)),
                  pl.BlockSpec((None, d, tf), lambda i, f: (layer, 0, f)),
                  pl.BlockSpec((None, tf, d), lambda i, f: (layer, f, 0)),
                  pl.BlockSpec((1, d), lambda i, f: (0, 0))],
        out_specs=[pl.BlockSpec((tm, d), lambda i, f: (i, 0)),
                   pl.BlockSpec((ns, d), lambda i, f: (0, 0))],
        scratch_shapes=[pltpu.VMEM((tm, d), BF16), pltpu.VMEM((ns, d), BF16)],
        compiler_params=_params("arbitrary", "arbitrary"),
        name="swiglu",
    )(x, xs, g, wg, wu, wd, g_final)


def _suffix_rows_exclusive(x):
    n = x.shape[0]
    row = lax.broadcasted_iota(jnp.int32, x.shape, 0)

    def up(a, k):
        return jnp.where(row + k < n, pltpu.roll(a, n - k, 0), 0.0)

    y = up(x, 1)
    k = 1
    while k < n:
        y = y + up(y, k)
        k *= 2
    return y


def _fox_decode_kernel(pt_ref, q_ref, kn_ref, vn_ref, cq_ref, *rest, heads, group, scale):
    del pt_ref
    k_refs, v_refs, f_refs = rest[:group], rest[group:2 * group], rest[2 * group:3 * group]
    u_same_ref, u_later_ref, o_ref, m_ref, l_ref, acc_ref, tail_ref = rest[3 * group:]
    step = pl.program_id(1)

    @pl.when(step == 0)
    def _():
        m_ref[...] = jnp.full_like(m_ref, -jnp.inf)
        l_ref[...] = jnp.zeros_like(l_ref)
        acc_ref[...] = jnp.zeros_like(acc_ref)
        tail_ref[...] = jnp.zeros_like(tail_ref)

    n_blk = k_refs[0].shape[0] // LANES
    wide = MXU_COLS // LANES
    q = q_ref[...]
    qb = q.astype(BF16)
    sub = lax.broadcasted_iota(jnp.int32, (heads, MXU_COLS), 0)
    lane = lax.broadcasted_iota(jnp.int32, (heads, MXU_COLS), 1)
    own = sub == lane % heads
    parts = _split3(jnp.concatenate([f[...] for f in f_refs], axis=0))
    rows_later = sum(_dot(x, u_later_ref[...]) for x in parts)
    rows_total = sum(_dot(x, u_same_ref[...]) for x in parts)
    tail = tail_ref[...]
    tiles = []
    for g in range(group):
        row_total = rows_total[g * n_blk:(g + 1) * n_blk]
        bias = cq_ref[...] + tail + rows_later[g * n_blk:(g + 1) * n_blk] + _suffix_rows_exclusive(row_total)
        tail = tail + jnp.sum(row_total, axis=0, keepdims=True)
        for r in range(0, n_blk, wide):
            kb = k_refs[g][r * LANES:(r + wide) * LANES, :].astype(BF16)
            brow = jnp.concatenate([bias[r + w:r + w + 1, :] for w in range(wide)], axis=1)
            s = jnp.where(own, _dot_nt(qb, kb) * scale + brow, -jnp.inf)
            tiles.extend(s[:, w * LANES:(w + 1) * LANES] for w in range(wide))
    tail_ref[...] = tail

    top = tiles[0]
    for t in tiles[1:]:
        top = jnp.maximum(top, t)
    m_old = m_ref[...]
    m_new = jnp.maximum(m_old, jnp.max(top, axis=-1, keepdims=True))
    a = jnp.exp(m_old - m_new)
    accs = [a * acc_ref[...], jnp.zeros(acc_ref.shape, F32)]
    psum = jnp.zeros((heads, LANES), F32)
    dh = acc_ref.shape[1]
    for n, idx in enumerate(range(0, len(tiles), wide)):
        g, r = divmod(idx, n_blk)
        prs = [jnp.exp(tiles[idx + w] - m_new) for w in range(wide)]
        for pr in prs:
            psum = psum + pr
        vb = jnp.concatenate([v_refs[g][(r + w) * LANES:(r + w + 1) * LANES, :].astype(BF16) for w in range(wide)],
                             axis=1)
        res = _dot(jnp.concatenate(prs, axis=0).astype(BF16), vb)
        for w in range(wide):
            accs[n % 2] = accs[n % 2] + res[w * heads:(w + 1) * heads, w * dh:(w + 1) * dh]
    m_ref[...] = m_new
    l_ref[...] = a * l_ref[...] + jnp.sum(psum, axis=-1, keepdims=True)
    acc_ref[...] = accs[0] + accs[1]

    @pl.when(step == pl.num_programs(1) - 1)
    def _():
        s_new = jnp.sum(q * kn_ref[...], axis=-1, keepdims=True) * scale
        m_fin = jnp.maximum(m_ref[...], s_new)
        a_fin = jnp.exp(m_ref[...] - m_fin)
        p_new = jnp.exp(s_new - m_fin)
        o_ref[...] = ((a_fin * acc_ref[...] + p_new * vn_ref[...]) / (a_fin * l_ref[...] + p_new)).astype(o_ref.dtype)


def _fox_decode(q, k_new, v_new, cq, cache_k, cache_v, cache_f, page_table, *, layer, group):
    bsz, heads, dh = q.shape
    rows = cache_k.shape[2]
    n_pages = page_table.shape[1]
    lane = np.arange(LANES)
    same = lane[:, None] % heads == lane[None, :] % heads
    u_same = jnp.asarray(same, BF16)
    u_later = jnp.asarray(same & (lane[:, None] > lane[None, :]), BF16)
    vec = pl.BlockSpec((None, heads, dh), lambda b, s, pt: (b, 0, 0))

    def paged(shape, g):
        return pl.BlockSpec((None, None) + shape, lambda b, s, pt: (layer, pt[b, n_pages - 1 - (s * group + g)], 0, 0))

    const = pl.BlockSpec((LANES, LANES), lambda b, s, pt: (0, 0))
    grid_spec = pltpu.PrefetchScalarGridSpec(
        num_scalar_prefetch=1,
        grid=(bsz, n_pages // group),
        in_specs=([vec, vec, vec, pl.BlockSpec((None, 1, LANES), lambda b, s, pt: (b, 0, 0))]
                  + [paged((rows, dh), g) for g in range(group)]
                  + [paged((rows, dh), g) for g in range(group)]
                  + [paged((rows // LANES, LANES), g) for g in range(group)]
                  + [const, const]),
        out_specs=vec,
        scratch_shapes=[pltpu.VMEM((heads, 1), F32), pltpu.VMEM((heads, 1), F32),
                        pltpu.VMEM((heads, dh), F32), pltpu.VMEM((1, LANES), F32)],
    )
    return pl.pallas_call(
        functools.partial(_fox_decode_kernel, heads=heads, group=group, scale=dh ** -0.5),
        out_shape=jax.ShapeDtypeStruct((bsz, heads, dh), BF16),
        grid_spec=grid_spec,
        compiler_params=_params("arbitrary", "arbitrary"),
        name="fox_decode",
    )(page_table, q, k_new, v_new, cq, *([cache_k] * group), *([cache_v] * group), *([cache_f] * group),
      u_same, u_later)


def kernel(x_prompt, x_sample, mem_prompt, cache_fox_k, cache_fox_v, cache_fox_logf, state_gla, cache_mem_k, cache_mem_v, page_table, g_mix, w_in, b_forget, w_gla_gate, b_gla_gate, g_gla_head, w_out, g_cross, g_mem, w_xq, w_xk, w_xv, w_xo, g_ffn, w_ffn_gate, w_ffn_up, w_ffn_down, g_final):
    depth = w_in.shape[0]
    bsz, seq, d = x_prompt.shape
    sb = x_sample.shape[0]
    mem = mem_prompt.shape[1]
    _, pool, page, fh, fdh = cache_fox_k.shape
    _, _, gh, gdk, gdv = state_gla.shape
    xh, xdh = cache_mem_k.shape[3], cache_mem_k.shape[4]
    fox_dim, gk_dim, gv_dim = fh * fdh, gh * gdk, gh * gdv
    rank = w_gla_gate.shape[1]
    n = bsz * seq
    c_forget = 3 * fox_dim
    c_gla = c_forget + fh
    c_rank = c_gla + 2 * gk_dim + 2 * gv_dim

    xp = x_prompt.reshape(n, d)
    xs = x_sample.reshape(sb, d)
    memx = mem_prompt.reshape(bsz * mem, d)
    ck = cache_fox_k.reshape(depth, pool, page * fh, fdh)
    cv = cache_fox_v.reshape(depth, pool, page * fh, fdh)
    cf = cache_fox_logf.reshape(depth, pool, page * fh // LANES, LANES)
    row = lambda v: v.reshape(1, -1)
    zeros_state = jnp.zeros((bsz, gh, gdk, gdv), F32)
    group = min(DECODE_PAGES_PER_STEP, page_table.shape[1])

    w_t = jnp.swapaxes(w_in, 1, 2)
    w_main, w_small = _w_main(w_t, c_a=c_forget, c_b0=c_gla, c_b1=c_rank, tr=W_PREP_ROWS)
    w_g2 = jnp.concatenate([jnp.zeros((depth, fh, gk_dim), F32), w_gla_gate,
                            jnp.zeros((depth, LANES - fh - rank, gk_dim), F32)], axis=1).astype(BF16)
    b_f = jnp.concatenate([b_forget, jnp.zeros((depth, LANES - fh), F32)], axis=1)
    w_kv = jnp.concatenate([w_xk, w_xv], axis=2)
    w_o = w_out.astype(BF16)

    pk, pv, pf, ps, pmk, pmv, sk, sv, sf, ss = ([] for _ in range(10))
    for l in range(depth):
        last = l == depth - 1

        z, logf, ga, k3, v3, drow = _in_proj(xp, row(g_mix[l]), w_main, w_small, b_f[l:l + 1], w_g2,
                                             row(b_gla_gate[l]), layer=l, n_fox_heads=fh, fox_dh=fdh, seq=seq,
                                             tm=min(ROW_TILE_PROJ, seq), tn=COL_TILE)
        o_f = _fox_prompt(z, drow, batch=bsz, seq=seq, heads=fh, dh=fdh, tq=min(FOX_Q_TILE, seq),
                          hp=FOX_HEADS_PER_STEP)
        o_g, s_fin = _gla(z, ga, zeros_state, row(g_gla_head[l]), batch=bsz, seq=seq, chunk=min(GLA_CHUNK, seq),
                          heads=gh, dk=gdk, dv=gdv, col0=3 * fox_dim)
        xp = _out_proj(o_f, o_g, w_o, xp, layer=l, tm=min(ROW_TILE_OUT, n))
        mkv = _norm_matmul(memx, row(g_mem[l]), w_kv, layer=l, tm=bsz * mem, tn=COL_TILE)
        mk, mv = mkv[:, :xh * xdh], mkv[:, xh * xdh:]
        xp = _cross_attn(xp.reshape(bsz, seq, d), row(g_cross[l]), w_xq, mk.reshape(bsz, mem, -1),
                         mv.reshape(bsz, mem, -1), w_xo, layer=l, heads=xh, dh=xdh,
                         tm=min(ROW_TILE_XATTN, seq)).reshape(n, d)
        pk.append(k3.reshape(bsz, seq, fh, fdh))
        pv.append(v3.reshape(bsz, seq, fh, fdh))
        pf.append(logf.reshape(bsz, seq, fh))
        ps.append(s_fin)
        pmk.append(mk.reshape(bsz, mem, xh, xdh))
        pmv.append(mv.reshape(bsz, mem, xh, xdh))

        zs, logf_s, ga_s, fk, fv = _in_proj(xs, row(g_mix[l]), w_main, w_small, b_f[l:l + 1], w_g2,
                                            row(b_gla_gate[l]), layer=l, n_fox_heads=fh, fox_dh=fdh, seq=0,
                                            tm=sb, tn=COL_TILE)
        fq, fk, fv = zs[:, :fox_dim].reshape(sb, fh, fdh), fk.reshape(sb, fh, fdh), fv.reshape(sb, fh, fdh)
        cq = jnp.tile(logf_s, (1, LANES // fh)).reshape(sb, 1, LANES)
        o_fs = _fox_decode(fq, fk, fv, cq, ck, cv, cf, page_table, layer=l, group=group).reshape(sb, fox_dim)
        cs = GLA_CHUNK_SAMPLE
        pad = lambda a: jnp.zeros((sb, cs, a.shape[1]), a.dtype).at[:, 0].set(a).reshape(sb * cs, -1)
        o_gs, s_new = _gla(pad(zs), pad(ga_s), state_gla[l], row(g_gla_head[l]), batch=sb, seq=cs, chunk=cs,
                           heads=gh, dk=gdk, dv=gdv, col0=3 * fox_dim)
        o_gs = o_gs.reshape(sb, cs, gv_dim)[:, 0]
        xs = _out_proj(o_fs, o_gs, w_o, xs, layer=l, tm=sb)
        xs = _cross_attn(xs.reshape(sb, 1, d), row(g_cross[l]), w_xq, cache_mem_k[l].reshape(sb, mem, -1),
                         cache_mem_v[l].reshape(sb, mem, -1), w_xo, layer=l, heads=xh, dh=xdh, tm=1).reshape(sb, d)
        xp, xs = _swiglu(xp, xs, row(g_ffn[l]), w_ffn_gate, w_ffn_up, w_ffn_down, row(g_final), layer=l,
                         final_norm=last, tm=min(ROW_TILE_FFN, n), tf=FFN_TILE)
        sk.append(fk.reshape(sb, 1, fh, fdh))
        sv.append(fv.reshape(sb, 1, fh, fdh))
        sf.append(logf_s.reshape(sb, 1, fh))
        ss.append(s_new)

    return (xp.reshape(bsz, seq, d), xs.reshape(sb, 1, d),
            jnp.stack(pk), jnp.stack(pv), jnp.stack(pf), jnp.stack(ps), jnp.stack(pmk), jnp.stack(pmv),
            jnp.stack(sk), jnp.stack(sv), jnp.stack(sf), jnp.stack(ss))
```

```python
import functools

import numpy as np
import jax
import jax.numpy as jnp
from jax import lax
from jax.experimental import pallas as pl
from jax.experimental.pallas import tpu as pltpu

F32 = jnp.float32
BF16 = jnp.bfloat16

RMS_EPS = 1e-6
LOG2E = 1.4426950408889634
GLA_TAU = 16.0
GLA_CHUNK = 128
GLA_CHUNK_SAMPLE = 64
LANES = 128
SUBLANES = 8
VMEM_LIMIT_BYTES = 56 * 1024 * 1024
ROW_TILE_PROJ = 1024
ROW_TILE_XATTN = 512
ROW_TILE_FFN = 1024
FFN_TILE = 256
COL_TILE = 512
FOX_Q_TILE = 512
FOX_HEADS_PER_STEP = 2
DECODE_PAGES_PER_STEP = 16
MXU_COLS = 256
W_PREP_ROWS = 256


def _params(*sem):
    return pltpu.CompilerParams(dimension_semantics=sem, vmem_limit_bytes=VMEM_LIMIT_BYTES)


def _rms(x, g):
    return x * lax.rsqrt(jnp.mean(x * x, axis=-1, keepdims=True) + RMS_EPS) * g


def _log_sigmoid(x):
    return jnp.minimum(x, 0.0) - jnp.log1p(jnp.exp(-jnp.abs(x)))


def _silu(x):
    return x / (1.0 + jnp.exp(-x))


def _dot(a, b):
    return jnp.dot(a, b, preferred_element_type=F32)


def _dot_nt(a, b):
    return lax.dot_general(a, b, (((1,), (1,)), ((), ())), preferred_element_type=F32)


def _dot_tn(a, b):
    return lax.dot_general(a, b, (((0,), (0,)), ((), ())), preferred_element_type=F32)


def _split3(x):
    a = x.astype(BF16)
    r = x - a.astype(F32)
    b = r.astype(BF16)
    c = (r - b.astype(F32)).astype(BF16)
    return a, b, c


def _dot3(m, x, dot=_dot):
    a, b, c = _split3(x)
    return dot(m, a) + dot(m, b) + dot(m, c)


def _w_main_kernel(w_ref, wf_ref, wr_ref, o_ref, os_ref):
    o_ref[...] = w_ref[0].astype(BF16)

    @pl.when(pl.program_id(1) == 0)
    def _():
        pad = jnp.zeros((os_ref.shape[0] - wf_ref.shape[1] - wr_ref.shape[1], os_ref.shape[1]), F32)
        os_ref[...] = jnp.concatenate([wf_ref[0], wr_ref[0], pad], axis=0).astype(BF16)


def _w_main(w_t, *, c_a, c_b0, c_b1, tr):
    depth, n_in, d = w_t.shape
    rows = c_a + c_b1 - c_b0

    def src(l, r):
        start = r * tr
        return l, pl.multiple_of(start + jnp.where(start >= c_a, c_b0 - c_a, 0), SUBLANES), 0

    rows_at = lambda n, at: pl.BlockSpec((pl.Element(1), pl.Element(n), pl.Element(d)), lambda l, r: (l, at, 0))
    return pl.pallas_call(
        _w_main_kernel,
        out_shape=[jax.ShapeDtypeStruct((depth, rows, d), BF16), jax.ShapeDtypeStruct((depth, LANES, d), BF16)],
        grid=(depth, rows // tr),
        in_specs=[pl.BlockSpec((pl.Element(1), pl.Element(tr), pl.Element(d)), src),
                  rows_at(c_b0 - c_a, c_a), rows_at(n_in - c_b1, c_b1)],
        out_specs=[pl.BlockSpec((None, tr, d), lambda l, r: (l, r, 0)),
                   pl.BlockSpec((None, LANES, d), lambda l, r: (l, 0, 0))],
        compiler_params=_params("arbitrary", "arbitrary"),
        name="w_main",
    )(w_t, w_t, w_t)


def _in_proj_kernel(x_ref, g_ref, w_ref, ws_ref, bf_ref, wg2_ref, bg_ref, *rest, n_fox_heads, tiles_per_seq):
    if tiles_per_seq:
        z_ref, logf_ref, ga_ref, k3_ref, v3_ref, drow_ref, hb_ref, carry_ref = rest
    else:
        z_ref, logf_ref, ga_ref, k3_ref, v3_ref, hb_ref = rest
    i = pl.program_id(0)
    j = pl.program_id(1)

    @pl.when(j == 0)
    def _():
        hb = _rms(x_ref[...], g_ref[...]).astype(BF16)
        hb_ref[...] = hb
        zs = _dot_nt(hb, ws_ref[...])
        lf = _log_sigmoid(zs + bf_ref[...])
        logf_ref[...] = lf[:, :n_fox_heads]
        gp = _dot(zs.astype(BF16), wg2_ref[...]) + bg_ref[...]
        ga_ref[...] = _log_sigmoid(gp) * (1.0 / GLA_TAU)
        if tiles_per_seq:
            tm = lf.shape[0]
            r = lax.broadcasted_iota(jnp.int32, (LANES, LANES), 0)
            c = lax.broadcasted_iota(jnp.int32, (LANES, LANES), 1)
            tri = jnp.where(r <= c, 1.0, 0.0).astype(BF16)
            lft = lf.T[:SUBLANES]
            run = jnp.where(i % tiles_per_seq == 0, 0.0, carry_ref[...])
            for c0 in range(0, tm, LANES):
                d = _dot3(tri, lft[:, c0:c0 + LANES], dot=lambda m, x: _dot(x, m)) + run
                drow_ref[:, c0:c0 + LANES] = d
                run = d[:, LANES - 1:LANES]
            carry_ref[...] = run

    zt = _dot_nt(hb_ref[...], w_ref[...])
    z_ref[...] = zt
    tm, tn = zt.shape
    dh = k3_ref.shape[1]
    heads_per_tile = tn // dh
    tiles_per_group = n_fox_heads // heads_per_tile
    for group, out_ref in ((1, k3_ref), (2, v3_ref)):
        for t in range(tiles_per_group):
            @pl.when(j == group * tiles_per_group + t)
            def _(out_ref=out_ref, t=t):
                for c in range(heads_per_tile):
                    out_ref[pl.ds(t * heads_per_tile + c, tm, stride=n_fox_heads), :] = zt[:, c * dh:(c + 1) * dh]


def _in_proj(x, g, w_main, w_small, b_f, w_g2, b_g, *, layer, n_fox_heads, fox_dh, seq, tm, tn):
    n, d = x.shape
    nz = w_main.shape[1]
    kg = w_g2.shape[2]
    tiles_per_seq = seq // tm if seq else 0
    out_shape = [jax.ShapeDtypeStruct((n, nz), F32),
                 jax.ShapeDtypeStruct((n, n_fox_heads), F32),
                 jax.ShapeDtypeStruct((n, kg), F32),
                 jax.ShapeDtypeStruct((n * n_fox_heads, fox_dh), F32),
                 jax.ShapeDtypeStruct((n * n_fox_heads, fox_dh), F32)]
    out_specs = [pl.BlockSpec((tm, tn), lambda i, j: (i, j)),
                 pl.BlockSpec((tm, n_fox_heads), lambda i, j: (i, 0)),
                 pl.BlockSpec((tm, kg), lambda i, j: (i, 0)),
                 pl.BlockSpec((tm * n_fox_heads, fox_dh), lambda i, j: (i, 0)),
                 pl.BlockSpec((tm * n_fox_heads, fox_dh), lambda i, j: (i, 0))]
    scratch = [pltpu.VMEM((tm, d), BF16)]
    if tiles_per_seq:
        out_shape.append(jax.ShapeDtypeStruct((n // seq, SUBLANES, seq), F32))
        out_specs.append(pl.BlockSpec((None, SUBLANES, tm), lambda i, j: (i // tiles_per_seq, 0, i % tiles_per_seq)))
        scratch.append(pltpu.VMEM((SUBLANES, 1), F32))
    return pl.pallas_call(
        functools.partial(_in_proj_kernel, n_fox_heads=n_fox_heads, tiles_per_seq=tiles_per_seq),
        out_shape=out_shape,
        grid=(n // tm, nz // tn),
        in_specs=[pl.BlockSpec((tm, d), lambda i, j: (i, 0)),
                  pl.BlockSpec((1, d), lambda i, j: (0, 0)),
                  pl.BlockSpec((None, tn, d), lambda i, j: (layer, j, 0)),
                  pl.BlockSpec((None, LANES, d), lambda i, j: (layer, 0, 0)),
                  pl.BlockSpec((1, LANES), lambda i, j: (0, 0)),
                  pl.BlockSpec((None, LANES, kg), lambda i, j: (layer, 0, 0)),
                  pl.BlockSpec((1, kg), lambda i, j: (0, 0))],
        out_specs=out_specs,
        scratch_shapes=scratch,
        compiler_params=_params("arbitrary", "arbitrary"),
        name="in_proj",
    )(x, g, w_main, w_small, b_f, w_g2, b_g)


def _norm_matmul_kernel(x_ref, g_ref, w_ref, o_ref, hb_ref):
    @pl.when(pl.program_id(1) == 0)
    def _():
        hb_ref[...] = _rms(x_ref[...], g_ref[...]).astype(BF16)

    o_ref[...] = _dot(hb_ref[...], w_ref[...].astype(BF16))


def _norm_matmul(x, g, w, *, layer, tm, tn):
    n, d = x.shape
    nout = w.shape[2]
    return pl.pallas_call(
        _norm_matmul_kernel,
        out_shape=jax.ShapeDtypeStruct((n, nout), F32),
        grid=(n // tm, nout // tn),
        in_specs=[pl.BlockSpec((tm, d), lambda i, j: (i, 0)),
                  pl.BlockSpec((1, d), lambda i, j: (0, 0)),
                  pl.BlockSpec((None, d, tn), lambda i, j: (layer, 0, j))],
        out_specs=pl.BlockSpec((tm, tn), lambda i, j: (i, j)),
        scratch_shapes=[pltpu.VMEM((tm, d), BF16)],
        compiler_params=_params("arbitrary", "arbitrary"),
        name="norm_matmul",
    )(x, g, w)


def _fox_prompt_kernel(q_ref, k_ref, v_ref, d_ref, o_ref, kb_ref, vb_ref, *, tq, dh, scale):
    hg = pl.program_id(1)
    qi = pl.program_id(2)
    hp = q_ref.shape[1] // dh

    @pl.when(qi == 0)
    def _():
        kb_ref[...] = k_ref[...].astype(BF16)
        vb_ref[...] = v_ref[...].astype(BF16)

    qs = [(q_ref[:, i * dh:(i + 1) * dh] * (scale * LOG2E)).astype(BF16) for i in range(hp)]

    def block(j, carry, masked):
        start = pl.multiple_of(j * tq, tq)
        out = []
        for i in range(hp):
            m, l, acc = carry[i]
            kj = kb_ref[pl.ds(start, tq), i * dh:(i + 1) * dh]
            vj = vb_ref[pl.ds(start, tq), i * dh:(i + 1) * dh]
            s = _dot_nt(qs[i], kj) - d_ref[pl.ds(hg * hp + i, 1), pl.ds(start, tq)] * LOG2E
            if masked:
                r = lax.broadcasted_iota(jnp.int32, (tq, tq), 0)
                c = lax.broadcasted_iota(jnp.int32, (tq, tq), 1)
                s = jnp.where(c <= r, s, -jnp.inf)
            m_new = jnp.maximum(m, jnp.max(s, axis=-1, keepdims=True))
            a = jnp.exp2(m - m_new)
            p = jnp.exp2(s - m_new)
            l = a * l + jnp.sum(p, axis=-1, keepdims=True)
            acc = a * acc + _dot(p.astype(BF16), vj)
            out.append((m_new, l, acc))
        return tuple(out)

    init = tuple((jnp.full((tq, 1), -jnp.inf, F32), jnp.zeros((tq, 1), F32), jnp.zeros((tq, dh), F32))
                 for _ in range(hp))
    carry = lax.fori_loop(0, qi, lambda j, c: block(j, c, False), init)
    final = block(qi, carry, True)
    for i, (_, l, acc) in enumerate(final):
        o_ref[:, i * dh:(i + 1) * dh] = (acc / l).astype(o_ref.dtype)


def _fox_prompt(z, drow, *, batch, seq, heads, dh, tq, hp):
    n = z.shape[0]
    nq = seq // tq
    hg = heads // hp
    w = hp * dh
    return pl.pallas_call(
        functools.partial(_fox_prompt_kernel, tq=tq, dh=dh, scale=dh ** -0.5),
        out_shape=jax.ShapeDtypeStruct((n, heads * dh), BF16),
        grid=(batch, hg, nq),
        in_specs=[pl.BlockSpec((tq, w), lambda b, h, i: (b * nq + i, h)),
                  pl.BlockSpec((seq, w), lambda b, h, i: (b, hg + h)),
                  pl.BlockSpec((seq, w), lambda b, h, i: (b, 2 * hg + h)),
                  pl.BlockSpec((None, SUBLANES, seq), lambda b, h, i: (b, 0, 0))],
        out_specs=pl.BlockSpec((tq, w), lambda b, h, i: (b * nq + i, h)),
        scratch_shapes=[pltpu.VMEM((seq, w), BF16), pltpu.VMEM((seq, w), BF16)],
        compiler_params=_params("arbitrary", "arbitrary", "arbitrary"),
        name="fox_prompt",
    )(z, z, z, drow)


def _gla_tables(c):
    r = np.arange(c)
    tri = (r[None, :] <= r[:, None]).astype(np.float32)
    masks = [np.eye(c, dtype=np.float32)]
    m = c // 2
    while m >= 1:
        same = (r[:, None] // (2 * m)) == (r[None, :] // (2 * m))
        upper = (r[:, None] % (2 * m)) >= m
        lower = (r[None, :] % (2 * m)) < m
        masks.append((same & upper & lower).astype(np.float32))
        m //= 2
    return tri, np.concatenate(masks, 0)


def _gla_level_refs(b, b_ref, col0):
    c, dk = b.shape
    bcast = lambda i, rows: jnp.broadcast_to(b_ref[pl.ds(i, 1), pl.ds(col0, dk)], (rows, dk))
    sub = lax.broadcasted_iota(jnp.int32, (c, dk), 0) % SUBLANES
    out = []
    m = c // 2
    while m >= 1:
        if 2 * m >= SUBLANES:
            out.append(jnp.concatenate([bcast(blk * 2 * m + m - 1, 2 * m) for blk in range(c // (2 * m))], axis=0))
        elif m > 1:
            ref = None
            for t in range(SUBLANES // (2 * m)):
                rows = jnp.concatenate([bcast(g * SUBLANES + t * 2 * m + m - 1, SUBLANES)
                                        for g in range(c // SUBLANES)], axis=0)
                ref = rows if ref is None else jnp.where(sub // (2 * m) == t, rows, ref)
            out.append(ref)
        else:
            out.append(jnp.where(sub % 2 == 0, b, pltpu.roll(b, 1, 0)))
        m //= 2
    return out


def _gla_kernel(q_ref, k_ref, v_ref, r_ref, ga_ref, s0_ref, tri_ref, masks_ref, gh_ref,
                og_ref, sfin_ref, st_ref, b_ref, *, heads, dk, dv, scale):
    ci = pl.program_id(1)

    @pl.when(ci == 0)
    def _():
        for h in range(heads):
            st_ref[h] = s0_ref[h].T

    c = q_ref.shape[0]
    b_all = _dot3(tri_ref[...], ga_ref[...])
    b_ref[...] = b_all
    for h in range(heads):
        q = q_ref[:, h * dk:(h + 1) * dk] * scale
        k = k_ref[:, h * dk:(h + 1) * dk]
        v = v_ref[:, h * dv:(h + 1) * dv]
        b = b_all[:, h * dk:(h + 1) * dk]
        a = jnp.where(masks_ref[0:c, :] != 0, jnp.sum(q * k, axis=-1, keepdims=True), 0.0)
        for lv, ref in enumerate(_gla_level_refs(b, b_ref, h * dk)):
            e = jnp.exp(-jnp.abs(b - ref))
            qt = (q * e).astype(BF16)
            kt = (k * e).astype(BF16)
            a = a + jnp.where(masks_ref[(lv + 1) * c:(lv + 2) * c, :] != 0, _dot_nt(qt, kt), 0.0)
        s_old = st_ref[h]
        o = _dot(a.astype(BF16), v.astype(BF16)) + _dot_nt((q * jnp.exp(b)).astype(BF16), s_old.astype(BF16))
        b_last = b[c - 1:c, :]
        upd = _dot(v.T.astype(BF16), (k * jnp.exp(b_last - b)).astype(BF16))
        st_ref[h] = s_old * jnp.exp(b_last) + upd
        y = _rms(o, gh_ref[...])
        og_ref[:, h * dv:(h + 1) * dv] = (y * _silu(r_ref[:, h * dv:(h + 1) * dv])).astype(og_ref.dtype)

    @pl.when(ci == pl.num_programs(1) - 1)
    def _():
        for h in range(heads):
            sfin_ref[h] = st_ref[h].T


def _gla(z, ga, s0, g_head, *, batch, seq, chunk, heads, dk, dv, col0):
    n = z.shape[0]
    c = chunk
    nc = seq // c
    wk, wv = heads * dk, heads * dv
    tri, masks = _gla_tables(c)
    row = lambda b, i: b * nc + i
    return pl.pallas_call(
        functools.partial(_gla_kernel, heads=heads, dk=dk, dv=dv, scale=dk ** -0.5),
        out_shape=[jax.ShapeDtypeStruct((n, wv), BF16), jax.ShapeDtypeStruct((batch, heads, dk, dv), F32)],
        grid=(batch, nc),
        in_specs=[pl.BlockSpec((c, wk), lambda b, i: (row(b, i), col0 // wk)),
                  pl.BlockSpec((c, wk), lambda b, i: (row(b, i), col0 // wk + 1)),
                  pl.BlockSpec((c, wv), lambda b, i: (row(b, i), (col0 + 2 * wk) // wv)),
                  pl.BlockSpec((c, wv), lambda b, i: (row(b, i), (col0 + 2 * wk) // wv + 1)),
                  pl.BlockSpec((c, wk), lambda b, i: (row(b, i), 0)),
                  pl.BlockSpec((None, heads, dk, dv), lambda b, i: (b, 0, 0, 0)),
                  pl.BlockSpec(tri.shape, lambda b, i: (0, 0)),
                  pl.BlockSpec(masks.shape, lambda b, i: (0, 0)),
                  pl.BlockSpec((1, dv), lambda b, i: (0, 0))],
        out_specs=[pl.BlockSpec((c, wv), lambda b, i: (row(b, i), 0)),
                   pl.BlockSpec((None, heads, dk, dv), lambda b, i: (b, 0, 0, 0))],
        scratch_shapes=[pltpu.VMEM((heads, dv, dk), F32), pltpu.VMEM((c, wk), F32)],
        compiler_params=_params("arbitrary", "arbitrary"),
        name="gla",
    )(z, z, z, z, ga, s0, jnp.asarray(tri, BF16), jnp.asarray(masks, F32), g_head)


def _cross_attn_kernel(a_ref, b_ref, w1_ref, w2_ref, x_ref, g_ref, wq_ref, mk_ref, mv_ref, wo_ref, o_ref, *, heads, dh):
    tm, d = x_ref.shape
    pad = lambda t: t if tm >= SUBLANES else jnp.broadcast_to(t, (SUBLANES, t.shape[1]))
    x = pad(x_ref[...]) + _dot(pad(a_ref[...]), w1_ref[...]) + _dot(pad(b_ref[...]), w2_ref[...])
    q = _dot(_rms(x, g_ref[...]).astype(BF16), wq_ref[...].astype(BF16)) * (dh ** -0.5)
    outs = []
    for h in range(heads):
        sl = slice(h * dh, (h + 1) * dh)
        s = _dot_nt(q[:, sl].astype(BF16), mk_ref[:, sl].astype(BF16))
        p = jnp.exp(s - jnp.max(s, axis=-1, keepdims=True))
        p = p / jnp.sum(p, axis=-1, keepdims=True)
        outs.append(_dot(p.astype(BF16), mv_ref[:, sl].astype(BF16)))
    y = x + _dot(jnp.concatenate(outs, axis=1).astype(BF16), wo_ref[...].astype(BF16))
    o_ref[...] = y[:tm]


def _cross_attn(a3, b3, w_o, x3, g, wq, mk, mv, wo, *, layer, heads, dh, tm):
    bsz, rows, d = x3.shape
    kh = a3.shape[2]
    mem = mk.shape[1]
    xd = heads * dh
    once = pl.Buffered(1)
    return pl.pallas_call(
        functools.partial(_cross_attn_kernel, heads=heads, dh=dh),
        out_shape=jax.ShapeDtypeStruct(x3.shape, F32),
        grid=(bsz, rows // tm),
        in_specs=[pl.BlockSpec((None, tm, kh), lambda b, i: (b, i, 0)),
                  pl.BlockSpec((None, tm, kh), lambda b, i: (b, i, 0)),
                  pl.BlockSpec((None, kh, d), lambda b, i: (layer, 0, 0), pipeline_mode=once),
                  pl.BlockSpec((None, kh, d), lambda b, i: (layer, 1, 0), pipeline_mode=once),
                  pl.BlockSpec((None, tm, d), lambda b, i: (b, i, 0)),
                  pl.BlockSpec((1, d), lambda b, i: (0, 0)),
                  pl.BlockSpec((None, d, xd), lambda b, i: (layer, 0, 0), pipeline_mode=once),
                  pl.BlockSpec((None, mem, xd), lambda b, i: (b, 0, 0)),
                  pl.BlockSpec((None, mem, xd), lambda b, i: (b, 0, 0)),
                  pl.BlockSpec((None, xd, d), lambda b, i: (layer, 0, 0), pipeline_mode=once)],
        out_specs=pl.BlockSpec((None, tm, d), lambda b, i: (b, i, 0)),
        compiler_params=_params("arbitrary", "arbitrary"),
        name="cross_attn",
    )(a3, b3, w_o, w_o, x3, g, wq, mk, mv, wo)


def _swiglu_kernel(x_ref, xs_ref, g_ref, wg_ref, wu_ref, wd_ref, gf_ref, o_ref, os_ref, hb_ref, hs_ref, *, final_norm):
    i = pl.program_id(0)
    f = pl.program_id(1)
    last = pl.num_programs(1) - 1

    @pl.when(f == 0)
    def _():
        x = x_ref[...]
        hb_ref[...] = _rms(x, g_ref[...]).astype(BF16)
        o_ref[...] = x

    wg, wu, wd = wg_ref[...].astype(BF16), wu_ref[...].astype(BF16), wd_ref[...].astype(BF16)

    def ffn(hb):
        return _dot((_silu(_dot(hb, wg)) * _dot(hb, wu)).astype(BF16), wd)

    o_ref[...] += ffn(hb_ref[...])

    @pl.when(f == last)
    def _():
        if final_norm:
            o_ref[...] = _rms(o_ref[...], gf_ref[...])

    @pl.when(i == 0)
    def _():
        @pl.when(f == 0)
        def _():
            xs = xs_ref[...]
            hs_ref[...] = _rms(xs, g_ref[...]).astype(BF16)
            os_ref[...] = xs

        os_ref[...] += ffn(hs_ref[...])

        @pl.when(f == last)
        def _():
            if final_norm:
                os_ref[...] = _rms(os_ref[...], gf_ref[...])


def _swiglu(x, xs, g, wg, wu, wd, g_final, *, layer, final_norm, tm, tf):
    n, d = x.shape
    ns = xs.shape[0]
    ffn = wg.shape[2]
    return pl.pallas_call(
        functools.partial(_swiglu_kernel, final_norm=final_norm),
        out_shape=[jax.ShapeDtypeStruct((n, d), F32), jax.ShapeDtypeStruct((ns, d), F32)],
        grid=(n // tm, ffn // tf),
        in_specs=[pl.BlockSpec((tm, d), lambda i, f: (i, 0)),
                  pl.BlockSpec((ns, d), lambda i, f: (0, 0)),
                  pl.BlockSpec((1, d), lambda i, f: (0, 0)),
                  pl.BlockSpec((None, d, tf), lambda i, f: (layer, 0, f)),
                  pl.BlockSpec((None, d, tf), lambda i, f: (layer, 0, f)),
                  pl.BlockSpec((None, tf, d), lambda i, f: (layer, f, 0)),
                  pl.BlockSpec((1, d), lambda i, f: (0, 0))],
        out_specs=[pl.BlockSpec((tm, d), lambda i, f: (i, 0)),
                   pl.BlockSpec((ns, d), lambda i, f: (0, 0))],
        scratch_shapes=[pltpu.VMEM((tm, d), BF16), pltpu.VMEM((ns, d), BF16)],
        compiler_params=_params("arbitrary", "arbitrary"),
        name="swiglu",
    )(x, xs, g, wg, wu, wd, g_final)


def _suffix_rows_exclusive(x):
    n = x.shape[0]
    row = lax.broadcasted_iota(jnp.int32, x.shape, 0)

    def up(a, k):
        return jnp.where(row + k < n, pltpu.roll(a, n - k, 0), 0.0)

    y = up(x, 1)
    k = 1
    while k < n:
        y = y + up(y, k)
        k *= 2
    return y


def _fox_decode_kernel(pt_ref, q_ref, kn_ref, vn_ref, cq_ref, *rest, heads, group, scale):
    del pt_ref
    k_refs, v_refs, f_refs = rest[:group], rest[group:2 * group], rest[2 * group:3 * group]
    u_same_ref, u_later_ref, o_ref, m_ref, l_ref, acc_ref, tail_ref = rest[3 * group:]
    step = pl.program_id(1)

    @pl.when(step == 0)
    def _():
        m_ref[...] = jnp.full_like(m_ref, -jnp.inf)
        l_ref[...] = jnp.zeros_like(l_ref)
        acc_ref[...] = jnp.zeros_like(acc_ref)
        tail_ref[...] = jnp.zeros_like(tail_ref)

    n_blk = k_refs[0].shape[0] // LANES
    wide = MXU_COLS // LANES
    q = q_ref[...]
    qb = q.astype(BF16)
    sub = lax.broadcasted_iota(jnp.int32, (heads, MXU_COLS), 0)
    lane = lax.broadcasted_iota(jnp.int32, (heads, MXU_COLS), 1)
    own = sub == lane % heads
    parts = _split3(jnp.concatenate([f[...] for f in f_refs], axis=0))
    rows_later = sum(_dot(x, u_later_ref[...]) for x in parts)
    rows_total = sum(_dot(x, u_same_ref[...]) for x in parts)
    tail = tail_ref[...]
    tiles = []
    for g in range(group):
        row_total = rows_total[g * n_blk:(g + 1) * n_blk]
        bias = cq_ref[...] + tail + rows_later[g * n_blk:(g + 1) * n_blk] + _suffix_rows_exclusive(row_total)
        tail = tail + jnp.sum(row_total, axis=0, keepdims=True)
        for r in range(0, n_blk, wide):
            kb = k_refs[g][r * LANES:(r + wide) * LANES, :].astype(BF16)
            brow = jnp.concatenate([bias[r + w:r + w + 1, :] for w in range(wide)], axis=1)
            s = jnp.where(own, _dot_nt(qb, kb) * scale + brow, -jnp.inf)
            tiles.extend(s[:, w * LANES:(w + 1) * LANES] for w in range(wide))
    tail_ref[...] = tail

    top = tiles[0]
    for t in tiles[1:]:
        top = jnp.maximum(top, t)
    m_old = m_ref[...]
    m_new = jnp.maximum(m_old, jnp.max(top, axis=-1, keepdims=True))
    a = jnp.exp(m_old - m_new)
    accs = [a * acc_ref[...], jnp.zeros(acc_ref.shape, F32)]
    psum = jnp.zeros((heads, LANES), F32)
    dh = acc_ref.shape[1]
    for n, idx in enumerate(range(0, len(tiles), wide)):
        g, r = divmod(idx, n_blk)
        prs = [jnp.exp(tiles[idx + w] - m_new) for w in range(wide)]
        for pr in prs:
            psum = psum + pr
        vb = jnp.concatenate([v_refs[g][(r + w) * LANES:(r + w + 1) * LANES, :].astype(BF16) for w in range(wide)],
                             axis=1)
        res = _dot(jnp.concatenate(prs, axis=0).astype(BF16), vb)
        for w in range(wide):
            accs[n % 2] = accs[n % 2] + res[w * heads:(w + 1) * heads, w * dh:(w + 1) * dh]
    m_ref[...] = m_new
    l_ref[...] = a * l_ref[...] + jnp.sum(psum, axis=-1, keepdims=True)
    acc_ref[...] = accs[0] + accs[1]

    @pl.when(step == pl.num_programs(1) - 1)
    def _():
        s_new = jnp.sum(q * kn_ref[...], axis=-1, keepdims=True) * scale
        m_fin = jnp.maximum(m_ref[...], s_new)
        a_fin = jnp.exp(m_ref[...] - m_fin)
        p_new = jnp.exp(s_new - m_fin)
        o_ref[...] = ((a_fin * acc_ref[...] + p_new * vn_ref[...]) / (a_fin * l_ref[...] + p_new)).astype(o_ref.dtype)


def _fox_decode(q, k_new, v_new, cq, cache_k, cache_v, cache_f, page_table, *, layer, group):
    bsz, heads, dh = q.shape
    rows = cache_k.shape[2]
    n_pages = page_table.shape[1]
    lane = np.arange(LANES)
    same = lane[:, None] % heads == lane[None, :] % heads
    u_same = jnp.asarray(same, BF16)
    u_later = jnp.asarray(same & (lane[:, None] > lane[None, :]), BF16)
    vec = pl.BlockSpec((None, heads, dh), lambda b, s, pt: (b, 0, 0))

    def paged(shape, g):
        return pl.BlockSpec((None, None) + shape, lambda b, s, pt: (layer, pt[b, n_pages - 1 - (s * group + g)], 0, 0))

    const = pl.BlockSpec((LANES, LANES), lambda b, s, pt: (0, 0))
    grid_spec = pltpu.PrefetchScalarGridSpec(
        num_scalar_prefetch=1,
        grid=(bsz, n_pages // group),
        in_specs=([vec, vec, vec, pl.BlockSpec((None, 1, LANES), lambda b, s, pt: (b, 0, 0))]
                  + [paged((rows, dh), g) for g in range(group)]
                  + [paged((rows, dh), g) for g in range(group)]
                  + [paged((rows // LANES, LANES), g) for g in range(group)]
                  + [const, const]),
        out_specs=vec,
        scratch_shapes=[pltpu.VMEM((heads, 1), F32), pltpu.VMEM((heads, 1), F32),
                        pltpu.VMEM((heads, dh), F32), pltpu.VMEM((1, LANES), F32)],
    )
    return pl.pallas_call(
        functools.partial(_fox_decode_kernel, heads=heads, group=group, scale=dh ** -0.5),
        out_shape=jax.ShapeDtypeStruct((bsz, heads, dh), BF16),
        grid_spec=grid_spec,
        compiler_params=_params("arbitrary", "arbitrary"),
        name="fox_decode",
    )(page_table, q, k_new, v_new, cq, *([cache_k] * group), *([cache_v] * group), *([cache_f] * group),
      u_same, u_later)


def kernel(x_prompt, x_sample, mem_prompt, cache_fox_k, cache_fox_v, cache_fox_logf, state_gla, cache_mem_k, cache_mem_v, page_table, g_mix, w_in, b_forget, w_gla_gate, b_gla_gate, g_gla_head, w_out, g_cross, g_mem, w_xq, w_xk, w_xv, w_xo, g_ffn, w_ffn_gate, w_ffn_up, w_ffn_down, g_final):
    depth = w_in.shape[0]
    bsz, seq, d = x_prompt.shape
    sb = x_sample.shape[0]
    mem = mem_prompt.shape[1]
    _, pool, page, fh, fdh = cache_fox_k.shape
    _, _, gh, gdk, gdv = state_gla.shape
    xh, xdh = cache_mem_k.shape[3], cache_mem_k.shape[4]
    fox_dim, gk_dim, gv_dim = fh * fdh, gh * gdk, gh * gdv
    rank = w_gla_gate.shape[1]
    n = bsz * seq
    c_forget = 3 * fox_dim
    c_gla = c_forget + fh
    c_rank = c_gla + 2 * gk_dim + 2 * gv_dim

    xp = x_prompt.reshape(n, d)
    xs = x_sample.reshape(sb, d)
    memx = mem_prompt.reshape(bsz * mem, d)
    ck = cache_fox_k.reshape(depth, pool, page * fh, fdh)
    cv = cache_fox_v.reshape(depth, pool, page * fh, fdh)
    cf = cache_fox_logf.reshape(depth, pool, page * fh // LANES, LANES)
    row = lambda v: v.reshape(1, -1)
    zeros_state = jnp.zeros((bsz, gh, gdk, gdv), F32)
    group = min(DECODE_PAGES_PER_STEP, page_table.shape[1])

    w_t = jnp.swapaxes(w_in, 1, 2)
    w_main, w_small = _w_main(w_t, c_a=c_forget, c_b0=c_gla, c_b1=c_rank, tr=W_PREP_ROWS)
    w_g2 = jnp.concatenate([jnp.zeros((depth, fh, gk_dim), F32), w_gla_gate,
                            jnp.zeros((depth, LANES - fh - rank, gk_dim), F32)], axis=1).astype(BF16)
    b_f = jnp.concatenate([b_forget, jnp.zeros((depth, LANES - fh), F32)], axis=1)
    w_kv = jnp.concatenate([w_xk, w_xv], axis=2)
    w_o = w_out.astype(BF16)

    pk, pv, pf, ps, pmk, pmv, sk, sv, sf, ss = ([] for _ in range(10))
    for l in range(depth):
        last = l == depth - 1

        z, logf, ga, k3, v3, drow = _in_proj(xp, row(g_mix[l]), w_main, w_small, b_f[l:l + 1], w_g2,
                                             row(b_gla_gate[l]), layer=l, n_fox_heads=fh, fox_dh=fdh, seq=seq,
                                             tm=min(ROW_TILE_PROJ, seq), tn=COL_TILE)
        o_f = _fox_prompt(z, drow, batch=bsz, seq=seq, heads=fh, dh=fdh, tq=min(FOX_Q_TILE, seq),
                          hp=FOX_HEADS_PER_STEP)
        o_g, s_fin = _gla(z, ga, zeros_state, row(g_gla_head[l]), batch=bsz, seq=seq, chunk=min(GLA_CHUNK, seq),
                          heads=gh, dk=gdk, dv=gdv, col0=3 * fox_dim)
        mkv = _norm_matmul(memx, row(g_mem[l]), w_kv, layer=l, tm=bsz * mem, tn=COL_TILE)
        mk, mv = mkv[:, :xh * xdh], mkv[:, xh * xdh:]
        xp = _cross_attn(o_f.reshape(bsz, seq, -1), o_g.reshape(bsz, seq, -1), w_o, xp.reshape(bsz, seq, d),
                         row(g_cross[l]), w_xq, mk.reshape(bsz, mem, -1), mv.reshape(bsz, mem, -1), w_xo, layer=l,
                         heads=xh, dh=xdh, tm=min(ROW_TILE_XATTN, seq)).reshape(n, d)
        pk.append(k3.reshape(bsz, seq, fh, fdh))
        pv.append(v3.reshape(bsz, seq, fh, fdh))
        pf.append(logf.reshape(bsz, seq, fh))
        ps.append(s_fin)
        pmk.append(mk.reshape(bsz, mem, xh, xdh))
        pmv.append(mv.reshape(bsz, mem, xh, xdh))

        zs, logf_s, ga_s, fk, fv = _in_proj(xs, row(g_mix[l]), w_main, w_small, b_f[l:l + 1], w_g2,
                                            row(b_gla_gate[l]), layer=l, n_fox_heads=fh, fox_dh=fdh, seq=0,
                                            tm=sb, tn=COL_TILE)
        fq, fk, fv = zs[:, :fox_dim].reshape(sb, fh, fdh), fk.reshape(sb, fh, fdh), fv.reshape(sb, fh, fdh)
        cq = jnp.tile(logf_s, (1, LANES // fh)).reshape(sb, 1, LANES)
        o_fs = _fox_decode(fq, fk, fv, cq, ck, cv, cf, page_table, layer=l, group=group).reshape(sb, fox_dim)
        cs = GLA_CHUNK_SAMPLE
        pad = lambda a: jnp.zeros((sb, cs, a.shape[1]), a.dtype).at[:, 0].set(a).reshape(sb * cs, -1)
        o_gs, s_new = _gla(pad(zs), pad(ga_s), state_gla[l], row(g_gla_head[l]), batch=sb, seq=cs, chunk=cs,
                           heads=gh, dk=gdk, dv=gdv, col0=3 * fox_dim)
        o_gs = o_gs.reshape(sb, cs, gv_dim)[:, 0]
        xs = _cross_attn(o_fs.reshape(sb, 1, -1), o_gs.reshape(sb, 1, -1), w_o, xs.reshape(sb, 1, d),
                         row(g_cross[l]), w_xq, cache_mem_k[l].reshape(sb, mem, -1),
                         cache_mem_v[l].reshape(sb, mem, -1), w_xo, layer=l, heads=xh, dh=xdh, tm=1).reshape(sb, d)
        xp, xs = _swiglu(xp, xs, row(g_ffn[l]), w_ffn_gate, w_ffn_up, w_ffn_down, row(g_final), layer=l,
                         final_norm=last, tm=min(ROW_TILE_FFN, n), tf=FFN_TILE)
        sk.append(fk.reshape(sb, 1, fh, fdh))
        sv.append(fv.reshape(sb, 1, fh, fdh))
        sf.append(logf_s.reshape(sb, 1, fh))
        ss.append(s_new)

    return (xp.reshape(bsz, seq, d), xs.reshape(sb, 1, d),
            jnp.stack(pk), jnp.stack(pv), jnp.stack(pf), jnp.stack(ps), jnp.stack(pmk), jnp.stack(pmv),
            jnp.stack(sk), jnp.stack(sv), jnp.stack(sf), jnp.stack(ss))
```

```python
import functools

import numpy as np
import jax
import jax.numpy as jnp
from jax import lax
from jax.experimental import pallas as pl
from jax.experimental.pallas import tpu as pltpu

F32 = jnp.float32
BF16 = jnp.bfloat16

RMS_EPS = 1e-6
LOG2E = 1.4426950408889634
GLA_TAU = 16.0
GLA_CHUNK = 128
GLA_CHUNK_SAMPLE = 64
LANES = 128
SUBLANES = 8
VMEM_LIMIT_BYTES = 56 * 1024 * 1024
ROW_TILE_PROJ = 1024
ROW_TILE_XATTN = 512
ROW_TILE_FFN = 1024
FFN_TILE = 256
COL_TILE = 512
FOX_Q_TILE = 512
FOX_HEADS_PER_STEP = 2
DECODE_PAGES_PER_STEP = 16
MXU_COLS = 256
W_PREP_ROWS = 256


def _params(*sem):
    return pltpu.CompilerParams(dimension_semantics=sem, vmem_limit_bytes=VMEM_LIMIT_BYTES)


def _rms(x, g):
    return x * lax.rsqrt(jnp.mean(x * x, axis=-1, keepdims=True) + RMS_EPS) * g


def _log_sigmoid(x):
    return jnp.minimum(x, 0.0) - jnp.log1p(jnp.exp(-jnp.abs(x)))


def _silu(x):
    return x / (1.0 + jnp.exp(-x))


def _dot(a, b):
    return jnp.dot(a, b, preferred_element_type=F32)


def _dot_nt(a, b):
    return lax.dot_general(a, b, (((1,), (1,)), ((), ())), preferred_element_type=F32)


def _dot_tn(a, b):
    return lax.dot_general(a, b, (((0,), (0,)), ((), ())), preferred_element_type=F32)


def _split3(x):
    a = x.astype(BF16)
    r = x - a.astype(F32)
    b = r.astype(BF16)
    c = (r - b.astype(F32)).astype(BF16)
    return a, b, c


def _dot3(m, x, dot=_dot):
    a, b, c = _split3(x)
    return dot(m, a) + dot(m, b) + dot(m, c)


def _w_main_kernel(w_ref, wf_ref, wr_ref, o_ref, os_ref):
    o_ref[...] = w_ref[0].astype(BF16)

    @pl.when(pl.program_id(1) == 0)
    def _():
        pad = jnp.zeros((os_ref.shape[0] - wf_ref.shape[1] - wr_ref.shape[1], os_ref.shape[1]), F32)
        os_ref[...] = jnp.concatenate([wf_ref[0], wr_ref[0], pad], axis=0).astype(BF16)


def _w_main(w_t, *, c_a, c_b0, c_b1, tr):
    depth, n_in, d = w_t.shape
    rows = c_a + c_b1 - c_b0

    def src(l, r):
        start = r * tr
        return l, pl.multiple_of(start + jnp.where(start >= c_a, c_b0 - c_a, 0), SUBLANES), 0

    rows_at = lambda n, at: pl.BlockSpec((pl.Element(1), pl.Element(n), pl.Element(d)), lambda l, r: (l, at, 0))
    return pl.pallas_call(
        _w_main_kernel,
        out_shape=[jax.ShapeDtypeStruct((depth, rows, d), BF16), jax.ShapeDtypeStruct((depth, LANES, d), BF16)],
        grid=(depth, rows // tr),
        in_specs=[pl.BlockSpec((pl.Element(1), pl.Element(tr), pl.Element(d)), src),
                  rows_at(c_b0 - c_a, c_a), rows_at(n_in - c_b1, c_b1)],
        out_specs=[pl.BlockSpec((None, tr, d), lambda l, r: (l, r, 0)),
                   pl.BlockSpec((None, LANES, d), lambda l, r: (l, 0, 0))],
        compiler_params=_params("arbitrary", "arbitrary"),
        name="w_main",
    )(w_t, w_t, w_t)


def _in_proj_kernel(x_ref, g_ref, w_ref, ws_ref, bf_ref, wg2_ref, bg_ref, *rest, n_fox_heads, tiles_per_seq):
    if tiles_per_seq:
        z_ref, logf_ref, ga_ref, k3_ref, v3_ref, drow_ref, hb_ref, carry_ref = rest
    else:
        z_ref, logf_ref, ga_ref, k3_ref, v3_ref, hb_ref = rest
    i = pl.program_id(0)
    j = pl.program_id(1)

    @pl.when(j == 0)
    def _():
        hb = _rms(x_ref[...], g_ref[...]).astype(BF16)
        hb_ref[...] = hb
        zs = _dot_nt(hb, ws_ref[...])
        lf = _log_sigmoid(zs + bf_ref[...])
        logf_ref[...] = lf[:, :n_fox_heads]
        gp = _dot(zs.astype(BF16), wg2_ref[...]) + bg_ref[...]
        ga_ref[...] = _log_sigmoid(gp) * (1.0 / GLA_TAU)
        if tiles_per_seq:
            tm = lf.shape[0]
            r = lax.broadcasted_iota(jnp.int32, (LANES, LANES), 0)
            c = lax.broadcasted_iota(jnp.int32, (LANES, LANES), 1)
            tri = jnp.where(r <= c, 1.0, 0.0).astype(BF16)
            lft = lf.T[:SUBLANES]
            run = jnp.where(i % tiles_per_seq == 0, 0.0, carry_ref[...])
            for c0 in range(0, tm, LANES):
                d = _dot3(tri, lft[:, c0:c0 + LANES], dot=lambda m, x: _dot(x, m)) + run
                drow_ref[:, c0:c0 + LANES] = d
                run = d[:, LANES - 1:LANES]
            carry_ref[...] = run

    zt = _dot_nt(hb_ref[...], w_ref[...])
    z_ref[...] = zt
    tm, tn = zt.shape
    dh = k3_ref.shape[1]
    heads_per_tile = tn // dh
    tiles_per_group = n_fox_heads // heads_per_tile
    for group, out_ref in ((1, k3_ref), (2, v3_ref)):
        for t in range(tiles_per_group):
            @pl.when(j == group * tiles_per_group + t)
            def _(out_ref=out_ref, t=t):
                for c in range(heads_per_tile):
                    out_ref[pl.ds(t * heads_per_tile + c, tm, stride=n_fox_heads), :] = zt[:, c * dh:(c + 1) * dh]


def _in_proj(x, g, w_main, w_small, b_f, w_g2, b_g, *, layer, n_fox_heads, fox_dh, seq, tm, tn):
    n, d = x.shape
    nz = w_main.shape[1]
    kg = w_g2.shape[2]
    tiles_per_seq = seq // tm if seq else 0
    out_shape = [jax.ShapeDtypeStruct((n, nz), F32),
                 jax.ShapeDtypeStruct((n, n_fox_heads), F32),
                 jax.ShapeDtypeStruct((n, kg), F32),
                 jax.ShapeDtypeStruct((n * n_fox_heads, fox_dh), F32),
                 jax.ShapeDtypeStruct((n * n_fox_heads, fox_dh), F32)]
    last = n // tm - 1
    group_tiles = n_fox_heads * fox_dh // tn
    tile = lambda i, j, done: jnp.where(j >= done, jnp.minimum(i + 1, last), i)
    out_specs = [pl.BlockSpec((tm, tn), lambda i, j: (i, j)),
                 pl.BlockSpec((tm, n_fox_heads), lambda i, j: (tile(i, j, 1), 0)),
                 pl.BlockSpec((tm, kg), lambda i, j: (tile(i, j, 1), 0)),
                 pl.BlockSpec((tm * n_fox_heads, fox_dh), lambda i, j: (tile(i, j, 2 * group_tiles), 0)),
                 pl.BlockSpec((tm * n_fox_heads, fox_dh), lambda i, j: (tile(i, j, 3 * group_tiles), 0))]
    scratch = [pltpu.VMEM((tm, d), BF16)]
    if tiles_per_seq:
        out_shape.append(jax.ShapeDtypeStruct((n // seq, SUBLANES, seq), F32))
        out_specs.append(pl.BlockSpec((None, SUBLANES, tm), lambda i, j: (tile(i, j, 1) // tiles_per_seq, 0,
                                                                          tile(i, j, 1) % tiles_per_seq)))
        scratch.append(pltpu.VMEM((SUBLANES, 1), F32))
    return pl.pallas_call(
        functools.partial(_in_proj_kernel, n_fox_heads=n_fox_heads, tiles_per_seq=tiles_per_seq),
        out_shape=out_shape,
        grid=(n // tm, nz // tn),
        in_specs=[pl.BlockSpec((tm, d), lambda i, j: (i, 0)),
                  pl.BlockSpec((1, d), lambda i, j: (0, 0)),
                  pl.BlockSpec((None, tn, d), lambda i, j: (layer, j, 0)),
                  pl.BlockSpec((None, LANES, d), lambda i, j: (layer, 0, 0)),
                  pl.BlockSpec((1, LANES), lambda i, j: (0, 0)),
                  pl.BlockSpec((None, LANES, kg), lambda i, j: (layer, 0, 0)),
                  pl.BlockSpec((1, kg), lambda i, j: (0, 0))],
        out_specs=out_specs,
        scratch_shapes=scratch,
        compiler_params=_params("arbitrary", "arbitrary"),
        name="in_proj",
    )(x, g, w_main, w_small, b_f, w_g2, b_g)


def _norm_matmul_kernel(x_ref, g_ref, w_ref, o_ref, hb_ref):
    @pl.when(pl.program_id(1) == 0)
    def _():
        hb_ref[...] = _rms(x_ref[...], g_ref[...]).astype(BF16)

    o_ref[...] = _dot(hb_ref[...], w_ref[...].astype(BF16))


def _norm_matmul(x, g, w, *, layer, tm, tn):
    n, d = x.shape
    nout = w.shape[2]
    return pl.pallas_call(
        _norm_matmul_kernel,
        out_shape=jax.ShapeDtypeStruct((n, nout), F32),
        grid=(n // tm, nout // tn),
        in_specs=[pl.BlockSpec((tm, d), lambda i, j: (i, 0)),
                  pl.BlockSpec((1, d), lambda i, j: (0, 0)),
                  pl.BlockSpec((None, d, tn), lambda i, j: (layer, 0, j))],
        out_specs=pl.BlockSpec((tm, tn), lambda i, j: (i, j)),
        scratch_shapes=[pltpu.VMEM((tm, d), BF16)],
        compiler_params=_params("arbitrary", "arbitrary"),
        name="norm_matmul",
    )(x, g, w)


def _fox_prompt_kernel(q_ref, k_ref, v_ref, d_ref, o_ref, kb_ref, vb_ref, *, tq, dh, scale):
    hg = pl.program_id(1)
    qi = pl.program_id(2)
    hp = q_ref.shape[1] // dh

    @pl.when(qi == 0)
    def _():
        kb_ref[...] = k_ref[...].astype(BF16)
        vb_ref[...] = v_ref[...].astype(BF16)

    qs = [(q_ref[:, i * dh:(i + 1) * dh] * (scale * LOG2E)).astype(BF16) for i in range(hp)]

    def block(j, carry, masked):
        start = pl.multiple_of(j * tq, tq)
        out = []
        for i in range(hp):
            m, l, acc = carry[i]
            kj = kb_ref[pl.ds(start, tq), i * dh:(i + 1) * dh]
            vj = vb_ref[pl.ds(start, tq), i * dh:(i + 1) * dh]
            s = _dot_nt(qs[i], kj) - d_ref[pl.ds(hg * hp + i, 1), pl.ds(start, tq)] * LOG2E
            if masked:
                r = lax.broadcasted_iota(jnp.int32, (tq, tq), 0)
                c = lax.broadcasted_iota(jnp.int32, (tq, tq), 1)
                s = jnp.where(c <= r, s, -jnp.inf)
            m_new = jnp.maximum(m, jnp.max(s, axis=-1, keepdims=True))
            a = jnp.exp2(m - m_new)
            p = jnp.exp2(s - m_new)
            l = a * l + jnp.sum(p, axis=-1, keepdims=True)
            acc = a * acc + _dot(p.astype(BF16), vj)
            out.append((m_new, l, acc))
        return tuple(out)

    init = tuple((jnp.full((tq, 1), -jnp.inf, F32), jnp.zeros((tq, 1), F32), jnp.zeros((tq, dh), F32))
                 for _ in range(hp))
    carry = lax.fori_loop(0, qi, lambda j, c: block(j, c, False), init)
    final = block(qi, carry, True)
    for i, (_, l, acc) in enumerate(final):
        o_ref[:, i * dh:(i + 1) * dh] = (acc / l).astype(o_ref.dtype)


def _fox_prompt(z, drow, *, batch, seq, heads, dh, tq, hp):
    n = z.shape[0]
    nq = seq // tq
    hg = heads // hp
    w = hp * dh
    return pl.pallas_call(
        functools.partial(_fox_prompt_kernel, tq=tq, dh=dh, scale=dh ** -0.5),
        out_shape=jax.ShapeDtypeStruct((n, heads * dh), BF16),
        grid=(batch, hg, nq),
        in_specs=[pl.BlockSpec((tq, w), lambda b, h, i: (b * nq + i, h)),
                  pl.BlockSpec((seq, w), lambda b, h, i: (b, hg + h)),
                  pl.BlockSpec((seq, w), lambda b, h, i: (b, 2 * hg + h)),
                  pl.BlockSpec((None, SUBLANES, seq), lambda b, h, i: (b, 0, 0))],
        out_specs=pl.BlockSpec((tq, w), lambda b, h, i: (b * nq + i, h)),
        scratch_shapes=[pltpu.VMEM((seq, w), BF16), pltpu.VMEM((seq, w), BF16)],
        compiler_params=_params("arbitrary", "arbitrary", "arbitrary"),
        name="fox_prompt",
    )(z, z, z, drow)


def _gla_tables(c):
    r = np.arange(c)
    tri = (r[None, :] <= r[:, None]).astype(np.float32)
    masks = [np.eye(c, dtype=np.float32)]
    m = c // 2
    while m >= 1:
        same = (r[:, None] // (2 * m)) == (r[None, :] // (2 * m))
        upper = (r[:, None] % (2 * m)) >= m
        lower = (r[None, :] % (2 * m)) < m
        masks.append((same & upper & lower).astype(np.float32))
        m //= 2
    return tri, np.concatenate(masks, 0)


def _gla_level_refs(b, b_ref, col0):
    c, dk = b.shape
    bcast = lambda i, rows: jnp.broadcast_to(b_ref[pl.ds(i, 1), pl.ds(col0, dk)], (rows, dk))
    sub = lax.broadcasted_iota(jnp.int32, (c, dk), 0) % SUBLANES
    out = []
    m = c // 2
    while m >= 1:
        if 2 * m >= SUBLANES:
            out.append(jnp.concatenate([bcast(blk * 2 * m + m - 1, 2 * m) for blk in range(c // (2 * m))], axis=0))
        elif m > 1:
            ref = None
            for t in range(SUBLANES // (2 * m)):
                rows = jnp.concatenate([bcast(g * SUBLANES + t * 2 * m + m - 1, SUBLANES)
                                        for g in range(c // SUBLANES)], axis=0)
                ref = rows if ref is None else jnp.where(sub // (2 * m) == t, rows, ref)
            out.append(ref)
        else:
            out.append(jnp.where(sub % 2 == 0, b, pltpu.roll(b, 1, 0)))
        m //= 2
    return out


def _gla_kernel(q_ref, k_ref, v_ref, r_ref, ga_ref, s0_ref, tri_ref, masks_ref, gh_ref,
                og_ref, sfin_ref, st_ref, b_ref, *, heads, dk, dv, scale):
    ci = pl.program_id(1)

    @pl.when(ci == 0)
    def _():
        for h in range(heads):
            st_ref[h] = s0_ref[h].T

    c = q_ref.shape[0]
    b_all = _dot3(tri_ref[...], ga_ref[...])
    b_ref[...] = b_all
    for h in range(heads):
        q = q_ref[:, h * dk:(h + 1) * dk] * scale
        k = k_ref[:, h * dk:(h + 1) * dk]
        v = v_ref[:, h * dv:(h + 1) * dv]
        b = b_all[:, h * dk:(h + 1) * dk]
        a = jnp.where(masks_ref[0:c, :] != 0, jnp.sum(q * k, axis=-1, keepdims=True), 0.0)
        for lv, ref in enumerate(_gla_level_refs(b, b_ref, h * dk)):
            e = jnp.exp(-jnp.abs(b - ref))
            qt = (q * e).astype(BF16)
            kt = (k * e).astype(BF16)
            a = a + jnp.where(masks_ref[(lv + 1) * c:(lv + 2) * c, :] != 0, _dot_nt(qt, kt), 0.0)
        s_old = st_ref[h]
        o = _dot(a.astype(BF16), v.astype(BF16)) + _dot_nt((q * jnp.exp(b)).astype(BF16), s_old.astype(BF16))
        b_last = b[c - 1:c, :]
        upd = _dot(v.T.astype(BF16), (k * jnp.exp(b_last - b)).astype(BF16))
        st_ref[h] = s_old * jnp.exp(b_last) + upd
        y = _rms(o, gh_ref[...])
        og_ref[:, h * dv:(h + 1) * dv] = (y * _silu(r_ref[:, h * dv:(h + 1) * dv])).astype(og_ref.dtype)

    @pl.when(ci == pl.num_programs(1) - 1)
    def _():
        for h in range(heads):
            sfin_ref[h] = st_ref[h].T


def _gla(z, ga, s0, g_head, *, batch, seq, chunk, heads, dk, dv, col0):
    n = z.shape[0]
    c = chunk
    nc = seq // c
    wk, wv = heads * dk, heads * dv
    tri, masks = _gla_tables(c)
    row = lambda b, i: b * nc + i
    return pl.pallas_call(
        functools.partial(_gla_kernel, heads=heads, dk=dk, dv=dv, scale=dk ** -0.5),
        out_shape=[jax.ShapeDtypeStruct((n, wv), BF16), jax.ShapeDtypeStruct((batch, heads, dk, dv), F32)],
        grid=(batch, nc),
        in_specs=[pl.BlockSpec((c, wk), lambda b, i: (row(b, i), col0 // wk)),
                  pl.BlockSpec((c, wk), lambda b, i: (row(b, i), col0 // wk + 1)),
                  pl.BlockSpec((c, wv), lambda b, i: (row(b, i), (col0 + 2 * wk) // wv)),
                  pl.BlockSpec((c, wv), lambda b, i: (row(b, i), (col0 + 2 * wk) // wv + 1)),
                  pl.BlockSpec((c, wk), lambda b, i: (row(b, i), 0)),
                  pl.BlockSpec((None, heads, dk, dv), lambda b, i: (b, 0, 0, 0)),
                  pl.BlockSpec(tri.shape, lambda b, i: (0, 0)),
                  pl.BlockSpec(masks.shape, lambda b, i: (0, 0)),
                  pl.BlockSpec((1, dv), lambda b, i: (0, 0))],
        out_specs=[pl.BlockSpec((c, wv), lambda b, i: (row(b, i), 0)),
                   pl.BlockSpec((None, heads, dk, dv), lambda b, i: (b, 0, 0, 0))],
        scratch_shapes=[pltpu.VMEM((heads, dv, dk), F32), pltpu.VMEM((c, wk), F32)],
        compiler_params=_params("arbitrary", "arbitrary"),
        name="gla",
    )(z, z, z, z, ga, s0, jnp.asarray(tri, BF16), jnp.asarray(masks, F32), g_head)


def _cross_attn_kernel(a_ref, b_ref, w1_ref, w2_ref, x_ref, g_ref, wq_ref, mk_ref, mv_ref, wo_ref, o_ref, *, heads, dh):
    x = x_ref[...] + _dot(a_ref[...], w1_ref[...]) + _dot(b_ref[...], w2_ref[...])
    q = _dot(_rms(x, g_ref[...]).astype(BF16), wq_ref[...].astype(BF16)) * (dh ** -0.5)
    outs = []
    for h in range(heads):
        sl = slice(h * dh, (h + 1) * dh)
        s = _dot_nt(q[:, sl].astype(BF16), mk_ref[:, sl].astype(BF16))
        p = jnp.exp(s - jnp.max(s, axis=-1, keepdims=True))
        p = p / jnp.sum(p, axis=-1, keepdims=True)
        outs.append(_dot(p.astype(BF16), mv_ref[:, sl].astype(BF16)))
    o_ref[...] = x + _dot(jnp.concatenate(outs, axis=1).astype(BF16), wo_ref[...].astype(BF16))


def _cross_attn(a3, b3, w_o, x3, g, wq, mk, mv, wo, *, layer, heads, dh, tm):
    bsz, rows, d = x3.shape
    kh = a3.shape[2]
    mem = mk.shape[1]
    xd = heads * dh
    once = pl.Buffered(1)
    return pl.pallas_call(
        functools.partial(_cross_attn_kernel, heads=heads, dh=dh),
        out_shape=jax.ShapeDtypeStruct(x3.shape, F32),
        grid=(bsz, rows // tm),
        in_specs=[pl.BlockSpec((None, tm, kh), lambda b, i: (b, i, 0)),
                  pl.BlockSpec((None, tm, kh), lambda b, i: (b, i, 0)),
                  pl.BlockSpec((None, kh, d), lambda b, i: (layer, 0, 0), pipeline_mode=once),
                  pl.BlockSpec((None, kh, d), lambda b, i: (layer, 1, 0), pipeline_mode=once),
                  pl.BlockSpec((None, tm, d), lambda b, i: (b, i, 0)),
                  pl.BlockSpec((1, d), lambda b, i: (0, 0)),
                  pl.BlockSpec((None, d, xd), lambda b, i: (layer, 0, 0), pipeline_mode=once),
                  pl.BlockSpec((None, mem, xd), lambda b, i: (b, 0, 0)),
                  pl.BlockSpec((None, mem, xd), lambda b, i: (b, 0, 0)),
                  pl.BlockSpec((None, xd, d), lambda b, i: (layer, 0, 0), pipeline_mode=once)],
        out_specs=pl.BlockSpec((None, tm, d), lambda b, i: (b, i, 0)),
        compiler_params=_params("arbitrary", "arbitrary"),
        name="cross_attn",
    )(a3, b3, w_o, w_o, x3, g, wq, mk, mv, wo)


def _cross_attn_rows_kernel(a_ref, b_ref, w1_ref, w2_ref, x_ref, g_ref, wq_ref, mk_ref, mv_ref, wo_ref, o_ref, *,
                            heads, dh):
    x = x_ref[...] + _dot(a_ref[...], w1_ref[...]) + _dot(b_ref[...], w2_ref[...])
    n_rows = x.shape[0]
    q = _dot(_rms(x, g_ref[...]).astype(BF16), wq_ref[...].astype(BF16)) * (dh ** -0.5)
    row = lax.broadcasted_iota(jnp.int32, (n_rows, dh), 0)
    outs = []
    for h in range(heads):
        sl = slice(h * dh, (h + 1) * dh)
        qh = q[:, sl].astype(BF16)
        oh = jnp.zeros((n_rows, dh), F32)
        for r in range(n_rows):
            s = _dot_nt(qh, mk_ref[r, :, sl].astype(BF16))
            p = jnp.exp(s - jnp.max(s, axis=-1, keepdims=True))
            p = p / jnp.sum(p, axis=-1, keepdims=True)
            oh = jnp.where(row == r, _dot(p.astype(BF16), mv_ref[r, :, sl].astype(BF16)), oh)
        outs.append(oh)
    o_ref[...] = x + _dot(jnp.concatenate(outs, axis=1).astype(BF16), wo_ref[...].astype(BF16))


def _cross_attn_rows(a, b, w_o, x, g, wq, mk, mv, wo, *, layer, heads, dh):
    n_rows, d = x.shape
    kh = a.shape[1]
    mem = mk.shape[1]
    xd = heads * dh
    whole = lambda shape: pl.BlockSpec(shape, lambda i: (0,) * len(shape))
    return pl.pallas_call(
        functools.partial(_cross_attn_rows_kernel, heads=heads, dh=dh),
        out_shape=jax.ShapeDtypeStruct(x.shape, F32),
        grid=(1,),
        in_specs=[whole((n_rows, kh)), whole((n_rows, kh)),
                  pl.BlockSpec((None, kh, d), lambda i: (layer, 0, 0)),
                  pl.BlockSpec((None, kh, d), lambda i: (layer, 1, 0)),
                  whole((n_rows, d)), whole((1, d)),
                  pl.BlockSpec((None, d, xd), lambda i: (layer, 0, 0)),
                  whole((n_rows, mem, xd)), whole((n_rows, mem, xd)),
                  pl.BlockSpec((None, xd, d), lambda i: (layer, 0, 0))],
        out_specs=whole((n_rows, d)),
        compiler_params=_params("arbitrary"),
        name="cross_attn_rows",
    )(a, b, w_o, w_o, x, g, wq, mk, mv, wo)


def _swiglu_kernel(x_ref, xs_ref, g_ref, wg_ref, wu_ref, wd_ref, gf_ref, o_ref, os_ref, hb_ref, hs_ref, *, final_norm):
    i = pl.program_id(0)
    f = pl.program_id(1)
    last = pl.num_programs(1) - 1

    @pl.when(f == 0)
    def _():
        x = x_ref[...]
        hb_ref[...] = _rms(x, g_ref[...]).astype(BF16)
        o_ref[...] = x

    wg, wu, wd = wg_ref[...].astype(BF16), wu_ref[...].astype(BF16), wd_ref[...].astype(BF16)

    def ffn(hb):
        return _dot((_silu(_dot(hb, wg)) * _dot(hb, wu)).astype(BF16), wd)

    o_ref[...] += ffn(hb_ref[...])

    @pl.when(f == last)
    def _():
        if final_norm:
            o_ref[...] = _rms(o_ref[...], gf_ref[...])

    @pl.when(i == 0)
    def _():
        @pl.when(f == 0)
        def _():
            xs = xs_ref[...]
            hs_ref[...] = _rms(xs, g_ref[...]).astype(BF16)
            os_ref[...] = xs

        os_ref[...] += ffn(hs_ref[...])

        @pl.when(f == last)
        def _():
            if final_norm:
                os_ref[...] = _rms(os_ref[...], gf_ref[...])


def _swiglu(x, xs, g, wg, wu, wd, g_final, *, layer, final_norm, tm, tf):
    n, d = x.shape
    ns = xs.shape[0]
    ffn = wg.shape[2]
    return pl.pallas_call(
        functools.partial(_swiglu_kernel, final_norm=final_norm),
        out_shape=[jax.ShapeDtypeStruct((n, d), F32), jax.ShapeDtypeStruct((ns, d), F32)],
        grid=(n // tm, ffn // tf),
        in_specs=[pl.BlockSpec((tm, d), lambda i, f: (i, 0)),
                  pl.BlockSpec((ns, d), lambda i, f: (0, 0)),
                  pl.BlockSpec((1, d), lambda i, f: (0, 0)),
                  pl.BlockSpec((None, d, tf), lambda i, f: (layer, 0, f)),
                  pl.BlockSpec((None, d, tf), lambda i, f: (layer, 0, f)),
                  pl.BlockSpec((None, tf, d), lambda i, f: (layer, f, 0)),
                  pl.BlockSpec((1, d), lambda i, f: (0, 0))],
        out_specs=[pl.BlockSpec((tm, d), lambda i, f: (i, 0)),
                   pl.BlockSpec((ns, d), lambda i, f: (0, 0))],
        scratch_shapes=[pltpu.VMEM((tm, d), BF16), pltpu.VMEM((ns, d), BF16)],
        compiler_params=_params("arbitrary", "arbitrary"),
        name="swiglu",
    )(x, xs, g, wg, wu, wd, g_final)


def _suffix_rows_exclusive(x):
    n = x.shape[0]
    row = lax.broadcasted_iota(jnp.int32, x.shape, 0)

    def up(a, k):
        return jnp.where(row + k < n, pltpu.roll(a, n - k, 0), 0.0)

    y = up(x, 1)
    k = 1
    while k < n:
        y = y + up(y, k)
        k *= 2
    return y


def _fox_decode_kernel(pt_ref, q_ref, kn_ref, vn_ref, cq_ref, *rest, heads, group, scale):
    del pt_ref
    k_refs, v_refs, f_refs = rest[:group], rest[group:2 * group], rest[2 * group:3 * group]
    u_same_ref, u_later_ref, o_ref, m_ref, l_ref, acc_ref, tail_ref = rest[3 * group:]
    step = pl.program_id(1)

    @pl.when(step == 0)
    def _():
        m_ref[...] = jnp.full_like(m_ref, -jnp.inf)
        l_ref[...] = jnp.zeros_like(l_ref)
        acc_ref[...] = jnp.zeros_like(acc_ref)
        tail_ref[...] = jnp.zeros_like(tail_ref)

    n_blk = k_refs[0].shape[0] // LANES
    wide = MXU_COLS // LANES
    q = q_ref[...]
    qb = q.astype(BF16)
    sub = lax.broadcasted_iota(jnp.int32, (heads, MXU_COLS), 0)
    lane = lax.broadcasted_iota(jnp.int32, (heads, MXU_COLS), 1)
    own = sub == lane % heads
    parts = _split3(jnp.concatenate([f[...] for f in f_refs], axis=0))
    rows_later = sum(_dot(x, u_later_ref[...]) for x in parts)
    rows_total = sum(_dot(x, u_same_ref[...]) for x in parts)
    tail = tail_ref[...]
    tiles = []
    for g in range(group):
        row_total = rows_total[g * n_blk:(g + 1) * n_blk]
        bias = cq_ref[...] + tail + rows_later[g * n_blk:(g + 1) * n_blk] + _suffix_rows_exclusive(row_total)
        tail = tail + jnp.sum(row_total, axis=0, keepdims=True)
        for r in range(0, n_blk, wide):
            kb = k_refs[g][r * LANES:(r + wide) * LANES, :].astype(BF16)
            brow = jnp.concatenate([bias[r + w:r + w + 1, :] for w in range(wide)], axis=1)
            s = jnp.where(own, _dot_nt(qb, kb) * scale + brow, -jnp.inf)
            tiles.extend(s[:, w * LANES:(w + 1) * LANES] for w in range(wide))
    tail_ref[...] = tail

    top = tiles[0]
    for t in tiles[1:]:
        top = jnp.maximum(top, t)
    m_old = m_ref[...]
    m_new = jnp.maximum(m_old, jnp.max(top, axis=-1, keepdims=True))
    a = jnp.exp(m_old - m_new)
    accs = [a * acc_ref[...], jnp.zeros(acc_ref.shape, F32)]
    psum = jnp.zeros((heads, LANES), F32)
    dh = acc_ref.shape[1]
    for n, idx in enumerate(range(0, len(tiles), wide)):
        g, r = divmod(idx, n_blk)
        prs = [jnp.exp(tiles[idx + w] - m_new) for w in range(wide)]
        for pr in prs:
            psum = psum + pr
        vb = jnp.concatenate([v_refs[g][(r + w) * LANES:(r + w + 1) * LANES, :].astype(BF16) for w in range(wide)],
                             axis=1)
        res = _dot(jnp.concatenate(prs, axis=0).astype(BF16), vb)
        for w in range(wide):
            accs[n % 2] = accs[n % 2] + res[w * heads:(w + 1) * heads, w * dh:(w + 1) * dh]
    m_ref[...] = m_new
    l_ref[...] = a * l_ref[...] + jnp.sum(psum, axis=-1, keepdims=True)
    acc_ref[...] = accs[0] + accs[1]

    @pl.when(step == pl.num_programs(1) - 1)
    def _():
        s_new = jnp.sum(q * kn_ref[...], axis=-1, keepdims=True) * scale
        m_fin = jnp.maximum(m_ref[...], s_new)
        a_fin = jnp.exp(m_ref[...] - m_fin)
        p_new = jnp.exp(s_new - m_fin)
        o_ref[...] = ((a_fin * acc_ref[...] + p_new * vn_ref[...]) / (a_fin * l_ref[...] + p_new)).astype(o_ref.dtype)


def _fox_decode(q, k_new, v_new, cq, cache_k, cache_v, cache_f, page_table, *, layer, group):
    bsz, heads, dh = q.shape
    rows = cache_k.shape[2]
    n_pages = page_table.shape[1]
    lane = np.arange(LANES)
    same = lane[:, None] % heads == lane[None, :] % heads
    u_same = jnp.asarray(same, BF16)
    u_later = jnp.asarray(same & (lane[:, None] > lane[None, :]), BF16)
    vec = pl.BlockSpec((None, heads, dh), lambda b, s, pt: (b, 0, 0))

    def paged(shape, g):
        return pl.BlockSpec((None, None) + shape, lambda b, s, pt: (layer, pt[b, n_pages - 1 - (s * group + g)], 0, 0))

    const = pl.BlockSpec((LANES, LANES), lambda b, s, pt: (0, 0))
    grid_spec = pltpu.PrefetchScalarGridSpec(
        num_scalar_prefetch=1,
        grid=(bsz, n_pages // group),
        in_specs=([vec, vec, vec, pl.BlockSpec((None, 1, LANES), lambda b, s, pt: (b, 0, 0))]
                  + [paged((rows, dh), g) for g in range(group)]
                  + [paged((rows, dh), g) for g in range(group)]
                  + [paged((rows // LANES, LANES), g) for g in range(group)]
                  + [const, const]),
        out_specs=vec,
        scratch_shapes=[pltpu.VMEM((heads, 1), F32), pltpu.VMEM((heads, 1), F32),
                        pltpu.VMEM((heads, dh), F32), pltpu.VMEM((1, LANES), F32)],
    )
    return pl.pallas_call(
        functools.partial(_fox_decode_kernel, heads=heads, group=group, scale=dh ** -0.5),
        out_shape=jax.ShapeDtypeStruct((bsz, heads, dh), BF16),
        grid_spec=grid_spec,
        compiler_params=_params("arbitrary", "arbitrary"),
        name="fox_decode",
    )(page_table, q, k_new, v_new, cq, *([cache_k] * group), *([cache_v] * group), *([cache_f] * group),
      u_same, u_later)


def kernel(x_prompt, x_sample, mem_prompt, cache_fox_k, cache_fox_v, cache_fox_logf, state_gla, cache_mem_k, cache_mem_v, page_table, g_mix, w_in, b_forget, w_gla_gate, b_gla_gate, g_gla_head, w_out, g_cross, g_mem, w_xq, w_xk, w_xv, w_xo, g_ffn, w_ffn_gate, w_ffn_up, w_ffn_down, g_final):
    depth = w_in.shape[0]
    bsz, seq, d = x_prompt.shape
    sb = x_sample.shape[0]
    mem = mem_prompt.shape[1]
    _, pool, page, fh, fdh = cache_fox_k.shape
    _, _, gh, gdk, gdv = state_gla.shape
    xh, xdh = cache_mem_k.shape[3], cache_mem_k.shape[4]
    fox_dim, gk_dim, gv_dim = fh * fdh, gh * gdk, gh * gdv
    rank = w_gla_gate.shape[1]
    n = bsz * seq
    c_forget = 3 * fox_dim
    c_gla = c_forget + fh
    c_rank = c_gla + 2 * gk_dim + 2 * gv_dim

    xp = x_prompt.reshape(n, d)
    xs = x_sample.reshape(sb, d)
    memx = mem_prompt.reshape(bsz * mem, d)
    ck = cache_fox_k.reshape(depth, pool, page * fh, fdh)
    cv = cache_fox_v.reshape(depth, pool, page * fh, fdh)
    cf = cache_fox_logf.reshape(depth, pool, page * fh // LANES, LANES)
    row = lambda v: v.reshape(1, -1)
    zeros_state = jnp.zeros((bsz, gh, gdk, gdv), F32)
    group = min(DECODE_PAGES_PER_STEP, page_table.shape[1])

    w_t = jnp.swapaxes(w_in, 1, 2)
    w_main, w_small = _w_main(w_t, c_a=c_forget, c_b0=c_gla, c_b1=c_rank, tr=W_PREP_ROWS)
    w_g2 = jnp.concatenate([jnp.zeros((depth, fh, gk_dim), F32), w_gla_gate,
                            jnp.zeros((depth, LANES - fh - rank, gk_dim), F32)], axis=1).astype(BF16)
    b_f = jnp.concatenate([b_forget, jnp.zeros((depth, LANES - fh), F32)], axis=1)
    w_kv = jnp.concatenate([w_xk, w_xv], axis=2)
    w_o = w_out.astype(BF16)

    pk, pv, pf, ps, pmk, pmv, sk, sv, sf, ss = ([] for _ in range(10))
    for l in range(depth):
        last = l == depth - 1

        z, logf, ga, k3, v3, drow = _in_proj(xp, row(g_mix[l]), w_main, w_small, b_f[l:l + 1], w_g2,
                                             row(b_gla_gate[l]), layer=l, n_fox_heads=fh, fox_dh=fdh, seq=seq,
                                             tm=min(ROW_TILE_PROJ, seq), tn=COL_TILE)
        o_f = _fox_prompt(z, drow, batch=bsz, seq=seq, heads=fh, dh=fdh, tq=min(FOX_Q_TILE, seq),
                          hp=FOX_HEADS_PER_STEP)
        o_g, s_fin = _gla(z, ga, zeros_state, row(g_gla_head[l]), batch=bsz, seq=seq, chunk=min(GLA_CHUNK, seq),
                          heads=gh, dk=gdk, dv=gdv, col0=3 * fox_dim)
        mkv = _norm_matmul(memx, row(g_mem[l]), w_kv, layer=l, tm=bsz * mem, tn=COL_TILE)
        mk, mv = mkv[:, :xh * xdh], mkv[:, xh * xdh:]
        xp = _cross_attn(o_f.reshape(bsz, seq, -1), o_g.reshape(bsz, seq, -1), w_o, xp.reshape(bsz, seq, d),
                         row(g_cross[l]), w_xq, mk.reshape(bsz, mem, -1), mv.reshape(bsz, mem, -1), w_xo, layer=l,
                         heads=xh, dh=xdh, tm=min(ROW_TILE_XATTN, seq)).reshape(n, d)
        pk.append(k3.reshape(bsz, seq, fh, fdh))
        pv.append(v3.reshape(bsz, seq, fh, fdh))
        pf.append(logf.reshape(bsz, seq, fh))
        ps.append(s_fin)
        pmk.append(mk.reshape(bsz, mem, xh, xdh))
        pmv.append(mv.reshape(bsz, mem, xh, xdh))

        zs, logf_s, ga_s, fk, fv = _in_proj(xs, row(g_mix[l]), w_main, w_small, b_f[l:l + 1], w_g2,
                                            row(b_gla_gate[l]), layer=l, n_fox_heads=fh, fox_dh=fdh, seq=0,
                                            tm=sb, tn=COL_TILE)
        fq, fk, fv = zs[:, :fox_dim].reshape(sb, fh, fdh), fk.reshape(sb, fh, fdh), fv.reshape(sb, fh, fdh)
        cq = jnp.tile(logf_s, (1, LANES // fh)).reshape(sb, 1, LANES)
        o_fs = _fox_decode(fq, fk, fv, cq, ck, cv, cf, page_table, layer=l, group=group).reshape(sb, fox_dim)
        cs = GLA_CHUNK_SAMPLE
        pad = lambda a: jnp.zeros((sb, cs, a.shape[1]), a.dtype).at[:, 0].set(a).reshape(sb * cs, -1)
        o_gs, s_new = _gla(pad(zs), pad(ga_s), state_gla[l], row(g_gla_head[l]), batch=sb, seq=cs, chunk=cs,
                           heads=gh, dk=gdk, dv=gdv, col0=3 * fox_dim)
        o_gs = o_gs.reshape(sb, cs, gv_dim)[:, 0]
        xs = _cross_attn_rows(o_fs, o_gs, w_o, xs, row(g_cross[l]), w_xq, cache_mem_k[l].reshape(sb, mem, -1),
                              cache_mem_v[l].reshape(sb, mem, -1), w_xo, layer=l, heads=xh, dh=xdh)
        xp, xs = _swiglu(xp, xs, row(g_ffn[l]), w_ffn_gate, w_ffn_up, w_ffn_down, row(g_final), layer=l,
                         final_norm=last, tm=min(ROW_TILE_FFN, n), tf=FFN_TILE)
        sk.append(fk.reshape(sb, 1, fh, fdh))
        sv.append(fv.reshape(sb, 1, fh, fdh))
        sf.append(logf_s.reshape(sb, 1, fh))
        ss.append(s_new)

    return (xp.reshape(bsz, seq, d), xs.reshape(sb, 1, d),
            jnp.stack(pk), jnp.stack(pv), jnp.stack(pf), jnp.stack(ps), jnp.stack(pmk), jnp.stack(pmv),
            jnp.stack(sk), jnp.stack(sv), jnp.stack(sf), jnp.stack(ss))
```

```python
import functools

import numpy as np
import jax
import jax.numpy as jnp
from jax import lax
from jax.experimental import pallas as pl
from jax.experimental.pallas import tpu as pltpu

F32 = jnp.float32
BF16 = jnp.bfloat16

RMS_EPS = 1e-6
LOG2E = 1.4426950408889634
GLA_TAU = 16.0
GLA_CHUNK = 128
GLA_CHUNK_SAMPLE = 64
LANES = 128
SUBLANES = 8
VMEM_LIMIT_BYTES = 56 * 1024 * 1024
ROW_TILE_PROJ = 1024
ROW_TILE_XATTN = 512
ROW_TILE_FFN = 1024
FFN_TILE = 256
COL_TILE = 512
FOX_Q_TILE = 512
FOX_HEADS_PER_STEP = 4
DECODE_PAGES_PER_STEP = 16
MXU_COLS = 256
W_PREP_ROWS = 256


def _params(*sem):
    return pltpu.CompilerParams(dimension_semantics=sem, vmem_limit_bytes=VMEM_LIMIT_BYTES)


def _rms(x, g):
    return x * lax.rsqrt(jnp.mean(x * x, axis=-1, keepdims=True) + RMS_EPS) * g


def _log_sigmoid(x):
    return jnp.minimum(x, 0.0) - jnp.log1p(jnp.exp(-jnp.abs(x)))


def _silu(x):
    return x / (1.0 + jnp.exp(-x))


def _dot(a, b):
    return jnp.dot(a, b, preferred_element_type=F32)


def _dot_nt(a, b):
    return lax.dot_general(a, b, (((1,), (1,)), ((), ())), preferred_element_type=F32)


def _dot_tn(a, b):
    return lax.dot_general(a, b, (((0,), (0,)), ((), ())), preferred_element_type=F32)


def _split3(x):
    a = x.astype(BF16)
    r = x - a.astype(F32)
    b = r.astype(BF16)
    c = (r - b.astype(F32)).astype(BF16)
    return a, b, c


def _dot3(m, x, dot=_dot):
    a, b, c = _split3(x)
    return dot(m, a) + dot(m, b) + dot(m, c)


def _w_main_kernel(w_ref, wf_ref, wr_ref, o_ref, os_ref):
    o_ref[...] = w_ref[0].astype(BF16)

    @pl.when(pl.program_id(1) == 0)
    def _():
        pad = jnp.zeros((os_ref.shape[0] - wf_ref.shape[1] - wr_ref.shape[1], os_ref.shape[1]), F32)
        os_ref[...] = jnp.concatenate([wf_ref[0], wr_ref[0], pad], axis=0).astype(BF16)


def _w_main(w_t, *, c_a, c_b0, c_b1, tr):
    depth, n_in, d = w_t.shape
    rows = c_a + c_b1 - c_b0

    def src(l, r):
        start = r * tr
        return l, pl.multiple_of(start + jnp.where(start >= c_a, c_b0 - c_a, 0), SUBLANES), 0

    rows_at = lambda n, at: pl.BlockSpec((pl.Element(1), pl.Element(n), pl.Element(d)), lambda l, r: (l, at, 0))
    return pl.pallas_call(
        _w_main_kernel,
        out_shape=[jax.ShapeDtypeStruct((depth, rows, d), BF16), jax.ShapeDtypeStruct((depth, LANES, d), BF16)],
        grid=(depth, rows // tr),
        in_specs=[pl.BlockSpec((pl.Element(1), pl.Element(tr), pl.Element(d)), src),
                  rows_at(c_b0 - c_a, c_a), rows_at(n_in - c_b1, c_b1)],
        out_specs=[pl.BlockSpec((None, tr, d), lambda l, r: (l, r, 0)),
                   pl.BlockSpec((None, LANES, d), lambda l, r: (l, 0, 0))],
        compiler_params=_params("arbitrary", "arbitrary"),
        name="w_main",
    )(w_t, w_t, w_t)


def _in_proj_kernel(x_ref, g_ref, w_ref, ws_ref, bf_ref, wg2_ref, bg_ref, *rest, n_fox_heads, tiles_per_seq):
    if tiles_per_seq:
        z_ref, logf_ref, ga_ref, k3_ref, v3_ref, drow_ref, hb_ref, carry_ref = rest
    else:
        z_ref, logf_ref, ga_ref, k3_ref, v3_ref, hb_ref = rest
    i = pl.program_id(0)
    j = pl.program_id(1)

    @pl.when(j == 0)
    def _():
        hb = _rms(x_ref[...], g_ref[...]).astype(BF16)
        hb_ref[...] = hb
        zs = _dot_nt(hb, ws_ref[...])
        lf = _log_sigmoid(zs + bf_ref[...])
        logf_ref[...] = lf[:, :n_fox_heads]
        gp = _dot(zs.astype(BF16), wg2_ref[...]) + bg_ref[...]
        ga_ref[...] = _log_sigmoid(gp) * (1.0 / GLA_TAU)
        if tiles_per_seq:
            tm = lf.shape[0]
            r = lax.broadcasted_iota(jnp.int32, (LANES, LANES), 0)
            c = lax.broadcasted_iota(jnp.int32, (LANES, LANES), 1)
            tri = jnp.where(r <= c, 1.0, 0.0).astype(BF16)
            lft = lf.T[:SUBLANES]
            run = jnp.where(i % tiles_per_seq == 0, 0.0, carry_ref[...])
            for c0 in range(0, tm, LANES):
                d = _dot3(tri, lft[:, c0:c0 + LANES], dot=lambda m, x: _dot(x, m)) + run
                drow_ref[:, c0:c0 + LANES] = d
                run = d[:, LANES - 1:LANES]
            carry_ref[...] = run

    zt = _dot_nt(hb_ref[...], w_ref[...])
    z_ref[...] = zt
    tm, tn = zt.shape
    dh = k3_ref.shape[1]
    heads_per_tile = tn // dh
    tiles_per_group = n_fox_heads // heads_per_tile
    for group, out_ref in ((1, k3_ref), (2, v3_ref)):
        for t in range(tiles_per_group):
            @pl.when(j == group * tiles_per_group + t)
            def _(out_ref=out_ref, t=t):
                for c in range(heads_per_tile):
                    out_ref[pl.ds(t * heads_per_tile + c, tm, stride=n_fox_heads), :] = zt[:, c * dh:(c + 1) * dh]


def _in_proj(x, g, w_main, w_small, b_f, w_g2, b_g, *, layer, n_fox_heads, fox_dh, seq, tm, tn):
    n, d = x.shape
    nz = w_main.shape[1]
    kg = w_g2.shape[2]
    tiles_per_seq = seq // tm if seq else 0
    out_shape = [jax.ShapeDtypeStruct((n, nz), F32),
                 jax.ShapeDtypeStruct((n, n_fox_heads), F32),
                 jax.ShapeDtypeStruct((n, kg), F32),
                 jax.ShapeDtypeStruct((n * n_fox_heads, fox_dh), F32),
                 jax.ShapeDtypeStruct((n * n_fox_heads, fox_dh), F32)]
    out_specs = [pl.BlockSpec((tm, tn), lambda i, j: (i, j)),
                 pl.BlockSpec((tm, n_fox_heads), lambda i, j: (i, 0)),
                 pl.BlockSpec((tm, kg), lambda i, j: (i, 0)),
                 pl.BlockSpec((tm * n_fox_heads, fox_dh), lambda i, j: (i, 0)),
                 pl.BlockSpec((tm * n_fox_heads, fox_dh), lambda i, j: (i, 0))]
    scratch = [pltpu.VMEM((tm, d), BF16)]
    if tiles_per_seq:
        out_shape.append(jax.ShapeDtypeStruct((n // seq, SUBLANES, seq), F32))
        out_specs.append(pl.BlockSpec((None, SUBLANES, tm), lambda i, j: (i // tiles_per_seq, 0, i % tiles_per_seq)))
        scratch.append(pltpu.VMEM((SUBLANES, 1), F32))
    return pl.pallas_call(
        functools.partial(_in_proj_kernel, n_fox_heads=n_fox_heads, tiles_per_seq=tiles_per_seq),
        out_shape=out_shape,
        grid=(n // tm, nz // tn),
        in_specs=[pl.BlockSpec((tm, d), lambda i, j: (i, 0)),
                  pl.BlockSpec((1, d), lambda i, j: (0, 0)),
                  pl.BlockSpec((None, tn, d), lambda i, j: (layer, j, 0)),
                  pl.BlockSpec((None, LANES, d), lambda i, j: (layer, 0, 0)),
                  pl.BlockSpec((1, LANES), lambda i, j: (0, 0)),
                  pl.BlockSpec((None, LANES, kg), lambda i, j: (layer, 0, 0)),
                  pl.BlockSpec((1, kg), lambda i, j: (0, 0))],
        out_specs=out_specs,
        scratch_shapes=scratch,
        compiler_params=_params("arbitrary", "arbitrary"),
        name="in_proj",
    )(x, g, w_main, w_small, b_f, w_g2, b_g)


def _norm_matmul_kernel(x_ref, g_ref, w_ref, o_ref, hb_ref):
    @pl.when(pl.program_id(1) == 0)
    def _():
        hb_ref[...] = _rms(x_ref[...], g_ref[...]).astype(BF16)

    o_ref[...] = _dot(hb_ref[...], w_ref[...].astype(BF16))


def _norm_matmul(x, g, w, *, layer, tm, tn):
    n, d = x.shape
    nout = w.shape[2]
    return pl.pallas_call(
        _norm_matmul_kernel,
        out_shape=jax.ShapeDtypeStruct((n, nout), F32),
        grid=(n // tm, nout // tn),
        in_specs=[pl.BlockSpec((tm, d), lambda i, j: (i, 0)),
                  pl.BlockSpec((1, d), lambda i, j: (0, 0)),
                  pl.BlockSpec((None, d, tn), lambda i, j: (layer, 0, j))],
        out_specs=pl.BlockSpec((tm, tn), lambda i, j: (i, j)),
        scratch_shapes=[pltpu.VMEM((tm, d), BF16)],
        compiler_params=_params("arbitrary", "arbitrary"),
        name="norm_matmul",
    )(x, g, w)


def _fox_prompt_kernel(q_ref, k_ref, v_ref, d_ref, o_ref, kb_ref, vb_ref, *, tq, dh, scale):
    hg = pl.program_id(1)
    qi = pl.program_id(2)
    hp = q_ref.shape[1] // dh

    @pl.when(qi == 0)
    def _():
        kb_ref[...] = k_ref[...].astype(BF16)
        vb_ref[...] = v_ref[...].astype(BF16)

    qs = [(q_ref[:, i * dh:(i + 1) * dh] * (scale * LOG2E)).astype(BF16) for i in range(hp)]

    def block(j, carry, masked):
        start = pl.multiple_of(j * tq, tq)
        out = []
        for i in range(hp):
            m, l, acc = carry[i]
            kj = kb_ref[pl.ds(start, tq), i * dh:(i + 1) * dh]
            vj = vb_ref[pl.ds(start, tq), i * dh:(i + 1) * dh]
            s = _dot_nt(qs[i], kj) - d_ref[pl.ds(hg * hp + i, 1), pl.ds(start, tq)] * LOG2E
            if masked:
                r = lax.broadcasted_iota(jnp.int32, (tq, tq), 0)
                c = lax.broadcasted_iota(jnp.int32, (tq, tq), 1)
                s = jnp.where(c <= r, s, -jnp.inf)
            m_new = jnp.maximum(m, jnp.max(s, axis=-1, keepdims=True))
            a = jnp.exp2(m - m_new)
            p = jnp.exp2(s - m_new)
            l = a * l + jnp.sum(p, axis=-1, keepdims=True)
            acc = a * acc + _dot(p.astype(BF16), vj)
            out.append((m_new, l, acc))
        return tuple(out)

    init = tuple((jnp.full((tq, 1), -jnp.inf, F32), jnp.zeros((tq, 1), F32), jnp.zeros((tq, dh), F32))
                 for _ in range(hp))
    carry = lax.fori_loop(0, qi, lambda j, c: block(j, c, False), init)
    final = block(qi, carry, True)
    for i, (_, l, acc) in enumerate(final):
        o_ref[:, i * dh:(i + 1) * dh] = (acc / l).astype(o_ref.dtype)


def _fox_prompt(z, drow, *, batch, seq, heads, dh, tq, hp):
    n = z.shape[0]
    nq = seq // tq
    hg = heads // hp
    w = hp * dh
    return pl.pallas_call(
        functools.partial(_fox_prompt_kernel, tq=tq, dh=dh, scale=dh ** -0.5),
        out_shape=jax.ShapeDtypeStruct((n, heads * dh), BF16),
        grid=(batch, hg, nq),
        in_specs=[pl.BlockSpec((tq, w), lambda b, h, i: (b * nq + i, h)),
                  pl.BlockSpec((seq, w), lambda b, h, i: (b, hg + h)),
                  pl.BlockSpec((seq, w), lambda b, h, i: (b, 2 * hg + h)),
                  pl.BlockSpec((None, SUBLANES, seq), lambda b, h, i: (b, 0, 0))],
        out_specs=pl.BlockSpec((tq, w), lambda b, h, i: (b * nq + i, h)),
        scratch_shapes=[pltpu.VMEM((seq, w), BF16), pltpu.VMEM((seq, w), BF16)],
        compiler_params=_params("arbitrary", "arbitrary", "arbitrary"),
        name="fox_prompt",
    )(z, z, z, drow)


def _gla_tables(c):
    r = np.arange(c)
    tri = (r[None, :] <= r[:, None]).astype(np.float32)
    masks = [np.eye(c, dtype=np.float32)]
    m = c // 2
    while m >= 1:
        same = (r[:, None] // (2 * m)) == (r[None, :] // (2 * m))
        upper = (r[:, None] % (2 * m)) >= m
        lower = (r[None, :] % (2 * m)) < m
        masks.append((same & upper & lower).astype(np.float32))
        m //= 2
    return tri, np.concatenate(masks, 0)


def _gla_level_refs(b, b_ref, col0):
    c, dk = b.shape
    bcast = lambda i, rows: jnp.broadcast_to(b_ref[pl.ds(i, 1), pl.ds(col0, dk)], (rows, dk))
    sub = lax.broadcasted_iota(jnp.int32, (c, dk), 0) % SUBLANES
    out = []
    m = c // 2
    while m >= 1:
        if 2 * m >= SUBLANES:
            out.append(jnp.concatenate([bcast(blk * 2 * m + m - 1, 2 * m) for blk in range(c // (2 * m))], axis=0))
        elif m > 1:
            ref = None
            for t in range(SUBLANES // (2 * m)):
                rows = jnp.concatenate([bcast(g * SUBLANES + t * 2 * m + m - 1, SUBLANES)
                                        for g in range(c // SUBLANES)], axis=0)
                ref = rows if ref is None else jnp.where(sub // (2 * m) == t, rows, ref)
            out.append(ref)
        else:
            out.append(jnp.where(sub % 2 == 0, b, pltpu.roll(b, 1, 0)))
        m //= 2
    return out


def _gla_kernel(q_ref, k_ref, v_ref, r_ref, ga_ref, s0_ref, tri_ref, masks_ref, gh_ref,
                og_ref, sfin_ref, st_ref, b_ref, *, heads, dk, dv, scale):
    ci = pl.program_id(1)

    @pl.when(ci == 0)
    def _():
        for h in range(heads):
            st_ref[h] = s0_ref[h].T

    c = q_ref.shape[0]
    b_all = _dot3(tri_ref[...], ga_ref[...])
    b_ref[...] = b_all
    for h in range(heads):
        q = q_ref[:, h * dk:(h + 1) * dk] * scale
        k = k_ref[:, h * dk:(h + 1) * dk]
        v = v_ref[:, h * dv:(h + 1) * dv]
        b = b_all[:, h * dk:(h + 1) * dk]
        a = jnp.where(masks_ref[0:c, :] != 0, jnp.sum(q * k, axis=-1, keepdims=True), 0.0)
        for lv, ref in enumerate(_gla_level_refs(b, b_ref, h * dk)):
            e = jnp.exp(-jnp.abs(b - ref))
            qt = (q * e).astype(BF16)
            kt = (k * e).astype(BF16)
            a = a + jnp.where(masks_ref[(lv + 1) * c:(lv + 2) * c, :] != 0, _dot_nt(qt, kt), 0.0)
        s_old = st_ref[h]
        o = _dot(a.astype(BF16), v.astype(BF16)) + _dot_nt((q * jnp.exp(b)).astype(BF16), s_old.astype(BF16))
        b_last = b[c - 1:c, :]
        upd = _dot(v.T.astype(BF16), (k * jnp.exp(b_last - b)).astype(BF16))
        st_ref[h] = s_old * jnp.exp(b_last) + upd
        y = _rms(o, gh_ref[...])
        og_ref[:, h * dv:(h + 1) * dv] = (y * _silu(r_ref[:, h * dv:(h + 1) * dv])).astype(og_ref.dtype)

    @pl.when(ci == pl.num_programs(1) - 1)
    def _():
        for h in range(heads):
            sfin_ref[h] = st_ref[h].T


def _gla(z, ga, s0, g_head, *, batch, seq, chunk, heads, dk, dv, col0):
    n = z.shape[0]
    c = chunk
    nc = seq // c
    wk, wv = heads * dk, heads * dv
    tri, masks = _gla_tables(c)
    row = lambda b, i: b * nc + i
    return pl.pallas_call(
        functools.partial(_gla_kernel, heads=heads, dk=dk, dv=dv, scale=dk ** -0.5),
        out_shape=[jax.ShapeDtypeStruct((n, wv), BF16), jax.ShapeDtypeStruct((batch, heads, dk, dv), F32)],
        grid=(batch, nc),
        in_specs=[pl.BlockSpec((c, wk), lambda b, i: (row(b, i), col0 // wk)),
                  pl.BlockSpec((c, wk), lambda b, i: (row(b, i), col0 // wk + 1)),
                  pl.BlockSpec((c, wv), lambda b, i: (row(b, i), (col0 + 2 * wk) // wv)),
                  pl.BlockSpec((c, wv), lambda b, i: (row(b, i), (col0 + 2 * wk) // wv + 1)),
                  pl.BlockSpec((c, wk), lambda b, i: (row(b, i), 0)),
                  pl.BlockSpec((None, heads, dk, dv), lambda b, i: (b, 0, 0, 0)),
                  pl.BlockSpec(tri.shape, lambda b, i: (0, 0)),
                  pl.BlockSpec(masks.shape, lambda b, i: (0, 0)),
                  pl.BlockSpec((1, dv), lambda b, i: (0, 0))],
        out_specs=[pl.BlockSpec((c, wv), lambda b, i: (row(b, i), 0)),
                   pl.BlockSpec((None, heads, dk, dv), lambda b, i: (b, 0, 0, 0))],
        scratch_shapes=[pltpu.VMEM((heads, dv, dk), F32), pltpu.VMEM((c, wk), F32)],
        compiler_params=_params("arbitrary", "arbitrary"),
        name="gla",
    )(z, z, z, z, ga, s0, jnp.asarray(tri, BF16), jnp.asarray(masks, F32), g_head)


def _cross_attn_kernel(a_ref, b_ref, w1_ref, w2_ref, x_ref, g_ref, wq_ref, mk_ref, mv_ref, wo_ref, o_ref, *, heads, dh):
    x = x_ref[...] + _dot(a_ref[...], w1_ref[...]) + _dot(b_ref[...], w2_ref[...])
    q = _dot(_rms(x, g_ref[...]).astype(BF16), wq_ref[...].astype(BF16)) * (dh ** -0.5)
    outs = []
    for h in range(heads):
        sl = slice(h * dh, (h + 1) * dh)
        s = _dot_nt(q[:, sl].astype(BF16), mk_ref[:, sl].astype(BF16))
        p = jnp.exp(s - jnp.max(s, axis=-1, keepdims=True))
        p = p / jnp.sum(p, axis=-1, keepdims=True)
        outs.append(_dot(p.astype(BF16), mv_ref[:, sl].astype(BF16)))
    o_ref[...] = x + _dot(jnp.concatenate(outs, axis=1).astype(BF16), wo_ref[...].astype(BF16))


def _cross_attn(a3, b3, w_o, x3, g, wq, mk, mv, wo, *, layer, heads, dh, tm):
    bsz, rows, d = x3.shape
    kh = a3.shape[2]
    mem = mk.shape[1]
    xd = heads * dh
    once = pl.Buffered(1)
    return pl.pallas_call(
        functools.partial(_cross_attn_kernel, heads=heads, dh=dh),
        out_shape=jax.ShapeDtypeStruct(x3.shape, F32),
        grid=(bsz, rows // tm),
        in_specs=[pl.BlockSpec((None, tm, kh), lambda b, i: (b, i, 0)),
                  pl.BlockSpec((None, tm, kh), lambda b, i: (b, i, 0)),
                  pl.BlockSpec((None, kh, d), lambda b, i: (layer, 0, 0), pipeline_mode=once),
                  pl.BlockSpec((None, kh, d), lambda b, i: (layer, 1, 0), pipeline_mode=once),
                  pl.BlockSpec((None, tm, d), lambda b, i: (b, i, 0)),
                  pl.BlockSpec((1, d), lambda b, i: (0, 0)),
                  pl.BlockSpec((None, d, xd), lambda b, i: (layer, 0, 0), pipeline_mode=once),
                  pl.BlockSpec((None, mem, xd), lambda b, i: (b, 0, 0)),
                  pl.BlockSpec((None, mem, xd), lambda b, i: (b, 0, 0)),
                  pl.BlockSpec((None, xd, d), lambda b, i: (layer, 0, 0), pipeline_mode=once)],
        out_specs=pl.BlockSpec((None, tm, d), lambda b, i: (b, i, 0)),
        compiler_params=_params("arbitrary", "arbitrary"),
        name="cross_attn",
    )(a3, b3, w_o, w_o, x3, g, wq, mk, mv, wo)


def _cross_attn_rows_kernel(a_ref, b_ref, w1_ref, w2_ref, x_ref, g_ref, wq_ref, mk_ref, mv_ref, wo_ref, o_ref, *,
                            heads, dh):
    x = x_ref[...] + _dot(a_ref[...], w1_ref[...]) + _dot(b_ref[...], w2_ref[...])
    n_rows = x.shape[0]
    q = _dot(_rms(x, g_ref[...]).astype(BF16), wq_ref[...].astype(BF16)) * (dh ** -0.5)
    row = lax.broadcasted_iota(jnp.int32, (n_rows, dh), 0)
    outs = []
    for h in range(heads):
        sl = slice(h * dh, (h + 1) * dh)
        qh = q[:, sl].astype(BF16)
        oh = jnp.zeros((n_rows, dh), F32)
        for r in range(n_rows):
            s = _dot_nt(qh, mk_ref[r, :, sl].astype(BF16))
            p = jnp.exp(s - jnp.max(s, axis=-1, keepdims=True))
            p = p / jnp.sum(p, axis=-1, keepdims=True)
            oh = jnp.where(row == r, _dot(p.astype(BF16), mv_ref[r, :, sl].astype(BF16)), oh)
        outs.append(oh)
    o_ref[...] = x + _dot(jnp.concatenate(outs, axis=1).astype(BF16), wo_ref[...].astype(BF16))


def _cross_attn_rows(a, b, w_o, x, g, wq, mk, mv, wo, *, layer, heads, dh):
    n_rows, d = x.shape
    kh = a.shape[1]
    mem = mk.shape[1]
    xd = heads * dh
    whole = lambda shape: pl.BlockSpec(shape, lambda i: (0,) * len(shape))
    return pl.pallas_call(
        functools.partial(_cross_attn_rows_kernel, heads=heads, dh=dh),
        out_shape=jax.ShapeDtypeStruct(x.shape, F32),
        grid=(1,),
        in_specs=[whole((n_rows, kh)), whole((n_rows, kh)),
                  pl.BlockSpec((None, kh, d), lambda i: (layer, 0, 0)),
                  pl.BlockSpec((None, kh, d), lambda i: (layer, 1, 0)),
                  whole((n_rows, d)), whole((1, d)),
                  pl.BlockSpec((None, d, xd), lambda i: (layer, 0, 0)),
                  whole((n_rows, mem, xd)), whole((n_rows, mem, xd)),
                  pl.BlockSpec((None, xd, d), lambda i: (layer, 0, 0))],
        out_specs=whole((n_rows, d)),
        compiler_params=_params("arbitrary"),
        name="cross_attn_rows",
    )(a, b, w_o, w_o, x, g, wq, mk, mv, wo)


def _swiglu_kernel(x_ref, xs_ref, g_ref, wg_ref, wu_ref, wd_ref, gf_ref, o_ref, os_ref, hb_ref, hs_ref, *, final_norm):
    i = pl.program_id(0)
    f = pl.program_id(1)
    last = pl.num_programs(1) - 1

    @pl.when(f == 0)
    def _():
        x = x_ref[...]
        hb_ref[...] = _rms(x, g_ref[...]).astype(BF16)
        o_ref[...] = x

    wg, wu, wd = wg_ref[...].astype(BF16), wu_ref[...].astype(BF16), wd_ref[...].astype(BF16)

    def ffn(hb):
        return _dot((_silu(_dot(hb, wg)) * _dot(hb, wu)).astype(BF16), wd)

    o_ref[...] += ffn(hb_ref[...])

    @pl.when(f == last)
    def _():
        if final_norm:
            o_ref[...] = _rms(o_ref[...], gf_ref[...])

    @pl.when(i == 0)
    def _():
        @pl.when(f == 0)
        def _():
            xs = xs_ref[...]
            hs_ref[...] = _rms(xs, g_ref[...]).astype(BF16)
            os_ref[...] = xs

        os_ref[...] += ffn(hs_ref[...])

        @pl.when(f == last)
        def _():
            if final_norm:
                os_ref[...] = _rms(os_ref[...], gf_ref[...])


def _swiglu(x, xs, g, wg, wu, wd, g_final, *, layer, final_norm, tm, tf):
    n, d = x.shape
    ns = xs.shape[0]
    ffn = wg.shape[2]
    return pl.pallas_call(
        functools.partial(_swiglu_kernel, final_norm=final_norm),
        out_shape=[jax.ShapeDtypeStruct((n, d), F32), jax.ShapeDtypeStruct((ns, d), F32)],
        grid=(n // tm, ffn // tf),
        in_specs=[pl.BlockSpec((tm, d), lambda i, f: (i, 0)),
                  pl.BlockSpec((ns, d), lambda i, f: (0, 0)),
                  pl.BlockSpec((1, d), lambda i, f: (0, 0)),
                  pl.BlockSpec((None, d, tf), lambda i, f: (layer, 0, f)),
                  pl.BlockSpec((None, d, tf), lambda i, f: (layer, 0, f)),
                  pl.BlockSpec((None, tf, d), lambda i, f: (layer, f, 0)),
                  pl.BlockSpec((1, d), lambda i, f: (0, 0))],
        out_specs=[pl.BlockSpec((tm, d), lambda i, f: (i, 0)),
                   pl.BlockSpec((ns, d), lambda i, f: (0, 0))],
        scratch_shapes=[pltpu.VMEM((tm, d), BF16), pltpu.VMEM((ns, d), BF16)],
        compiler_params=_params("arbitrary", "arbitrary"),
        name="swiglu",
    )(x, xs, g, wg, wu, wd, g_final)


def _suffix_rows_exclusive(x):
    n = x.shape[0]
    row = lax.broadcasted_iota(jnp.int32, x.shape, 0)

    def up(a, k):
        return jnp.where(row + k < n, pltpu.roll(a, n - k, 0), 0.0)

    y = up(x, 1)
    k = 1
    while k < n:
        y = y + up(y, k)
        k *= 2
    return y


def _fox_decode_kernel(pt_ref, q_ref, kn_ref, vn_ref, cq_ref, *rest, heads, group, scale):
    del pt_ref
    k_refs, v_refs, f_refs = rest[:group], rest[group:2 * group], rest[2 * group:3 * group]
    u_same_ref, u_later_ref, o_ref, m_ref, l_ref, acc_ref, tail_ref = rest[3 * group:]
    step = pl.program_id(1)

    @pl.when(step == 0)
    def _():
        m_ref[...] = jnp.full_like(m_ref, -jnp.inf)
        l_ref[...] = jnp.zeros_like(l_ref)
        acc_ref[...] = jnp.zeros_like(acc_ref)
        tail_ref[...] = jnp.zeros_like(tail_ref)

    n_blk = k_refs[0].shape[0] // LANES
    wide = MXU_COLS // LANES
    q = q_ref[...]
    qb = q.astype(BF16)
    sub = lax.broadcasted_iota(jnp.int32, (heads, MXU_COLS), 0)
    lane = lax.broadcasted_iota(jnp.int32, (heads, MXU_COLS), 1)
    own = sub == lane % heads
    parts = _split3(jnp.concatenate([f[...] for f in f_refs], axis=0))
    rows_later = sum(_dot(x, u_later_ref[...]) for x in parts)
    rows_total = sum(_dot(x, u_same_ref[...]) for x in parts)
    tail = tail_ref[...]
    tiles = []
    for g in range(group):
        row_total = rows_total[g * n_blk:(g + 1) * n_blk]
        bias = cq_ref[...] + tail + rows_later[g * n_blk:(g + 1) * n_blk] + _suffix_rows_exclusive(row_total)
        tail = tail + jnp.sum(row_total, axis=0, keepdims=True)
        for r in range(0, n_blk, wide):
            kb = k_refs[g][r * LANES:(r + wide) * LANES, :].astype(BF16)
            brow = jnp.concatenate([bias[r + w:r + w + 1, :] for w in range(wide)], axis=1)
            s = jnp.where(own, _dot_nt(qb, kb) * scale + brow, -jnp.inf)
            tiles.extend(s[:, w * LANES:(w + 1) * LANES] for w in range(wide))
    tail_ref[...] = tail

    top = tiles[0]
    for t in tiles[1:]:
        top = jnp.maximum(top, t)
    m_old = m_ref[...]
    m_new = jnp.maximum(m_old, jnp.max(top, axis=-1, keepdims=True))
    a = jnp.exp(m_old - m_new)
    accs = [a * acc_ref[...], jnp.zeros(acc_ref.shape, F32)]
    psum = jnp.zeros((heads, LANES), F32)
    dh = acc_ref.shape[1]
    for n, idx in enumerate(range(0, len(tiles), wide)):
        g, r = divmod(idx, n_blk)
        prs = [jnp.exp(tiles[idx + w] - m_new) for w in range(wide)]
        for pr in prs:
            psum = psum + pr
        vb = jnp.concatenate([v_refs[g][(r + w) * LANES:(r + w + 1) * LANES, :].astype(BF16) for w in range(wide)],
                             axis=1)
        res = _dot(jnp.concatenate(prs, axis=0).astype(BF16), vb)
        for w in range(wide):
            accs[n % 2] = accs[n % 2] + res[w * heads:(w + 1) * heads, w * dh:(w + 1) * dh]
    m_ref[...] = m_new
    l_ref[...] = a * l_ref[...] + jnp.sum(psum, axis=-1, keepdims=True)
    acc_ref[...] = accs[0] + accs[1]

    @pl.when(step == pl.num_programs(1) - 1)
    def _():
        s_new = jnp.sum(q * kn_ref[...], axis=-1, keepdims=True) * scale
        m_fin = jnp.maximum(m_ref[...], s_new)
        a_fin = jnp.exp(m_ref[...] - m_fin)
        p_new = jnp.exp(s_new - m_fin)
        o_ref[...] = ((a_fin * acc_ref[...] + p_new * vn_ref[...]) / (a_fin * l_ref[...] + p_new)).astype(o_ref.dtype)


def _fox_decode(q, k_new, v_new, cq, cache_k, cache_v, cache_f, page_table, *, layer, group):
    bsz, heads, dh = q.shape
    rows = cache_k.shape[2]
    n_pages = page_table.shape[1]
    lane = np.arange(LANES)
    same = lane[:, None] % heads == lane[None, :] % heads
    u_same = jnp.asarray(same, BF16)
    u_later = jnp.asarray(same & (lane[:, None] > lane[None, :]), BF16)
    vec = pl.BlockSpec((None, heads, dh), lambda b, s, pt: (b, 0, 0))

    def paged(shape, g):
        return pl.BlockSpec((None, None) + shape, lambda b, s, pt: (layer, pt[b, n_pages - 1 - (s * group + g)], 0, 0))

    const = pl.BlockSpec((LANES, LANES), lambda b, s, pt: (0, 0))
    grid_spec = pltpu.PrefetchScalarGridSpec(
        num_scalar_prefetch=1,
        grid=(bsz, n_pages // group),
        in_specs=([vec, vec, vec, pl.BlockSpec((None, 1, LANES), lambda b, s, pt: (b, 0, 0))]
                  + [paged((rows, dh), g) for g in range(group)]
                  + [paged((rows, dh), g) for g in range(group)]
                  + [paged((rows // LANES, LANES), g) for g in range(group)]
                  + [const, const]),
        out_specs=vec,
        scratch_shapes=[pltpu.VMEM((heads, 1), F32), pltpu.VMEM((heads, 1), F32),
                        pltpu.VMEM((heads, dh), F32), pltpu.VMEM((1, LANES), F32)],
    )
    return pl.pallas_call(
        functools.partial(_fox_decode_kernel, heads=heads, group=group, scale=dh ** -0.5),
        out_shape=jax.ShapeDtypeStruct((bsz, heads, dh), BF16),
        grid_spec=grid_spec,
        compiler_params=_params("arbitrary", "arbitrary"),
        name="fox_decode",
    )(page_table, q, k_new, v_new, cq, *([cache_k] * group), *([cache_v] * group), *([cache_f] * group),
      u_same, u_later)


def kernel(x_prompt, x_sample, mem_prompt, cache_fox_k, cache_fox_v, cache_fox_logf, state_gla, cache_mem_k, cache_mem_v, page_table, g_mix, w_in, b_forget, w_gla_gate, b_gla_gate, g_gla_head, w_out, g_cross, g_mem, w_xq, w_xk, w_xv, w_xo, g_ffn, w_ffn_gate, w_ffn_up, w_ffn_down, g_final):
    depth = w_in.shape[0]
    bsz, seq, d = x_prompt.shape
    sb = x_sample.shape[0]
    mem = mem_prompt.shape[1]
    _, pool, page, fh, fdh = cache_fox_k.shape
    _, _, gh, gdk, gdv = state_gla.shape
    xh, xdh = cache_mem_k.shape[3], cache_mem_k.shape[4]
    fox_dim, gk_dim, gv_dim = fh * fdh, gh * gdk, gh * gdv
    rank = w_gla_gate.shape[1]
    n = bsz * seq
    c_forget = 3 * fox_dim
    c_gla = c_forget + fh
    c_rank = c_gla + 2 * gk_dim + 2 * gv_dim

    xp = x_prompt.reshape(n, d)
    xs = x_sample.reshape(sb, d)
    memx = mem_prompt.reshape(bsz * mem, d)
    ck = cache_fox_k.reshape(depth, pool, page * fh, fdh)
    cv = cache_fox_v.reshape(depth, pool, page * fh, fdh)
    cf = cache_fox_logf.reshape(depth, pool, page * fh // LANES, LANES)
    row = lambda v: v.reshape(1, -1)
    zeros_state = jnp.zeros((bsz, gh, gdk, gdv), F32)
    group = min(DECODE_PAGES_PER_STEP, page_table.shape[1])

    w_t = jnp.swapaxes(w_in, 1, 2)
    w_main, w_small = _w_main(w_t, c_a=c_forget, c_b0=c_gla, c_b1=c_rank, tr=W_PREP_ROWS)
    w_g2 = jnp.concatenate([jnp.zeros((depth, fh, gk_dim), F32), w_gla_gate,
                            jnp.zeros((depth, LANES - fh - rank, gk_dim), F32)], axis=1).astype(BF16)
    b_f = jnp.concatenate([b_forget, jnp.zeros((depth, LANES - fh), F32)], axis=1)
    w_kv = jnp.concatenate([w_xk, w_xv], axis=2)
    w_o = w_out.astype(BF16)

    pk, pv, pf, ps, pmk, pmv, sk, sv, sf, ss = ([] for _ in range(10))
    for l in range(depth):
        last = l == depth - 1

        z, logf, ga, k3, v3, drow = _in_proj(xp, row(g_mix[l]), w_main, w_small, b_f[l:l + 1], w_g2,
                                             row(b_gla_gate[l]), layer=l, n_fox_heads=fh, fox_dh=fdh, seq=seq,
                                             tm=min(ROW_TILE_PROJ, seq), tn=COL_TILE)
        o_f = _fox_prompt(z, drow, batch=bsz, seq=seq, heads=fh, dh=fdh, tq=min(FOX_Q_TILE, seq),
                          hp=FOX_HEADS_PER_STEP)
        o_g, s_fin = _gla(z, ga, zeros_state, row(g_gla_head[l]), batch=bsz, seq=seq, chunk=min(GLA_CHUNK, seq),
                          heads=gh, dk=gdk, dv=gdv, col0=3 * fox_dim)
        mkv = _norm_matmul(memx, row(g_mem[l]), w_kv, layer=l, tm=bsz * mem, tn=COL_TILE)
        mk, mv = mkv[:, :xh * xdh], mkv[:, xh * xdh:]
        xp = _cross_attn(o_f.reshape(bsz, seq, -1), o_g.reshape(bsz, seq, -1), w_o, xp.reshape(bsz, seq, d),
                         row(g_cross[l]), w_xq, mk.reshape(bsz, mem, -1), mv.reshape(bsz, mem, -1), w_xo, layer=l,
                         heads=xh, dh=xdh, tm=min(ROW_TILE_XATTN, seq)).reshape(n, d)
        pk.append(k3.reshape(bsz, seq, fh, fdh))
        pv.append(v3.reshape(bsz, seq, fh, fdh))
        pf.append(logf.reshape(bsz, seq, fh))
        ps.append(s_fin)
        pmk.append(mk.reshape(bsz, mem, xh, xdh))
        pmv.append(mv.reshape(bsz, mem, xh, xdh))

        zs, logf_s, ga_s, fk, fv = _in_proj(xs, row(g_mix[l]), w_main, w_small, b_f[l:l + 1], w_g2,
                                            row(b_gla_gate[l]), layer=l, n_fox_heads=fh, fox_dh=fdh, seq=0,
                                            tm=sb, tn=COL_TILE)
        fq, fk, fv = zs[:, :fox_dim].reshape(sb, fh, fdh), fk.reshape(sb, fh, fdh), fv.reshape(sb, fh, fdh)
        cq = jnp.tile(logf_s, (1, LANES // fh)).reshape(sb, 1, LANES)
        o_fs = _fox_decode(fq, fk, fv, cq, ck, cv, cf, page_table, layer=l, group=group).reshape(sb, fox_dim)
        cs = GLA_CHUNK_SAMPLE
        pad = lambda a: jnp.zeros((sb, cs, a.shape[1]), a.dtype).at[:, 0].set(a).reshape(sb * cs, -1)
        o_gs, s_new = _gla(pad(zs), pad(ga_s), state_gla[l], row(g_gla_head[l]), batch=sb, seq=cs, chunk=cs,
                           heads=gh, dk=gdk, dv=gdv, col0=3 * fox_dim)
        o_gs = o_gs.reshape(sb, cs, gv_dim)[:, 0]
        xs = _cross_attn_rows(o_fs, o_gs, w_o, xs, row(g_cross[l]), w_xq, cache_mem_k[l].reshape(sb, mem, -1),
                              cache_mem_v[l].reshape(sb, mem, -1), w_xo, layer=l, heads=xh, dh=xdh)
        xp, xs = _swiglu(xp, xs, row(g_ffn[l]), w_ffn_gate, w_ffn_up, w_ffn_down, row(g_final), layer=l,
                         final_norm=last, tm=min(ROW_TILE_FFN, n), tf=FFN_TILE)
        sk.append(fk.reshape(sb, 1, fh, fdh))
        sv.append(fv.reshape(sb, 1, fh, fdh))
        sf.append(logf_s.reshape(sb, 1, fh))
        ss.append(s_new)

    return (xp.reshape(bsz, seq, d), xs.reshape(sb, 1, d),
            jnp.stack(pk), jnp.stack(pv), jnp.stack(pf), jnp.stack(ps), jnp.stack(pmk), jnp.stack(pmv),
            jnp.stack(sk), jnp.stack(sv), jnp.stack(sf), jnp.stack(ss))
```

```python
import functools

import numpy as np
import jax
import jax.numpy as jnp
from jax import lax
from jax.experimental import pallas as pl
from jax.experimental.pallas import tpu as pltpu

F32 = jnp.float32
BF16 = jnp.bfloat16

RMS_EPS = 1e-6
LOG2E = 1.4426950408889634
GLA_TAU = 16.0
GLA_CHUNK = 128
GLA_CHUNK_SAMPLE = 64
LANES = 128
SUBLANES = 8
VMEM_LIMIT_BYTES = 56 * 1024 * 1024
ROW_TILE_PROJ = 1024
ROW_TILE_XATTN = 512
ROW_TILE_FFN = 1024
FFN_TILE = 256
COL_TILE = 512
FOX_Q_TILE = 512
FOX_HEADS_PER_STEP = 4
DECODE_PAGES_PER_STEP = 16
MXU_COLS = 256
W_PREP_ROWS = 256


def _params(*sem):
    return pltpu.CompilerParams(dimension_semantics=sem, vmem_limit_bytes=VMEM_LIMIT_BYTES)


def _rms(x, g):
    return x * lax.rsqrt(jnp.mean(x * x, axis=-1, keepdims=True) + RMS_EPS) * g


def _log_sigmoid(x):
    return jnp.minimum(x, 0.0) - jnp.log1p(jnp.exp(-jnp.abs(x)))


def _silu(x):
    return x / (1.0 + jnp.exp(-x))


def _dot(a, b):
    return jnp.dot(a, b, preferred_element_type=F32)


def _dot_nt(a, b):
    return lax.dot_general(a, b, (((1,), (1,)), ((), ())), preferred_element_type=F32)


def _dot_tn(a, b):
    return lax.dot_general(a, b, (((0,), (0,)), ((), ())), preferred_element_type=F32)


def _split3(x):
    a = x.astype(BF16)
    r = x - a.astype(F32)
    b = r.astype(BF16)
    c = (r - b.astype(F32)).astype(BF16)
    return a, b, c


def _dot3(m, x, dot=_dot):
    a, b, c = _split3(x)
    return dot(m, a) + dot(m, b) + dot(m, c)


def _w_main_kernel(w_ref, wf_ref, wr_ref, o_ref, os_ref):
    o_ref[...] = w_ref[0].astype(BF16)

    @pl.when(pl.program_id(1) == 0)
    def _():
        pad = jnp.zeros((os_ref.shape[0] - wf_ref.shape[1] - wr_ref.shape[1], os_ref.shape[1]), F32)
        os_ref[...] = jnp.concatenate([wf_ref[0], wr_ref[0], pad], axis=0).astype(BF16)


def _w_main(w_t, *, c_a, c_b0, c_b1, tr):
    depth, n_in, d = w_t.shape
    rows = c_a + c_b1 - c_b0

    def src(l, r):
        start = r * tr
        return l, pl.multiple_of(start + jnp.where(start >= c_a, c_b0 - c_a, 0), SUBLANES), 0

    rows_at = lambda n, at: pl.BlockSpec((pl.Element(1), pl.Element(n), pl.Element(d)), lambda l, r: (l, at, 0))
    return pl.pallas_call(
        _w_main_kernel,
        out_shape=[jax.ShapeDtypeStruct((depth, rows, d), BF16), jax.ShapeDtypeStruct((depth, LANES, d), BF16)],
        grid=(depth, rows // tr),
        in_specs=[pl.BlockSpec((pl.Element(1), pl.Element(tr), pl.Element(d)), src),
                  rows_at(c_b0 - c_a, c_a), rows_at(n_in - c_b1, c_b1)],
        out_specs=[pl.BlockSpec((None, tr, d), lambda l, r: (l, r, 0)),
                   pl.BlockSpec((None, LANES, d), lambda l, r: (l, 0, 0))],
        compiler_params=_params("arbitrary", "arbitrary"),
        name="w_main",
    )(w_t, w_t, w_t)


def _in_proj_kernel(x_ref, g_ref, w_ref, ws_ref, bf_ref, wg2_ref, bg_ref, *rest, n_fox_heads, tiles_per_seq):
    if tiles_per_seq:
        z_ref, logf_ref, ga_ref, k3_ref, v3_ref, drow_ref, hb_ref, carry_ref = rest
    else:
        z_ref, logf_ref, ga_ref, k3_ref, v3_ref, hb_ref = rest
    i = pl.program_id(0)
    j = pl.program_id(1)

    @pl.when(j == 0)
    def _():
        hb = _rms(x_ref[...], g_ref[...]).astype(BF16)
        hb_ref[...] = hb
        zs = _dot_nt(hb, ws_ref[...])
        lf = _log_sigmoid(zs + bf_ref[...])
        logf_ref[...] = lf[:, :n_fox_heads]
        gp = _dot(zs.astype(BF16), wg2_ref[...]) + bg_ref[...]
        ga_ref[...] = _log_sigmoid(gp) * (1.0 / GLA_TAU)
        if tiles_per_seq:
            tm = lf.shape[0]
            r = lax.broadcasted_iota(jnp.int32, (LANES, LANES), 0)
            c = lax.broadcasted_iota(jnp.int32, (LANES, LANES), 1)
            tri = jnp.where(r <= c, 1.0, 0.0).astype(BF16)
            lft = lf.T[:SUBLANES]
            run = jnp.where(i % tiles_per_seq == 0, 0.0, carry_ref[...])
            for c0 in range(0, tm, LANES):
                d = _dot3(tri, lft[:, c0:c0 + LANES], dot=lambda m, x: _dot(x, m)) + run
                drow_ref[:, c0:c0 + LANES] = d
                run = d[:, LANES - 1:LANES]
            carry_ref[...] = run

    zt = _dot_nt(hb_ref[...], w_ref[...])
    z_ref[...] = zt
    tm, tn = zt.shape
    dh = k3_ref.shape[1]
    heads_per_tile = tn // dh
    tiles_per_group = n_fox_heads // heads_per_tile
    for group, out_ref in ((1, k3_ref), (2, v3_ref)):
        for t in range(tiles_per_group):
            @pl.when(j == group * tiles_per_group + t)
            def _(out_ref=out_ref, t=t):
                for c in range(heads_per_tile):
                    out_ref[pl.ds(t * heads_per_tile + c, tm, stride=n_fox_heads), :] = zt[:, c * dh:(c + 1) * dh]


def _in_proj(x, g, w_main, w_small, b_f, w_g2, b_g, *, layer, n_fox_heads, fox_dh, seq, tm, tn):
    n, d = x.shape
    nz = w_main.shape[1]
    kg = w_g2.shape[2]
    tiles_per_seq = seq // tm if seq else 0
    out_shape = [jax.ShapeDtypeStruct((n, nz), F32),
                 jax.ShapeDtypeStruct((n, n_fox_heads), F32),
                 jax.ShapeDtypeStruct((n, kg), F32),
                 jax.ShapeDtypeStruct((n * n_fox_heads, fox_dh), F32),
                 jax.ShapeDtypeStruct((n * n_fox_heads, fox_dh), F32)]
    out_specs = [pl.BlockSpec((tm, tn), lambda i, j: (i, j)),
                 pl.BlockSpec((tm, n_fox_heads), lambda i, j: (i, 0)),
                 pl.BlockSpec((tm, kg), lambda i, j: (i, 0)),
                 pl.BlockSpec((tm * n_fox_heads, fox_dh), lambda i, j: (i, 0)),
                 pl.BlockSpec((tm * n_fox_heads, fox_dh), lambda i, j: (i, 0))]
    scratch = [pltpu.VMEM((tm, d), BF16)]
    if tiles_per_seq:
        out_shape.append(jax.ShapeDtypeStruct((n // seq, SUBLANES, seq), F32))
        out_specs.append(pl.BlockSpec((None, SUBLANES, tm), lambda i, j: (i // tiles_per_seq, 0, i % tiles_per_seq)))
        scratch.append(pltpu.VMEM((SUBLANES, 1), F32))
    return pl.pallas_call(
        functools.partial(_in_proj_kernel, n_fox_heads=n_fox_heads, tiles_per_seq=tiles_per_seq),
        out_shape=out_shape,
        grid=(n // tm, nz // tn),
        in_specs=[pl.BlockSpec((tm, d), lambda i, j: (i, 0)),
                  pl.BlockSpec((1, d), lambda i, j: (0, 0)),
                  pl.BlockSpec((None, tn, d), lambda i, j: (layer, j, 0)),
                  pl.BlockSpec((None, LANES, d), lambda i, j: (layer, 0, 0)),
                  pl.BlockSpec((1, LANES), lambda i, j: (0, 0)),
                  pl.BlockSpec((None, LANES, kg), lambda i, j: (layer, 0, 0)),
                  pl.BlockSpec((1, kg), lambda i, j: (0, 0))],
        out_specs=out_specs,
        scratch_shapes=scratch,
        compiler_params=_params("arbitrary", "arbitrary"),
        name="in_proj",
    )(x, g, w_main, w_small, b_f, w_g2, b_g)


def _norm_matmul_kernel(x_ref, g_ref, w_ref, o_ref, hb_ref):
    @pl.when(pl.program_id(1) == 0)
    def _():
        hb_ref[...] = _rms(x_ref[...], g_ref[...]).astype(BF16)

    o_ref[...] = _dot(hb_ref[...], w_ref[...].astype(BF16))


def _norm_matmul(x, g, w, *, layer, tm, tn):
    n, d = x.shape
    nout = w.shape[2]
    return pl.pallas_call(
        _norm_matmul_kernel,
        out_shape=jax.ShapeDtypeStruct((n, nout), F32),
        grid=(n // tm, nout // tn),
        in_specs=[pl.BlockSpec((tm, d), lambda i, j: (i, 0)),
                  pl.BlockSpec((1, d), lambda i, j: (0, 0)),
                  pl.BlockSpec((None, d, tn), lambda i, j: (layer, 0, j))],
        out_specs=pl.BlockSpec((tm, tn), lambda i, j: (i, j)),
        scratch_shapes=[pltpu.VMEM((tm, d), BF16)],
        compiler_params=_params("arbitrary", "arbitrary"),
        name="norm_matmul",
    )(x, g, w)


def _fox_prompt_kernel(q_ref, k_ref, v_ref, d_ref, o_ref, kb_ref, vb_ref, *, tq, dh, scale):
    hg = pl.program_id(1)
    qi = pl.program_id(2)
    hp = q_ref.shape[1] // dh

    @pl.when(qi == 0)
    def _():
        kb_ref[...] = k_ref[...].astype(BF16)
        vb_ref[...] = v_ref[...].astype(BF16)

    qs = [(q_ref[:, i * dh:(i + 1) * dh] * (scale * LOG2E)).astype(BF16) for i in range(hp)]

    def block(j, carry, masked):
        start = pl.multiple_of(j * tq, tq)
        out = []
        for i in range(hp):
            m, l, acc = carry[i]
            kj = kb_ref[pl.ds(start, tq), i * dh:(i + 1) * dh]
            vj = vb_ref[pl.ds(start, tq), i * dh:(i + 1) * dh]
            s = _dot_nt(qs[i], kj) - d_ref[pl.ds(hg * hp + i, 1), pl.ds(start, tq)] * LOG2E
            if masked:
                r = lax.broadcasted_iota(jnp.int32, (tq, tq), 0)
                c = lax.broadcasted_iota(jnp.int32, (tq, tq), 1)
                s = jnp.where(c <= r, s, -jnp.inf)
            m_new = jnp.maximum(m, jnp.max(s, axis=-1, keepdims=True))
            a = jnp.exp2(m - m_new)
            p = jnp.exp2(s - m_new)
            l = a * l + jnp.sum(p, axis=-1, keepdims=True)
            acc = a * acc + _dot(p.astype(BF16), vj)
            out.append((m_new, l, acc))
        return tuple(out)

    init = tuple((jnp.full((tq, 1), -jnp.inf, F32), jnp.zeros((tq, 1), F32), jnp.zeros((tq, dh), F32))
                 for _ in range(hp))
    carry = lax.fori_loop(0, qi, lambda j, c: block(j, c, False), init)
    final = block(qi, carry, True)
    for i, (_, l, acc) in enumerate(final):
        o_ref[:, i * dh:(i + 1) * dh] = (acc / l).astype(o_ref.dtype)


def _fox_prompt(z, drow, *, batch, seq, heads, dh, tq, hp):
    n = z.shape[0]
    nq = seq // tq
    hg = heads // hp
    w = hp * dh
    return pl.pallas_call(
        functools.partial(_fox_prompt_kernel, tq=tq, dh=dh, scale=dh ** -0.5),
        out_shape=jax.ShapeDtypeStruct((n, heads * dh), BF16),
        grid=(batch, hg, nq),
        in_specs=[pl.BlockSpec((tq, w), lambda b, h, i: (b * nq + i, h)),
                  pl.BlockSpec((seq, w), lambda b, h, i: (b, hg + h)),
                  pl.BlockSpec((seq, w), lambda b, h, i: (b, 2 * hg + h)),
                  pl.BlockSpec((None, SUBLANES, seq), lambda b, h, i: (b, 0, 0))],
        out_specs=pl.BlockSpec((tq, w), lambda b, h, i: (b * nq + i, h)),
        scratch_shapes=[pltpu.VMEM((seq, w), BF16), pltpu.VMEM((seq, w), BF16)],
        compiler_params=_params("arbitrary", "arbitrary", "arbitrary"),
        name="fox_prompt",
    )(z, z, z, drow)


def _gla_tables(c):
    r = np.arange(c)
    tri = (r[None, :] <= r[:, None]).astype(np.float32)
    masks = [np.eye(c, dtype=np.float32)]
    m = c // 2
    while m >= 1:
        same = (r[:, None] // (2 * m)) == (r[None, :] // (2 * m))
        upper = (r[:, None] % (2 * m)) >= m
        lower = (r[None, :] % (2 * m)) < m
        masks.append((same & upper & lower).astype(np.float32))
        m //= 2
    return tri, np.concatenate(masks, 0)


def _gla_level_refs(b, b_ref, col0):
    c, dk = b.shape
    bcast = lambda i, rows: jnp.broadcast_to(b_ref[pl.ds(i, 1), pl.ds(col0, dk)], (rows, dk))
    sub = lax.broadcasted_iota(jnp.int32, (c, dk), 0) % SUBLANES
    out = []
    m = c // 2
    while m >= 1:
        if 2 * m >= SUBLANES:
            out.append(jnp.concatenate([bcast(blk * 2 * m + m - 1, 2 * m) for blk in range(c // (2 * m))], axis=0))
        elif m > 1:
            ref = None
            for t in range(SUBLANES // (2 * m)):
                rows = jnp.concatenate([bcast(g * SUBLANES + t * 2 * m + m - 1, SUBLANES)
                                        for g in range(c // SUBLANES)], axis=0)
                ref = rows if ref is None else jnp.where(sub // (2 * m) == t, rows, ref)
            out.append(ref)
        else:
            out.append(jnp.where(sub % 2 == 0, b, pltpu.roll(b, 1, 0)))
        m //= 2
    return out


def _gla_kernel(q_ref, k_ref, v_ref, r_ref, ga_ref, s0_ref, tri_ref, masks_ref, gh_ref,
                og_ref, sfin_ref, st_ref, b_ref, *, heads, dk, dv, scale):
    ci = pl.program_id(1)

    @pl.when(ci == 0)
    def _():
        for h in range(heads):
            st_ref[h] = s0_ref[h].T

    c = tri_ref.shape[0]
    for sub in range(q_ref.shape[0] // c):
        rows = slice(sub * c, (sub + 1) * c)
        b_all = _dot3(tri_ref[...], ga_ref[rows, :])
        b_ref[sub] = b_all
        for h in range(heads):
            q = q_ref[rows, h * dk:(h + 1) * dk] * scale
            k = k_ref[rows, h * dk:(h + 1) * dk]
            v = v_ref[rows, h * dv:(h + 1) * dv]
            b = b_all[:, h * dk:(h + 1) * dk]
            a = jnp.where(masks_ref[0:c, :] != 0, jnp.sum(q * k, axis=-1, keepdims=True), 0.0)
            for lv, ref in enumerate(_gla_level_refs(b, b_ref.at[sub], h * dk)):
                e = jnp.exp(-jnp.abs(b - ref))
                qt = (q * e).astype(BF16)
                kt = (k * e).astype(BF16)
                a = a + jnp.where(masks_ref[(lv + 1) * c:(lv + 2) * c, :] != 0, _dot_nt(qt, kt), 0.0)
            s_old = st_ref[h]
            o = _dot(a.astype(BF16), v.astype(BF16)) + _dot_nt((q * jnp.exp(b)).astype(BF16), s_old.astype(BF16))
            b_last = b[c - 1:c, :]
            upd = _dot(v.T.astype(BF16), (k * jnp.exp(b_last - b)).astype(BF16))
            st_ref[h] = s_old * jnp.exp(b_last) + upd
            y = _rms(o, gh_ref[...])
            og_ref[rows, h * dv:(h + 1) * dv] = (y * _silu(r_ref[rows, h * dv:(h + 1) * dv])).astype(og_ref.dtype)

    @pl.when(ci == pl.num_programs(1) - 1)
    def _():
        for h in range(heads):
            sfin_ref[h] = st_ref[h].T


def _gla(z, ga, s0, g_head, *, batch, seq, chunk, heads, dk, dv, col0):
    n = z.shape[0]
    c = chunk
    per = 2 if seq // c % 2 == 0 else 1
    nc = seq // (c * per)
    wk, wv = heads * dk, heads * dv
    tri, masks = _gla_tables(c)
    row = lambda b, i: b * nc + i
    return pl.pallas_call(
        functools.partial(_gla_kernel, heads=heads, dk=dk, dv=dv, scale=dk ** -0.5),
        out_shape=[jax.ShapeDtypeStruct((n, wv), BF16), jax.ShapeDtypeStruct((batch, heads, dk, dv), F32)],
        grid=(batch, nc),
        in_specs=[pl.BlockSpec((c * per, wk), lambda b, i: (row(b, i), col0 // wk)),
                  pl.BlockSpec((c * per, wk), lambda b, i: (row(b, i), col0 // wk + 1)),
                  pl.BlockSpec((c * per, wv), lambda b, i: (row(b, i), (col0 + 2 * wk) // wv)),
                  pl.BlockSpec((c * per, wv), lambda b, i: (row(b, i), (col0 + 2 * wk) // wv + 1)),
                  pl.BlockSpec((c * per, wk), lambda b, i: (row(b, i), 0)),
                  pl.BlockSpec((None, heads, dk, dv), lambda b, i: (b, 0, 0, 0)),
                  pl.BlockSpec(tri.shape, lambda b, i: (0, 0)),
                  pl.BlockSpec(masks.shape, lambda b, i: (0, 0)),
                  pl.BlockSpec((1, dv), lambda b, i: (0, 0))],
        out_specs=[pl.BlockSpec((c * per, wv), lambda b, i: (row(b, i), 0)),
                   pl.BlockSpec((None, heads, dk, dv), lambda b, i: (b, 0, 0, 0))],
        scratch_shapes=[pltpu.VMEM((heads, dv, dk), F32), pltpu.VMEM((per, c, wk), F32)],
        compiler_params=_params("arbitrary", "arbitrary"),
        name="gla",
    )(z, z, z, z, ga, s0, jnp.asarray(tri, BF16), jnp.asarray(masks, F32), g_head)


def _cross_attn_kernel(a_ref, b_ref, w1_ref, w2_ref, x_ref, g_ref, wq_ref, mk_ref, mv_ref, wo_ref, o_ref, *, heads, dh):
    x = x_ref[...] + _dot(a_ref[...], w1_ref[...]) + _dot(b_ref[...], w2_ref[...])
    q = _dot(_rms(x, g_ref[...]).astype(BF16), wq_ref[...].astype(BF16)) * (dh ** -0.5)
    outs = []
    for h in range(heads):
        sl = slice(h * dh, (h + 1) * dh)
        s = _dot_nt(q[:, sl].astype(BF16), mk_ref[:, sl].astype(BF16))
        p = jnp.exp(s - jnp.max(s, axis=-1, keepdims=True))
        p = p / jnp.sum(p, axis=-1, keepdims=True)
        outs.append(_dot(p.astype(BF16), mv_ref[:, sl].astype(BF16)))
    o_ref[...] = x + _dot(jnp.concatenate(outs, axis=1).astype(BF16), wo_ref[...].astype(BF16))


def _cross_attn(a3, b3, w_o, x3, g, wq, mk, mv, wo, *, layer, heads, dh, tm):
    bsz, rows, d = x3.shape
    kh = a3.shape[2]
    mem = mk.shape[1]
    xd = heads * dh
    once = pl.Buffered(1)
    return pl.pallas_call(
        functools.partial(_cross_attn_kernel, heads=heads, dh=dh),
        out_shape=jax.ShapeDtypeStruct(x3.shape, F32),
        grid=(bsz, rows // tm),
        in_specs=[pl.BlockSpec((None, tm, kh), lambda b, i: (b, i, 0)),
                  pl.BlockSpec((None, tm, kh), lambda b, i: (b, i, 0)),
                  pl.BlockSpec((None, kh, d), lambda b, i: (layer, 0, 0), pipeline_mode=once),
                  pl.BlockSpec((None, kh, d), lambda b, i: (layer, 1, 0), pipeline_mode=once),
                  pl.BlockSpec((None, tm, d), lambda b, i: (b, i, 0)),
                  pl.BlockSpec((1, d), lambda b, i: (0, 0)),
                  pl.BlockSpec((None, d, xd), lambda b, i: (layer, 0, 0), pipeline_mode=once),
                  pl.BlockSpec((None, mem, xd), lambda b, i: (b, 0, 0)),
                  pl.BlockSpec((None, mem, xd), lambda b, i: (b, 0, 0)),
                  pl.BlockSpec((None, xd, d), lambda b, i: (layer, 0, 0), pipeline_mode=once)],
        out_specs=pl.BlockSpec((None, tm, d), lambda b, i: (b, i, 0)),
        compiler_params=_params("arbitrary", "arbitrary"),
        name="cross_attn",
    )(a3, b3, w_o, w_o, x3, g, wq, mk, mv, wo)


def _cross_attn_rows_kernel(a_ref, b_ref, w1_ref, w2_ref, x_ref, g_ref, wq_ref, mk_ref, mv_ref, wo_ref, o_ref, *,
                            heads, dh):
    x = x_ref[...] + _dot(a_ref[...], w1_ref[...]) + _dot(b_ref[...], w2_ref[...])
    n_rows = x.shape[0]
    q = _dot(_rms(x, g_ref[...]).astype(BF16), wq_ref[...].astype(BF16)) * (dh ** -0.5)
    row = lax.broadcasted_iota(jnp.int32, (n_rows, dh), 0)
    outs = []
    for h in range(heads):
        sl = slice(h * dh, (h + 1) * dh)
        qh = q[:, sl].astype(BF16)
        oh = jnp.zeros((n_rows, dh), F32)
        for r in range(n_rows):
            s = _dot_nt(qh, mk_ref[r, :, sl].astype(BF16))
            p = jnp.exp(s - jnp.max(s, axis=-1, keepdims=True))
            p = p / jnp.sum(p, axis=-1, keepdims=True)
            oh = jnp.where(row == r, _dot(p.astype(BF16), mv_ref[r, :, sl].astype(BF16)), oh)
        outs.append(oh)
    o_ref[...] = x + _dot(jnp.concatenate(outs, axis=1).astype(BF16), wo_ref[...].astype(BF16))


def _cross_attn_rows(a, b, w_o, x, g, wq, mk, mv, wo, *, layer, heads, dh):
    n_rows, d = x.shape
    kh = a.shape[1]
    mem = mk.shape[1]
    xd = heads * dh
    whole = lambda shape: pl.BlockSpec(shape, lambda i: (0,) * len(shape))
    return pl.pallas_call(
        functools.partial(_cross_attn_rows_kernel, heads=heads, dh=dh),
        out_shape=jax.ShapeDtypeStruct(x.shape, F32),
        grid=(1,),
        in_specs=[whole((n_rows, kh)), whole((n_rows, kh)),
                  pl.BlockSpec((None, kh, d), lambda i: (layer, 0, 0)),
                  pl.BlockSpec((None, kh, d), lambda i: (layer, 1, 0)),
                  whole((n_rows, d)), whole((1, d)),
                  pl.BlockSpec((None, d, xd), lambda i: (layer, 0, 0)),
                  whole((n_rows, mem, xd)), whole((n_rows, mem, xd)),
                  pl.BlockSpec((None, xd, d), lambda i: (layer, 0, 0))],
        out_specs=whole((n_rows, d)),
        compiler_params=_params("arbitrary"),
        name="cross_attn_rows",
    )(a, b, w_o, w_o, x, g, wq, mk, mv, wo)


def _swiglu_kernel(x_ref, xs_ref, g_ref, wg_ref, wu_ref, wd_ref, gf_ref, o_ref, os_ref, hb_ref, hs_ref, *, final_norm):
    i = pl.program_id(0)
    f = pl.program_id(1)
    last = pl.num_programs(1) - 1

    @pl.when(f == 0)
    def _():
        x = x_ref[...]
        hb_ref[...] = _rms(x, g_ref[...]).astype(BF16)
        o_ref[...] = x

    wg, wu, wd = wg_ref[...].astype(BF16), wu_ref[...].astype(BF16), wd_ref[...].astype(BF16)

    def ffn(hb):
        return _dot((_silu(_dot(hb, wg)) * _dot(hb, wu)).astype(BF16), wd)

    o_ref[...] += ffn(hb_ref[...])

    @pl.when(f == last)
    def _():
        if final_norm:
            o_ref[...] = _rms(o_ref[...], gf_ref[...])

    @pl.when(i == 0)
    def _():
        @pl.when(f == 0)
        def _():
            xs = xs_ref[...]
            hs_ref[...] = _rms(xs, g_ref[...]).astype(BF16)
            os_ref[...] = xs

        os_ref[...] += ffn(hs_ref[...])

        @pl.when(f == last)
        def _():
            if final_norm:
                os_ref[...] = _rms(os_ref[...], gf_ref[...])


def _swiglu(x, xs, g, wg, wu, wd, g_final, *, layer, final_norm, tm, tf):
    n, d = x.shape
    ns = xs.shape[0]
    ffn = wg.shape[2]
    return pl.pallas_call(
        functools.partial(_swiglu_kernel, final_norm=final_norm),
        out_shape=[jax.ShapeDtypeStruct((n, d), F32), jax.ShapeDtypeStruct((ns, d), F32)],
        grid=(n // tm, ffn // tf),
        in_specs=[pl.BlockSpec((tm, d), lambda i, f: (i, 0)),
                  pl.BlockSpec((ns, d), lambda i, f: (0, 0)),
                  pl.BlockSpec((1, d), lambda i, f: (0, 0)),
                  pl.BlockSpec((None, d, tf), lambda i, f: (layer, 0, f)),
                  pl.BlockSpec((None, d, tf), lambda i, f: (layer, 0, f)),
                  pl.BlockSpec((None, tf, d), lambda i, f: (layer, f, 0)),
                  pl.BlockSpec((1, d), lambda i, f: (0, 0))],
        out_specs=[pl.BlockSpec((tm, d), lambda i, f: (i, 0)),
                   pl.BlockSpec((ns, d), lambda i, f: (0, 0))],
        scratch_shapes=[pltpu.VMEM((tm, d), BF16), pltpu.VMEM((ns, d), BF16)],
        compiler_params=_params("arbitrary", "arbitrary"),
        name="swiglu",
    )(x, xs, g, wg, wu, wd, g_final)


def _suffix_rows_exclusive(x):
    n = x.shape[0]
    row = lax.broadcasted_iota(jnp.int32, x.shape, 0)

    def up(a, k):
        return jnp.where(row + k < n, pltpu.roll(a, n - k, 0), 0.0)

    y = up(x, 1)
    k = 1
    while k < n:
        y = y + up(y, k)
        k *= 2
    return y


def _fox_decode_kernel(pt_ref, q_ref, kn_ref, vn_ref, cq_ref, *rest, heads, group, scale):
    del pt_ref
    k_refs, v_refs, f_refs = rest[:group], rest[group:2 * group], rest[2 * group:3 * group]
    u_same_ref, u_later_ref, o_ref, m_ref, l_ref, acc_ref, tail_ref = rest[3 * group:]
    step = pl.program_id(1)

    @pl.when(step == 0)
    def _():
        m_ref[...] = jnp.full_like(m_ref, -jnp.inf)
        l_ref[...] = jnp.zeros_like(l_ref)
        acc_ref[...] = jnp.zeros_like(acc_ref)
        tail_ref[...] = jnp.zeros_like(tail_ref)

    n_blk = k_refs[0].shape[0] // LANES
    wide = MXU_COLS // LANES
    q = q_ref[...]
    qb = q.astype(BF16)
    sub = lax.broadcasted_iota(jnp.int32, (heads, MXU_COLS), 0)
    lane = lax.broadcasted_iota(jnp.int32, (heads, MXU_COLS), 1)
    own = sub == lane % heads
    parts = _split3(jnp.concatenate([f[...] for f in f_refs], axis=0))
    rows_later = sum(_dot(x, u_later_ref[...]) for x in parts)
    rows_total = sum(_dot(x, u_same_ref[...]) for x in parts)
    tail = tail_ref[...]
    tiles = []
    for g in range(group):
        row_total = rows_total[g * n_blk:(g + 1) * n_blk]
        bias = cq_ref[...] + tail + rows_later[g * n_blk:(g + 1) * n_blk] + _suffix_rows_exclusive(row_total)
        tail = tail + jnp.sum(row_total, axis=0, keepdims=True)
        for r in range(0, n_blk, wide):
            kb = k_refs[g][r * LANES:(r + wide) * LANES, :].astype(BF16)
            brow = jnp.concatenate([bias[r + w:r + w + 1, :] for w in range(wide)], axis=1)
            s = jnp.where(own, _dot_nt(qb, kb) * scale + brow, -jnp.inf)
            tiles.extend(s[:, w * LANES:(w + 1) * LANES] for w in range(wide))
    tail_ref[...] = tail

    top = tiles[0]
    for t in tiles[1:]:
        top = jnp.maximum(top, t)
    m_old = m_ref[...]
    m_new = jnp.maximum(m_old, jnp.max(top, axis=-1, keepdims=True))
    a = jnp.exp(m_old - m_new)
    accs = [a * acc_ref[...], jnp.zeros(acc_ref.shape, F32)]
    psum = jnp.zeros((heads, LANES), F32)
    dh = acc_ref.shape[1]
    for n, idx in enumerate(range(0, len(tiles), wide)):
        g, r = divmod(idx, n_blk)
        prs = [jnp.exp(tiles[idx + w] - m_new) for w in range(wide)]
        for pr in prs:
            psum = psum + pr
        vb = jnp.concatenate([v_refs[g][(r + w) * LANES:(r + w + 1) * LANES, :].astype(BF16) for w in range(wide)],
                             axis=1)
        res = _dot(jnp.concatenate(prs, axis=0).astype(BF16), vb)
        for w in range(wide):
            accs[n % 2] = accs[n % 2] + res[w * heads:(w + 1) * heads, w * dh:(w + 1) * dh]
    m_ref[...] = m_new
    l_ref[...] = a * l_ref[...] + jnp.sum(psum, axis=-1, keepdims=True)
    acc_ref[...] = accs[0] + accs[1]

    @pl.when(step == pl.num_programs(1) - 1)
    def _():
        s_new = jnp.sum(q * kn_ref[...], axis=-1, keepdims=True) * scale
        m_fin = jnp.maximum(m_ref[...], s_new)
        a_fin = jnp.exp(m_ref[...] - m_fin)
        p_new = jnp.exp(s_new - m_fin)
        o_ref[...] = ((a_fin * acc_ref[...] + p_new * vn_ref[...]) / (a_fin * l_ref[...] + p_new)).astype(o_ref.dtype)


def _fox_decode(q, k_new, v_new, cq, cache_k, cache_v, cache_f, page_table, *, layer, group):
    bsz, heads, dh = q.shape
    rows = cache_k.shape[2]
    n_pages = page_table.shape[1]
    lane = np.arange(LANES)
    same = lane[:, None] % heads == lane[None, :] % heads
    u_same = jnp.asarray(same, BF16)
    u_later = jnp.asarray(same & (lane[:, None] > lane[None, :]), BF16)
    vec = pl.BlockSpec((None, heads, dh), lambda b, s, pt: (b, 0, 0))

    def paged(shape, g):
        return pl.BlockSpec((None, None) + shape, lambda b, s, pt: (layer, pt[b, n_pages - 1 - (s * group + g)], 0, 0))

    const = pl.BlockSpec((LANES, LANES), lambda b, s, pt: (0, 0))
    grid_spec = pltpu.PrefetchScalarGridSpec(
        num_scalar_prefetch=1,
        grid=(bsz, n_pages // group),
        in_specs=([vec, vec, vec, pl.BlockSpec((None, 1, LANES), lambda b, s, pt: (b, 0, 0))]
                  + [paged((rows, dh), g) for g in range(group)]
                  + [paged((rows, dh), g) for g in range(group)]
                  + [paged((rows // LANES, LANES), g) for g in range(group)]
                  + [const, const]),
        out_specs=vec,
        scratch_shapes=[pltpu.VMEM((heads, 1), F32), pltpu.VMEM((heads, 1), F32),
                        pltpu.VMEM((heads, dh), F32), pltpu.VMEM((1, LANES), F32)],
    )
    return pl.pallas_call(
        functools.partial(_fox_decode_kernel, heads=heads, group=group, scale=dh ** -0.5),
        out_shape=jax.ShapeDtypeStruct((bsz, heads, dh), BF16),
        grid_spec=grid_spec,
        compiler_params=_params("arbitrary", "arbitrary"),
        name="fox_decode",
    )(page_table, q, k_new, v_new, cq, *([cache_k] * group), *([cache_v] * group), *([cache_f] * group),
      u_same, u_later)


def kernel(x_prompt, x_sample, mem_prompt, cache_fox_k, cache_fox_v, cache_fox_logf, state_gla, cache_mem_k, cache_mem_v, page_table, g_mix, w_in, b_forget, w_gla_gate, b_gla_gate, g_gla_head, w_out, g_cross, g_mem, w_xq, w_xk, w_xv, w_xo, g_ffn, w_ffn_gate, w_ffn_up, w_ffn_down, g_final):
    depth = w_in.shape[0]
    bsz, seq, d = x_prompt.shape
    sb = x_sample.shape[0]
    mem = mem_prompt.shape[1]
    _, pool, page, fh, fdh = cache_fox_k.shape
    _, _, gh, gdk, gdv = state_gla.shape
    xh, xdh = cache_mem_k.shape[3], cache_mem_k.shape[4]
    fox_dim, gk_dim, gv_dim = fh * fdh, gh * gdk, gh * gdv
    rank = w_gla_gate.shape[1]
    n = bsz * seq
    c_forget = 3 * fox_dim
    c_gla = c_forget + fh
    c_rank = c_gla + 2 * gk_dim + 2 * gv_dim

    xp = x_prompt.reshape(n, d)
    xs = x_sample.reshape(sb, d)
    memx = mem_prompt.reshape(bsz * mem, d)
    ck = cache_fox_k.reshape(depth, pool, page * fh, fdh)
    cv = cache_fox_v.reshape(depth, pool, page * fh, fdh)
    cf = cache_fox_logf.reshape(depth, pool, page * fh // LANES, LANES)
    row = lambda v: v.reshape(1, -1)
    zeros_state = jnp.zeros((bsz, gh, gdk, gdv), F32)
    group = min(DECODE_PAGES_PER_STEP, page_table.shape[1])

    w_t = jnp.swapaxes(w_in, 1, 2)
    w_main, w_small = _w_main(w_t, c_a=c_forget, c_b0=c_gla, c_b1=c_rank, tr=W_PREP_ROWS)
    w_g2 = jnp.concatenate([jnp.zeros((depth, fh, gk_dim), F32), w_gla_gate,
                            jnp.zeros((depth, LANES - fh - rank, gk_dim), F32)], axis=1).astype(BF16)
    b_f = jnp.concatenate([b_forget, jnp.zeros((depth, LANES - fh), F32)], axis=1)
    w_kv = jnp.concatenate([w_xk, w_xv], axis=2)
    w_o = w_out.astype(BF16)

    pk, pv, pf, ps, pmk, pmv, sk, sv, sf, ss = ([] for _ in range(10))
    for l in range(depth):
        last = l == depth - 1

        z, logf, ga, k3, v3, drow = _in_proj(xp, row(g_mix[l]), w_main, w_small, b_f[l:l + 1], w_g2,
                                             row(b_gla_gate[l]), layer=l, n_fox_heads=fh, fox_dh=fdh, seq=seq,
                                             tm=min(ROW_TILE_PROJ, seq), tn=COL_TILE)
        o_f = _fox_prompt(z, drow, batch=bsz, seq=seq, heads=fh, dh=fdh, tq=min(FOX_Q_TILE, seq),
                          hp=FOX_HEADS_PER_STEP)
        o_g, s_fin = _gla(z, ga, zeros_state, row(g_gla_head[l]), batch=bsz, seq=seq, chunk=min(GLA_CHUNK, seq),
                          heads=gh, dk=gdk, dv=gdv, col0=3 * fox_dim)
        mkv = _norm_matmul(memx, row(g_mem[l]), w_kv, layer=l, tm=bsz * mem, tn=COL_TILE)
        mk, mv = mkv[:, :xh * xdh], mkv[:, xh * xdh:]
        xp = _cross_attn(o_f.reshape(bsz, seq, -1), o_g.reshape(bsz, seq, -1), w_o, xp.reshape(bsz, seq, d),
                         row(g_cross[l]), w_xq, mk.reshape(bsz, mem, -1), mv.reshape(bsz, mem, -1), w_xo, layer=l,
                         heads=xh, dh=xdh, tm=min(ROW_TILE_XATTN, seq)).reshape(n, d)
        pk.append(k3.reshape(bsz, seq, fh, fdh))
        pv.append(v3.reshape(bsz, seq, fh, fdh))
        pf.append(logf.reshape(bsz, seq, fh))
        ps.append(s_fin)
        pmk.append(mk.reshape(bsz, mem, xh, xdh))
        pmv.append(mv.reshape(bsz, mem, xh, xdh))

        zs, logf_s, ga_s, fk, fv = _in_proj(xs, row(g_mix[l]), w_main, w_small, b_f[l:l + 1], w_g2,
                                            row(b_gla_gate[l]), layer=l, n_fox_heads=fh, fox_dh=fdh, seq=0,
                                            tm=sb, tn=COL_TILE)
        fq, fk, fv = zs[:, :fox_dim].reshape(sb, fh, fdh), fk.reshape(sb, fh, fdh), fv.reshape(sb, fh, fdh)
        cq = jnp.tile(logf_s, (1, LANES // fh)).reshape(sb, 1, LANES)
        o_fs = _fox_decode(fq, fk, fv, cq, ck, cv, cf, page_table, layer=l, group=group).reshape(sb, fox_dim)
        cs = GLA_CHUNK_SAMPLE
        pad = lambda a: jnp.zeros((sb, cs, a.shape[1]), a.dtype).at[:, 0].set(a).reshape(sb * cs, -1)
        o_gs, s_new = _gla(pad(zs), pad(ga_s), state_gla[l], row(g_gla_head[l]), batch=sb, seq=cs, chunk=cs,
                           heads=gh, dk=gdk, dv=gdv, col0=3 * fox_dim)
        o_gs = o_gs.reshape(sb, cs, gv_dim)[:, 0]
        xs = _cross_attn_rows(o_fs, o_gs, w_o, xs, row(g_cross[l]), w_xq, cache_mem_k[l].reshape(sb, mem, -1),
                              cache_mem_v[l].reshape(sb, mem, -1), w_xo, layer=l, heads=xh, dh=xdh)
        xp, xs = _swiglu(xp, xs, row(g_ffn[l]), w_ffn_gate, w_ffn_up, w_ffn_down, row(g_final), layer=l,
                         final_norm=last, tm=min(ROW_TILE_FFN, n), tf=FFN_TILE)
        sk.append(fk.reshape(sb, 1, fh, fdh))
        sv.append(fv.reshape(sb, 1, fh, fdh))
        sf.append(logf_s.reshape(sb, 1, fh))
        ss.append(s_new)

    return (xp.reshape(bsz, seq, d), xs.reshape(sb, 1, d),
            jnp.stack(pk), jnp.stack(pv), jnp.stack(pf), jnp.stack(ps), jnp.stack(pmk), jnp.stack(pmv),
            jnp.stack(sk), jnp.stack(sv), jnp.stack(sf), jnp.stack(ss))
```
